```python
import math
import jax, jax.numpy as jnp
from jax import lax
import numpy as np

D_MODEL = 1024
BATCH = 16
SEQ = 2048
DEPTH = 2

HEAD_DIM = 64
EPS = 1e-6
NEG = -1e30
A_HEADS = 8
A_PATTERNS = ((128, 1), (512, 4), (2048, 16))
A_BLOCK = 128
B_HEADS = 4
B_VDIM = 2 * HEAD_DIM
Q_BLOCK = 128
ATT_IN = 3 * A_HEADS * HEAD_DIM + 2 * (2 * B_HEADS * HEAD_DIM) + B_HEADS * B_VDIM
ATT_OUT = A_HEADS * HEAD_DIM + B_HEADS * B_VDIM
C_CHUNK = 128
C_GROUPS = 8
C_WIDTH = 2 * D_MODEL
C_GROUP_DIM = C_WIDTH // C_GROUPS
PEER_HEADS = 8
PEER_NKEYS = 128
PEER_EXPERTS = PEER_NKEYS * PEER_NKEYS
PEER_TOPK = 16
PEER_DKEY = 256
PEER_TOKEN_BLOCK = 128

kernel_name = 'hybrid_dilated_diff_sgu_peer'


def rms_norm(x, g):
    xf = x.astype(jnp.float32)
    y = xf * lax.rsqrt(jnp.mean(xf * xf, axis=-1, keepdims=True) + EPS)
    return (y * g.astype(jnp.float32)).astype(x.dtype)


def dilated_branch(q, k, v, window, dilation):
    b, s, h, dh = q.shape
    L = s // dilation
    span = window // dilation
    nb = -(-L // A_BLOCK)
    Lp = nb * A_BLOCK

    def to_sub(t):
        t = t.reshape(b, L, dilation, h, dh).transpose(0, 2, 3, 1, 4)
        t = jnp.pad(t, ((0, 0), (0, 0), (0, 0), (0, Lp - L), (0, 0)))
        return t.reshape(b, dilation, h, nb, A_BLOCK, dh)

    def with_prev(t):
        prev = jnp.pad(t, ((0, 0), (0, 0), (0, 0), (1, 0), (0, 0), (0, 0)))[:, :, :, :-1]
        return jnp.concatenate([prev, t], axis=4)

    qb = to_sub(q)
    kw = with_prev(to_sub(k))
    vw = with_prev(to_sub(v))
    scores = jnp.einsum('brhnqd,brhnkd->brhnqk', qb, kw).astype(jnp.float32) * (dh ** -0.5)
    qi = jnp.arange(nb)[:, None, None] * A_BLOCK + jnp.arange(A_BLOCK)[None, :, None]
    ki = (jnp.arange(nb)[:, None, None] - 1) * A_BLOCK + jnp.arange(2 * A_BLOCK)[None, None, :]
    dist = qi - ki
    valid = (dist >= 0) & (dist <= span) & (ki >= 0)
    scores = jnp.where(valid, scores, NEG)
    m = jnp.max(scores, axis=-1, keepdims=True)
    p = jnp.exp(scores - m)
    den = jnp.sum(p, axis=-1, keepdims=True)
    out = jnp.einsum('brhnqk,brhnkd->brhnqd', p, vw.astype(jnp.float32)) / den
    lse = (m + jnp.log(den))[..., 0]
    out = out.reshape(b, dilation, h, Lp, dh)[:, :, :, :L].transpose(0, 3, 1, 2, 4).reshape(b, s, h, dh)
    lse = lse.reshape(b, dilation, h, Lp)[:, :, :, :L].transpose(0, 3, 1, 2).reshape(b, s, h)
    return out, lse


def diff_attention(q1, q2, k1, k2, v, lam):
    b, s, h, dh = q1.shape
    nq = s // Q_BLOCK
    scale = dh ** -0.5
    kpos = jnp.arange(s)
    vf = v.astype(jnp.float32)

    def blockify(t):
        return t.reshape(b, nq, Q_BLOCK, h, dh).transpose(1, 0, 2, 3, 4)

    def one_block(args):
        qb1, qb2, start = args
        qpos = start + jnp.arange(Q_BLOCK)
        mask = kpos[None, :] <= qpos[:, None]

        def probs(qb, kk):
            sc = jnp.einsum('bqhd,bkhd->bhqk', qb, kk).astype(jnp.float32) * scale
            return jax.nn.softmax(jnp.where(mask, sc, NEG), axis=-1)

        a = probs(qb1, k1) - lam * probs(qb2, k2)
        return jnp.einsum('bhqk,bkhe->bqhe', a, vf)

    starts = jnp.arange(nq) * Q_BLOCK
    out = lax.map(one_block, (blockify(q1), blockify(q2), starts))
    return out.transpose(1, 0, 2, 3, 4).reshape(b, s, h, 2 * dh)


def attention_layer(x, norm_g, w_in, a_q_gain, a_k_gain, b_q_gain, b_k_gain,
                    lam_q1, lam_k1, lam_q2, lam_k2, b_sub_gain, w_out, lambda_init):
    b, s, _ = x.shape
    proj = rms_norm(x, norm_g) @ w_in
    aw = A_HEADS * HEAD_DIM
    bw = 2 * B_HEADS * HEAD_DIM
    aq, ak, av, bq, bk, bv = jnp.split(proj, [aw, 2 * aw, 3 * aw, 3 * aw + bw, 3 * aw + 2 * bw], axis=-1)
    aq = rms_norm(aq.reshape(b, s, A_HEADS, HEAD_DIM), a_q_gain)
    ak = rms_norm(ak.reshape(b, s, A_HEADS, HEAD_DIM), a_k_gain)
    av = av.reshape(b, s, A_HEADS, HEAD_DIM)
    outs, lses = [], []
    for window, dilation in A_PATTERNS:
        o, l = dilated_branch(aq, ak, av, window, dilation)
        outs.append(o)
        lses.append(l)
    wts = jax.nn.softmax(jnp.stack(lses), axis=0)
    out_a = jnp.sum(wts[..., None] * jnp.stack(outs), axis=0)
    bq = rms_norm(bq.reshape(b, s, B_HEADS, 2, HEAD_DIM), b_q_gain)
    bk = rms_norm(bk.reshape(b, s, B_HEADS, 2, HEAD_DIM), b_k_gain)
    bv = bv.reshape(b, s, B_HEADS, B_VDIM)
    lam = (jnp.exp(jnp.sum((lam_q1 * lam_k1).astype(jnp.float32)))
           - jnp.exp(jnp.sum((lam_q2 * lam_k2).astype(jnp.float32))) + lambda_init)
    out_b = diff_attention(bq[..., 0, :], bq[..., 1, :], bk[..., 0, :], bk[..., 1, :], bv, lam)
    out_b = rms_norm(out_b, b_sub_gain) * (1.0 - lambda_init)
    mixed = jnp.concatenate([out_a.reshape(b, s, aw), out_b.reshape(b, s, B_HEADS * B_VDIM)], axis=-1)
    return x + mixed.astype(x.dtype) @ w_out


def sgu_layer(x, norm_g, w_in, v_gain, w_spatial, b_spatial, w_out):
    b, s, _ = x.shape
    nc = s // C_CHUNK
    z = jax.nn.gelu(rms_norm(x, norm_g) @ w_in)
    u, v = jnp.split(z, 2, axis=-1)
    v = rms_norm(v, v_gain).reshape(b, nc, C_CHUNK, C_GROUPS, C_GROUP_DIM)
    causal = jnp.tril(jnp.ones((C_CHUNK, C_CHUNK), dtype=bool))
    ws = jnp.where(causal, w_spatial, 0.0)
    gate = jnp.einsum('gts,bnsgc->bntgc', ws, v) + b_spatial.T[None, None, :, :, None]
    y = (u * gate.reshape(b, s, C_WIDTH)) @ w_out
    return x + y.astype(x.dtype)


def peer_layer(x, norm_g, w_query, sub_keys, expert_down, expert_up):
    b, s, d = x.shape
    hn = rms_norm(x, norm_g).reshape(b * s // PEER_TOKEN_BLOCK, PEER_TOKEN_BLOCK, d)

    def block(xb):
        q = (xb @ w_query).reshape(-1, PEER_HEADS, 2, PEER_DKEY // 2)
        sc = jnp.einsum('thpc,hpkc->thpk', q, sub_keys).astype(jnp.float32)
        s1, i1 = lax.top_k(sc[:, :, 0], PEER_TOPK)
        s2, i2 = lax.top_k(sc[:, :, 1], PEER_TOPK)
        cand = (s1[..., :, None] + s2[..., None, :]).reshape(-1, PEER_HEADS, PEER_TOPK * PEER_TOPK)
        top, ci = lax.top_k(cand, PEER_TOPK)
        e1 = jnp.take_along_axis(i1, ci // PEER_TOPK, axis=-1)
        e2 = jnp.take_along_axis(i2, ci % PEER_TOPK, axis=-1)
        idx = e1 * PEER_NKEYS + e2
        g = jax.nn.softmax(top, axis=-1)
        hid = jax.nn.gelu(jnp.einsum('td,thkd->thk', xb, expert_down[idx]).astype(jnp.float32))
        return jnp.einsum('thk,thkd->td', (g * hid).astype(xb.dtype), expert_up[idx])

    y = lax.map(block, hn).reshape(b, s, d)
    return x + y.astype(x.dtype)


def setup_inputs(seed: int = 0) -> dict:
    key = jax.random.key(seed)
    keys = list(jax.random.split(key, 32))
    ne = (DEPTH + 1) // 2
    no = DEPTH // 2

    def nrm(shape, scale):
        return jax.random.normal(keys.pop(), shape, jnp.float32) * scale

    def gain(shape):
        return 1.0 + nrm(shape, 0.02)

    return {
        'x': nrm((BATCH, SEQ, D_MODEL), 1.0),
        'attn_norm_g': gain((ne, D_MODEL)),
        'attn_w_in': nrm((ne, D_MODEL, ATT_IN), D_MODEL ** -0.5),
        'a_q_gain': gain((ne, HEAD_DIM)),
        'a_k_gain': gain((ne, HEAD_DIM)),
        'b_q_gain': gain((ne, 2, HEAD_DIM)),
        'b_k_gain': gain((ne, 2, HEAD_DIM)),
        'lam_q1': nrm((ne, HEAD_DIM), 0.1),
        'lam_k1': nrm((ne, HEAD_DIM), 0.1),
        'lam_q2': nrm((ne, HEAD_DIM), 0.1),
        'lam_k2': nrm((ne, HEAD_DIM), 0.1),
        'b_sub_gain': gain((ne, B_VDIM)),
        'attn_w_out': nrm((ne, ATT_OUT, D_MODEL), ATT_OUT ** -0.5),
        'sgu_norm_g': gain((no, D_MODEL)),
        'sgu_w_in': nrm((no, D_MODEL, 2 * C_WIDTH), D_MODEL ** -0.5),
        'sgu_v_gain': gain((no, C_WIDTH)),
        'sgu_w_spatial': nrm((no, C_GROUPS, C_CHUNK, C_CHUNK), 0.5 * C_CHUNK ** -0.5),
        'sgu_b_spatial': 1.0 + nrm((no, C_GROUPS, C_CHUNK), 0.1),
        'sgu_w_out': nrm((no, C_WIDTH, D_MODEL), C_WIDTH ** -0.5),
        'ffn_norm_g': gain((DEPTH, D_MODEL)),
        'peer_w_query': nrm((DEPTH, D_MODEL, PEER_HEADS * PEER_DKEY), D_MODEL ** -0.5),
        'peer_sub_keys': nrm((DEPTH, PEER_HEADS, 2, PEER_NKEYS, PEER_DKEY // 2), (PEER_DKEY // 2) ** -0.5),
        'peer_down': nrm((DEPTH, PEER_EXPERTS, D_MODEL), D_MODEL ** -0.5),
        'peer_up': nrm((DEPTH, PEER_EXPERTS, D_MODEL), PEER_HEADS ** -0.5),
    }


def reference(x, attn_norm_g, attn_w_in, a_q_gain, a_k_gain, b_q_gain, b_k_gain,
              lam_q1, lam_k1, lam_q2, lam_k2, b_sub_gain, attn_w_out,
              sgu_norm_g, sgu_w_in, sgu_v_gain, sgu_w_spatial, sgu_b_spatial, sgu_w_out,
              ffn_norm_g, peer_w_query, peer_sub_keys, peer_down, peer_up):
    for layer in range(DEPTH):
        i = layer // 2
        if layer % 2 == 0:
            lambda_init = 0.8 - 0.6 * math.exp(-0.3 * layer)
            x = attention_layer(x, attn_norm_g[i], attn_w_in[i], a_q_gain[i], a_k_gain[i],
                                b_q_gain[i], b_k_gain[i], lam_q1[i], lam_k1[i], lam_q2[i], lam_k2[i],
                                b_sub_gain[i], attn_w_out[i], lambda_init)
        else:
            x = sgu_layer(x, sgu_norm_g[i], sgu_w_in[i], sgu_v_gain[i], sgu_w_spatial[i],
                          sgu_b_spatial[i], sgu_w_out[i])
        x = peer_layer(x, ffn_norm_g[layer], peer_w_query[layer], peer_sub_keys[layer],
                       peer_down[layer], peer_up[layer])
    return x
```

```python
import functools
import math

import jax
import jax.numpy as jnp
from jax import lax
from jax.experimental import pallas as pl
from jax.experimental.pallas import tpu as pltpu

D_MODEL = 1024
HEAD_DIM = 64
EPS = 1e-6
NEG = -1e30
A_HEADS = 8
B_HEADS = 4
C_CHUNK = 128
C_GROUPS = 8
C_WIDTH = 2 * D_MODEL
C_GROUP_DIM = C_WIDTH // C_GROUPS
PEER_HEADS = 8
PEER_NKEYS = 128
PEER_TOPK = 16
PEER_PICKS = PEER_HEADS * PEER_TOPK

LANES = 128
HALF = D_MODEL // 2
VMEM_LIMIT = 56 * 1024 * 1024

BF16 = jnp.bfloat16
F32 = jnp.float32


def _gelu(x):
    return 0.5 * x * (1.0 + jnp.tanh(math.sqrt(2.0 / math.pi) * (x + 0.044715 * (x * x * x))))


def _rms(x, g):
    return x * lax.rsqrt(jnp.mean(x * x, axis=-1, keepdims=True) + EPS) * g


def _dot_nt(a, b):
    return lax.dot_general(a, b, (((1,), (1,)), ((), ())), preferred_element_type=F32)


def _norm_matmul_kernel(x_ref, g_ref, w_ref, o_ref, xn_ref):
    @pl.when(pl.program_id(1) == 0)
    def _():
        xn_ref[...] = _rms(x_ref[...], g_ref[...]).astype(BF16)

    o_ref[...] = jnp.dot(xn_ref[...], w_ref[...], preferred_element_type=F32)


def norm_matmul(x, g, w, *, tm=512, tn=512):
    n, d = x.shape
    nout = w.shape[1]
    tm = min(tm, n)
    return pl.pallas_call(
        _norm_matmul_kernel,
        grid=(n // tm, nout // tn),
        in_specs=[
            pl.BlockSpec((tm, d), lambda i, j: (i, 0)),
            pl.BlockSpec((1, d), lambda i, j: (0, 0)),
            pl.BlockSpec((d, tn), lambda i, j: (0, j)),
        ],
        out_specs=pl.BlockSpec((tm, tn), lambda i, j: (i, j)),
        out_shape=jax.ShapeDtypeStruct((n, nout), F32),
        scratch_shapes=[pltpu.VMEM((tm, d), BF16)],
        compiler_params=pltpu.CompilerParams(
            dimension_semantics=("arbitrary", "arbitrary"), vmem_limit_bytes=VMEM_LIMIT),
        name="norm_matmul",
    )(x, g.reshape(1, d), w.astype(BF16))


def _pair_norm(t, gain, lo):
    sq = t * t
    s_lo = jnp.sum(jnp.where(lo, sq, 0.0), axis=-1, keepdims=True)
    s_hi = jnp.sum(jnp.where(lo, 0.0, sq), axis=-1, keepdims=True)
    ms = jnp.where(lo, s_lo, s_hi) * (1.0 / HEAD_DIM)
    return t * lax.rsqrt(ms + EPS) * gain


def _attn_kernel(*refs, mode, tq, tk, lambda_init):
    if mode == "dilated":
        q_ref, k_ref, v_ref, qg_ref, kg_ref, o_ref, kn_ref = refs
    else:
        (q_ref, k_ref, v_ref, qg_ref, kg_ref, lq1_ref, lk1_ref, lq2_ref, lk2_ref, sg_ref,
         o_ref, kn_ref) = refs
    i = pl.program_id(2)
    lo = lax.broadcasted_iota(jnp.int32, (1, LANES), 1) < HEAD_DIM

    @pl.when(i == 0)
    def _():
        kn_ref[...] = _pair_norm(k_ref[...], kg_ref[...], lo).astype(BF16)

    qn = _pair_norm(q_ref[...], qg_ref[...], lo) * (HEAD_DIM ** -0.5)
    qa = jnp.where(lo, qn, 0.0).astype(BF16)
    qb = jnp.where(lo, 0.0, qn).astype(BF16)
    rows = i * tq + lax.broadcasted_iota(jnp.int32, (tq, tk), 0)
    col0 = lax.broadcasted_iota(jnp.int32, (tq, tk), 1)

    def body(j, carry):
        ma, la, acca, mb, lb, accb = carry
        off = pl.multiple_of(j * tk, tk)
        kb = kn_ref[pl.ds(off, tk), :]
        vb = v_ref[pl.ds(off, tk), :].astype(BF16)
        dist = rows - (col0 + j * tk)
        if mode == "dilated":
            cnt = ((dist <= 128).astype(F32)
                   + ((dist <= 512) & ((dist & 3) == 0)).astype(F32)
                   + ((dist & 15) == 0).astype(F32))
            cnt = jnp.where(dist >= 0, cnt, 0.0)
        else:
            cnt = (dist >= 0).astype(F32)
        valid = cnt > 0.0

        def update(qh, m, l, acc):
            s = jnp.where(valid, _dot_nt(qh, kb), NEG)
            m_new = jnp.maximum(m, jnp.max(s, axis=-1, keepdims=True))
            alpha = jnp.exp(m - m_new)
            p = cnt * jnp.exp(s - m_new)
            l_new = alpha * l + jnp.sum(p, axis=-1, keepdims=True)
            acc_new = alpha * acc + jnp.dot(p.astype(BF16), vb, preferred_element_type=F32)
            return m_new, l_new, acc_new

        ma, la, acca = update(qa, ma, la, acca)
        mb, lb, accb = update(qb, mb, lb, accb)
        return ma, la, acca, mb, lb, accb

    m0 = jnp.full((tq, 1), NEG, F32)
    l0 = jnp.zeros((tq, 1), F32)
    a0 = jnp.zeros((tq, LANES), F32)
    nkv = (i * tq + tq + tk - 1) // tk
    ma, la, acca, mb, lb, accb = lax.fori_loop(0, nkv, body, (m0, l0, a0, m0, l0, a0))
    oa = acca / la
    ob = accb / lb
    if mode == "dilated":
        o_ref[...] = jnp.where(lo, oa, ob)
    else:
        lam = (jnp.exp(jnp.sum(lq1_ref[...] * lk1_ref[...], axis=-1, keepdims=True))
               - jnp.exp(jnp.sum(lq2_ref[...] * lk2_ref[...], axis=-1, keepdims=True))
               + lambda_init)
        o = oa - lam * ob
        o_ref[...] = _rms(o, sg_ref[...]) * (1.0 - lambda_init)


def pair_attention(proj, batch, seq, mode, qcol, kcol, vcol, npairs, q_gain, k_gain, extras=(),
                   lambda_init=0.0, tq=256, tk=256):
    nq = seq // tq
    small = [q_gain.reshape(1, LANES), k_gain.reshape(1, LANES)] + [e.reshape(1, -1) for e in extras]
    small_specs = [pl.BlockSpec(s.shape, lambda b, p, i: (0, 0)) for s in small]
    kernel = functools.partial(_attn_kernel, mode=mode, tq=tq, tk=tk, lambda_init=lambda_init)
    return pl.pallas_call(
        kernel,
        grid=(batch, npairs, nq),
        in_specs=[
            pl.BlockSpec((tq, LANES), lambda b, p, i: (b * nq + i, qcol + p)),
            pl.BlockSpec((seq, LANES), lambda b, p, i: (b, kcol + p)),
            pl.BlockSpec((seq, LANES), lambda b, p, i: (b, vcol + p)),
        ] + small_specs,
        out_specs=pl.BlockSpec((tq, LANES), lambda b, p, i: (b * nq + i, p)),
        out_shape=jax.ShapeDtypeStruct((batch * seq, npairs * LANES), F32),
        scratch_shapes=[pltpu.VMEM((seq, LANES), BF16)],
        compiler_params=pltpu.CompilerParams(
            dimension_semantics=("arbitrary", "arbitrary", "arbitrary"),
            vmem_limit_bytes=VMEM_LIMIT),
        name="attn_" + mode,
    )(proj, proj, proj, *small)


def _out_proj_kernel(x_ref, a_ref, b_ref, wa_ref, wb_ref, o_ref):
    o_ref[...] = (x_ref[...]
                  + jnp.dot(a_ref[...].astype(BF16), wa_ref[...], preferred_element_type=F32)
                  + jnp.dot(b_ref[...].astype(BF16), wb_ref[...], preferred_element_type=F32))


def out_proj(x, a, b, w, *, tm=512):
    n, d = x.shape
    ka, kb = a.shape[1], b.shape[1]
    w = w.astype(BF16)
    tm = min(tm, n)
    return pl.pallas_call(
        _out_proj_kernel,
        grid=(n // tm,),
        in_specs=[
            pl.BlockSpec((tm, d), lambda i: (i, 0)),
            pl.BlockSpec((tm, ka), lambda i: (i, 0)),
            pl.BlockSpec((tm, kb), lambda i: (i, 0)),
            pl.BlockSpec((ka, d), lambda i: (0, 0)),
            pl.BlockSpec((kb, d), lambda i: (0, 0)),
        ],
        out_specs=pl.BlockSpec((tm, d), lambda i: (i, 0)),
        out_shape=jax.ShapeDtypeStruct((n, d), F32),
        compiler_params=pltpu.CompilerParams(
            dimension_semantics=("arbitrary",), vmem_limit_bytes=VMEM_LIMIT),
        name="out_proj",
    )(x, a, b, w[:ka], w[ka:])


def _sgu_kernel(x_ref, g_ref, win_ref, vg_ref, ws_ref, bs_ref, wout_ref, o_ref, gated_ref, *, tm):
    x = x_ref[...]
    xn = _rms(x, g_ref[...]).astype(BF16)
    z = _gelu(jnp.dot(xn, win_ref[...], preferred_element_type=F32))
    u = z[:, :C_WIDTH]
    v = _rms(z[:, C_WIDTH:], vg_ref[...]).astype(BF16)
    r = lax.broadcasted_iota(jnp.int32, (C_CHUNK, C_CHUNK), 0)
    c = lax.broadcasted_iota(jnp.int32, (C_CHUNK, C_CHUNK), 1)
    causal = c <= r
    for grp in range(C_GROUPS):
        ws = jnp.where(causal, ws_ref[grp], 0.0).astype(BF16)
        bias = bs_ref[:, grp:grp + 1]
        cols = slice(grp * C_GROUP_DIM, (grp + 1) * C_GROUP_DIM)
        for ch in range(tm // C_CHUNK):
            rws = slice(ch * C_CHUNK, (ch + 1) * C_CHUNK)
            gate = jnp.dot(ws, v[rws, cols], preferred_element_type=F32) + bias
            gated_ref[rws, cols] = (u[rws, cols] * gate).astype(BF16)
    o_ref[...] = x + jnp.dot(gated_ref[...], wout_ref[...], preferred_element_type=F32)


def sgu_layer(x, norm_g, w_in, v_gain, w_spatial, b_spatial, w_out, *, tm=256):
    n, d = x.shape
    kernel = functools.partial(_sgu_kernel, tm=tm)
    return pl.pallas_call(
        kernel,
        grid=(n // tm,),
        in_specs=[
            pl.BlockSpec((tm, d), lambda i: (i, 0)),
            pl.BlockSpec((1, d), lambda i: (0, 0)),
            pl.BlockSpec((d, 2 * C_WIDTH), lambda i: (0, 0)),
            pl.BlockSpec((1, C_WIDTH), lambda i: (0, 0)),
            pl.BlockSpec((C_GROUPS, C_CHUNK, C_CHUNK), lambda i: (0, 0, 0)),
            pl.BlockSpec((C_CHUNK, C_GROUPS), lambda i: (0, 0)),
            pl.BlockSpec((C_WIDTH, d), lambda i: (0, 0)),
        ],
        out_specs=pl.BlockSpec((tm, d), lambda i: (i, 0)),
        out_shape=jax.ShapeDtypeStruct((n, d), F32),
        scratch_shapes=[pltpu.VMEM((tm, C_WIDTH), BF16)],
        compiler_params=pltpu.CompilerParams(
            dimension_semantics=("arbitrary",), vmem_limit_bytes=VMEM_LIMIT),
        name="sgu",
    )(x, norm_g.reshape(1, d), w_in.astype(BF16), v_gain.reshape(1, C_WIDTH), w_spatial,
      b_spatial.T, w_out.astype(BF16))


def _peer_scores_kernel(x_ref, g_ref, wq_ref, sk_ref, hn_ref, sc_ref):
    hn = _rms(x_ref[...], g_ref[...])
    hn_ref[...] = hn
    q = jnp.dot(hn.astype(BF16), wq_ref[...], preferred_element_type=F32)
    for hp in range(2 * PEER_HEADS):
        cols = slice(hp * PEER_NKEYS, (hp + 1) * PEER_NKEYS)
        sc_ref[:, cols] = _dot_nt(q[:, cols].astype(BF16), sk_ref[hp].astype(BF16))


def peer_scores(x, norm_g, w_query, sub_keys, *, tm=512):
    n, d = x.shape
    nq = w_query.shape[1]
    tm = min(tm, n)
    sk = sub_keys.reshape(2 * PEER_HEADS, PEER_NKEYS, PEER_NKEYS)
    return pl.pallas_call(
        _peer_scores_kernel,
        grid=(n // tm,),
        in_specs=[
            pl.BlockSpec((tm, d), lambda i: (i, 0)),
            pl.BlockSpec((1, d), lambda i: (0, 0)),
            pl.BlockSpec((d, nq), lambda i: (0, 0)),
            pl.BlockSpec(sk.shape, lambda i: (0, 0, 0)),
        ],
        out_specs=[pl.BlockSpec((tm, d), lambda i: (i, 0)),
                   pl.BlockSpec((tm, nq), lambda i: (i, 0))],
        out_shape=[jax.ShapeDtypeStruct((n, d), F32), jax.ShapeDtypeStruct((n, nq), F32)],
        compiler_params=pltpu.CompilerParams(
            dimension_semantics=("arbitrary",), vmem_limit_bytes=VMEM_LIMIT),
        name="peer_scores",
    )(x, norm_g.reshape(1, d), w_query.astype(BF16), sk)


def _peer_topk_kernel(sc_ref, idx_ref, gate_ref):
    tm = sc_ref.shape[0]
    lane_i = lax.broadcasted_iota(jnp.int32, (tm, LANES), 1)
    lane = lane_i.astype(F32)
    grp = lane_i // PEER_TOPK
    sub = lane_i % PEER_TOPK
    ninf = jnp.float32(-jnp.inf)

    def head(h, carry):
        idx_acc, top_acc, max_acc, den_acc = carry

        def extract(part):
            off = pl.multiple_of((2 * h + part) * PEER_NKEYS, PEER_NKEYS)
            s = sc_ref[:, pl.ds(off, PEER_NKEYS)]
            vals, ids = [], []
            for _ in range(PEER_TOPK):
                m = jnp.max(s, axis=-1, keepdims=True)
                am = jnp.min(jnp.where(s == m, lane, float(PEER_NKEYS)), axis=-1, keepdims=True)
                s = jnp.where(lane == am, ninf, s)
                vals.append(m)
                ids.append(am)
            return vals, ids

        v1, i1 = extract(0)
        v2, i2 = extract(1)
        s2t = jnp.zeros((tm, LANES), F32)
        i2t = jnp.zeros((tm, LANES), F32)
        for j in range(PEER_TOPK):
            s2t = jnp.where(sub == j, v2[j], s2t)
            i2t = jnp.where(sub == j, i2[j], i2t)
        cand, eid = [], []
        for hf in range(2):
            s1r = jnp.zeros((tm, LANES), F32)
            i1r = jnp.zeros((tm, LANES), F32)
            for ii in range(LANES // PEER_TOPK):
                s1r = jnp.where(grp == ii, v1[8 * hf + ii], s1r)
                i1r = jnp.where(grp == ii, i1[8 * hf + ii], i1r)
            cand.append(s1r + s2t)
            eid.append(i1r * float(PEER_NKEYS) + i2t)
        c_lo, c_hi = cand
        e_lo, e_hi = eid
        den = jnp.zeros((tm, 1), F32)
        m_first = None
        for k in range(PEER_TOPK):
            m = jnp.max(jnp.maximum(c_lo, c_hi), axis=-1, keepdims=True)
            pos = jnp.min(jnp.minimum(jnp.where(c_lo == m, lane, 2.0 * LANES),
                                      jnp.where(c_hi == m, lane + float(LANES), 2.0 * LANES)),
                          axis=-1, keepdims=True)
            hit_lo = lane == pos
            hit_hi = (lane + float(LANES)) == pos
            e = jnp.max(jnp.maximum(jnp.where(hit_lo, e_lo, -1.0), jnp.where(hit_hi, e_hi, -1.0)),
                        axis=-1, keepdims=True)
            c_lo = jnp.where(hit_lo, ninf, c_lo)
            c_hi = jnp.where(hit_hi, ninf, c_hi)
            if k == 0:
                m_first = m
            den = den + jnp.exp(m - m_first)
            slot = lane_i == (h * PEER_TOPK + k)
            idx_acc = jnp.where(slot, e, idx_acc)
            top_acc = jnp.where(slot, m, top_acc)
        mine = grp == h
        max_acc = jnp.where(mine, m_first, max_acc)
        den_acc = jnp.where(mine, den, den_acc)
        return idx_acc, top_acc, max_acc, den_acc

    zf = jnp.zeros((tm, LANES), F32)
    idx, top, mx, den = lax.fori_loop(0, PEER_HEADS, head, (zf, zf, zf, zf + 1.0))
    idx_ref[...] = idx.astype(jnp.int32)
    gate_ref[...] = jnp.exp(top - mx) / den


def peer_topk(scores, *, tm=128):
    n = scores.shape[0]
    return pl.pallas_call(
        _peer_topk_kernel,
        grid=(n // tm,),
        in_specs=[pl.BlockSpec((tm, scores.shape[1]), lambda i: (i, 0))],
        out_specs=[pl.BlockSpec((tm, PEER_PICKS), lambda i: (i, 0)),
                   pl.BlockSpec((tm, PEER_PICKS), lambda i: (i, 0))],
        out_shape=[jax.ShapeDtypeStruct((n, PEER_PICKS), jnp.int32),
                   jax.ShapeDtypeStruct((n, PEER_PICKS), F32)],
        compiler_params=pltpu.CompilerParams(
            dimension_semantics=("arbitrary",), vmem_limit_bytes=VMEM_LIMIT),
        name="peer_topk",
    )(scores)


def pack_expert_table(down, up):
    e, d = down.shape
    def pairs(t):
        return t.astype(BF16).reshape(e, 2, d // 2).transpose(0, 2, 1)
    both = jnp.concatenate([pairs(down), pairs(up)], axis=1)
    return lax.bitcast_convert_type(both, jnp.uint32)


def _row_copy(tab_ref, buf_ref, sem_ref, expert, slot, row):
    return pltpu.make_async_copy(tab_ref.at[pl.ds(expert, 1), :],
                                 buf_ref.at[slot, pl.ds(row, 1), :], sem_ref.at[slot])


def _peer_apply_kernel(idx0_ref, idx1_ref, hn_ref, gate_ref, res_ref, tab_ref, o_ref, buf_ref,
                       sem_ref, *, tb):
    step = pl.program_id(0)
    nsteps = pl.num_programs(0)
    slot = step % 2
    rows = tb * PEER_PICKS

    def issue(idx_ref, to_slot):
        def per_token(t, carry):
            for j in range(PEER_PICKS):
                _row_copy(tab_ref, buf_ref, sem_ref, idx_ref[t, j], to_slot,
                          t * PEER_PICKS + j).start(priority=j % 2)
            return carry
        lax.fori_loop(0, tb, per_token, 0)

    @pl.when(step == 0)
    def _():
        issue(idx0_ref, 0)

    @pl.when(step + 1 < nsteps)
    def _():
        issue(idx1_ref, 1 - slot)

    pltpu.make_async_copy(tab_ref.at[pl.ds(0, rows), :], buf_ref.at[slot], sem_ref.at[slot]).wait()

    even = (lax.broadcasted_iota(jnp.int32, (1, LANES), 1) % 2) == 0
    pad = jnp.zeros((6, HALF), F32)
    for t in range(tb):
        words = buf_ref[slot, pl.ds(t * PEER_PICKS, PEER_PICKS), :]
        tbl = pltpu.bitcast(words, BF16)
        down, up = tbl[:, :HALF], tbl[:, HALF:]
        x = hn_ref[pl.ds(t, 1), :]
        x2 = jnp.concatenate([x[:, :HALF], x[:, HALF:], pad], axis=0).astype(BF16)
        r = _dot_nt(x2, down)
        coefs = []
        for c in range(2 * PEER_PICKS // LANES):
            cols = slice(c * LANES, (c + 1) * LANES)
            part = jnp.where(even, r[0:1, cols], r[1:2, cols])
            hid = part + jnp.where(even, pltpu.roll(part, LANES - 1, 1), pltpu.roll(part, 1, 1))
            coefs.append(gate_ref[pl.ds(t, 1), cols] * _gelu(hid))
        coef = jnp.concatenate(coefs, axis=1)
        even2 = jnp.concatenate([even, even], axis=1)
        c2 = jnp.concatenate([jnp.where(even2, coef, 0.0), jnp.where(even2, 0.0, coef),
                              jnp.zeros((6, 2 * PEER_PICKS), F32)], axis=0).astype(BF16)
        y2 = jnp.dot(c2, up, preferred_element_type=F32)
        y = jnp.concatenate([y2[0:1], y2[1:2]], axis=1)
        o_ref[pl.ds(t, 1), :] = res_ref[pl.ds(t, 1), :] + y


def peer_apply(idx, hn, gate, res, table, *, tb=8):
    n, d = hn.shape
    nsteps = n // tb
    gate2 = jnp.repeat(gate, 2, axis=1)
    kernel = functools.partial(_peer_apply_kernel, tb=tb)
    smem = pltpu.SMEM
    return pl.pallas_call(
        kernel,
        grid=(nsteps,),
        in_specs=[
            pl.BlockSpec((tb, PEER_PICKS), lambda i: (i, 0), memory_space=smem),
            pl.BlockSpec((tb, PEER_PICKS), lambda i: (jnp.minimum(i + 1, nsteps - 1), 0),
                         memory_space=smem),
            pl.BlockSpec((tb, d), lambda i: (i, 0)),
            pl.BlockSpec((tb, 2 * PEER_PICKS), lambda i: (i, 0)),
            pl.BlockSpec((tb, d), lambda i: (i, 0)),
            pl.BlockSpec(memory_space=pl.ANY),
        ],
        out_specs=pl.BlockSpec((tb, d), lambda i: (i, 0)),
        out_shape=jax.ShapeDtypeStruct((n, d), F32),
        scratch_shapes=[pltpu.VMEM((2, tb * PEER_PICKS, d), jnp.uint32),
                        pltpu.SemaphoreType.DMA((2,))],
        compiler_params=pltpu.CompilerParams(
            dimension_semantics=("arbitrary",), vmem_limit_bytes=VMEM_LIMIT),
        name="peer_apply",
    )(idx, idx, hn, gate2, res, table)


def peer_layer(x, norm_g, w_query, sub_keys, down, up):
    hn, scores = peer_scores(x, norm_g, w_query, sub_keys)
    idx, gate = peer_topk(scores)
    return peer_apply(idx, hn, gate, x, pack_expert_table(down, up))


def attention_layer(x, batch, seq, norm_g, w_in, a_q_gain, a_k_gain, b_q_gain, b_k_gain,
                    lam_q1, lam_k1, lam_q2, lam_k2, b_sub_gain, w_out, lambda_init):
    proj = norm_matmul(x, norm_g, w_in)
    na = A_HEADS // 2
    tile2 = lambda g: jnp.concatenate([g, g])
    out_a = pair_attention(proj, batch, seq, "dilated", 0, na, 2 * na, na,
                           tile2(a_q_gain), tile2(a_k_gain))
    out_b = pair_attention(proj, batch, seq, "diff", 3 * na, 3 * na + B_HEADS, 3 * na + 2 * B_HEADS,
                           B_HEADS, b_q_gain.reshape(-1), b_k_gain.reshape(-1),
                           extras=(lam_q1, lam_k1, lam_q2, lam_k2, b_sub_gain),
                           lambda_init=lambda_init)
    return out_proj(x, out_a, out_b, w_out)


def kernel(x, attn_norm_g, attn_w_in, a_q_gain, a_k_gain, b_q_gain, b_k_gain, lam_q1, lam_k1,
           lam_q2, lam_k2, b_sub_gain, attn_w_out, sgu_norm_g, sgu_w_in, sgu_v_gain, sgu_w_spatial,
           sgu_b_spatial, sgu_w_out, ffn_norm_g, peer_w_query, peer_sub_keys, peer_down, peer_up):
    batch, seq, d = x.shape
    depth = ffn_norm_g.shape[0]
    h = x.reshape(batch * seq, d)
    for layer in range(depth):
        i = layer // 2
        if layer % 2 == 0:
            lambda_init = 0.8 - 0.6 * math.exp(-0.3 * layer)
            h = attention_layer(h, batch, seq, attn_norm_g[i], attn_w_in[i], a_q_gain[i],
                                a_k_gain[i], b_q_gain[i], b_k_gain[i], lam_q1[i], lam_k1[i],
                                lam_q2[i], lam_k2[i], b_sub_gain[i], attn_w_out[i], lambda_init)
        else:
            h = sgu_layer(h, sgu_norm_g[i], sgu_w_in[i], sgu_v_gain[i], sgu_w_spatial[i],
                          sgu_b_spatial[i], sgu_w_out[i])
        h = peer_layer(h, ffn_norm_g[layer], peer_w_query[layer], peer_sub_keys[layer],
                       peer_down[layer], peer_up[layer])
    return h.reshape(batch, seq, d)
```

```python
import functools
import math

import jax
import jax.numpy as jnp
from jax import lax
from jax.experimental import pallas as pl
from jax.experimental.pallas import tpu as pltpu

D_MODEL = 1024
HEAD_DIM = 64
EPS = 1e-6
NEG = -1e30
A_HEADS = 8
B_HEADS = 4
C_CHUNK = 128
C_GROUPS = 8
C_WIDTH = 2 * D_MODEL
C_GROUP_DIM = C_WIDTH // C_GROUPS
PEER_HEADS = 8
PEER_NKEYS = 128
PEER_TOPK = 16
PEER_PICKS = PEER_HEADS * PEER_TOPK

LANES = 128
HALF = D_MODEL // 2
VMEM_LIMIT = 56 * 1024 * 1024

BF16 = jnp.bfloat16
F32 = jnp.float32


def _gelu(x):
    return 0.5 * x * (1.0 + jnp.tanh(math.sqrt(2.0 / math.pi) * (x + 0.044715 * (x * x * x))))


def _rms(x, g):
    return x * lax.rsqrt(jnp.mean(x * x, axis=-1, keepdims=True) + EPS) * g


def _dot_nt(a, b):
    return lax.dot_general(a, b, (((1,), (1,)), ((), ())), preferred_element_type=F32)


def _norm_matmul_kernel(x_ref, g_ref, w_ref, o_ref, xn_ref):
    @pl.when(pl.program_id(1) == 0)
    def _():
        xn_ref[...] = _rms(x_ref[...], g_ref[...]).astype(BF16)

    o_ref[...] = jnp.dot(xn_ref[...], w_ref[...], preferred_element_type=F32)


def norm_matmul(x, g, w, *, tm=512, tn=512):
    n, d = x.shape
    nout = w.shape[1]
    tm = min(tm, n)
    return pl.pallas_call(
        _norm_matmul_kernel,
        grid=(n // tm, nout // tn),
        in_specs=[
            pl.BlockSpec((tm, d), lambda i, j: (i, 0)),
            pl.BlockSpec((1, d), lambda i, j: (0, 0)),
            pl.BlockSpec((d, tn), lambda i, j: (0, j)),
        ],
        out_specs=pl.BlockSpec((tm, tn), lambda i, j: (i, j)),
        out_shape=jax.ShapeDtypeStruct((n, nout), F32),
        scratch_shapes=[pltpu.VMEM((tm, d), BF16)],
        compiler_params=pltpu.CompilerParams(
            dimension_semantics=("arbitrary", "arbitrary"), vmem_limit_bytes=VMEM_LIMIT),
        name="norm_matmul",
    )(x, g.reshape(1, d), w.astype(BF16))


def _pair_norm(t, gain, lo):
    sq = t * t
    s_lo = jnp.sum(jnp.where(lo, sq, 0.0), axis=-1, keepdims=True)
    s_hi = jnp.sum(jnp.where(lo, 0.0, sq), axis=-1, keepdims=True)
    ms = jnp.where(lo, s_lo, s_hi) * (1.0 / HEAD_DIM)
    return t * lax.rsqrt(ms + EPS) * gain


def _attn_kernel(*refs, mode, tq, tk, lambda_init):
    if mode == "dilated":
        q_ref, k_ref, v_ref, qg_ref, kg_ref, o_ref, kn_ref = refs
    else:
        (q_ref, k_ref, v_ref, qg_ref, kg_ref, lq1_ref, lk1_ref, lq2_ref, lk2_ref, sg_ref,
         o_ref, kn_ref) = refs
    i = pl.program_id(2)
    lo = lax.broadcasted_iota(jnp.int32, (1, LANES), 1) < HEAD_DIM

    @pl.when(i == 0)
    def _():
        kn_ref[...] = _pair_norm(k_ref[...], kg_ref[...], lo).astype(BF16)

    qn = _pair_norm(q_ref[...], qg_ref[...], lo) * (HEAD_DIM ** -0.5)
    qa = jnp.where(lo, qn, 0.0).astype(BF16)
    qb = jnp.where(lo, 0.0, qn).astype(BF16)
    rows = i * tq + lax.broadcasted_iota(jnp.int32, (tq, tk), 0)
    col0 = lax.broadcasted_iota(jnp.int32, (tq, tk), 1)

    def body(j, carry):
        ma, la, acca, mb, lb, accb = carry
        off = pl.multiple_of(j * tk, tk)
        kb = kn_ref[pl.ds(off, tk), :]
        vb = v_ref[pl.ds(off, tk), :].astype(BF16)
        dist = rows - (col0 + j * tk)
        if mode == "dilated":
            cnt = ((dist <= 128).astype(F32)
                   + ((dist <= 512) & ((dist & 3) == 0)).astype(F32)
                   + ((dist & 15) == 0).astype(F32))
            cnt = jnp.where(dist >= 0, cnt, 0.0)
        else:
            cnt = (dist >= 0).astype(F32)
        valid = cnt > 0.0

        def update(qh, m, l, acc):
            s = jnp.where(valid, _dot_nt(qh, kb), NEG)
            m_new = jnp.maximum(m, jnp.max(s, axis=-1, keepdims=True))
            alpha = jnp.exp(m - m_new)
            p = cnt * jnp.exp(s - m_new)
            l_new = alpha * l + jnp.sum(p, axis=-1, keepdims=True)
            acc_new = alpha * acc + jnp.dot(p.astype(BF16), vb, preferred_element_type=F32)
            return m_new, l_new, acc_new

        ma, la, acca = update(qa, ma, la, acca)
        mb, lb, accb = update(qb, mb, lb, accb)
        return ma, la, acca, mb, lb, accb

    m0 = jnp.full((tq, 1), NEG, F32)
    l0 = jnp.zeros((tq, 1), F32)
    a0 = jnp.zeros((tq, LANES), F32)
    nkv = (i * tq + tq + tk - 1) // tk
    ma, la, acca, mb, lb, accb = lax.fori_loop(0, nkv, body, (m0, l0, a0, m0, l0, a0))
    oa = acca / la
    ob = accb / lb
    if mode == "dilated":
        o_ref[...] = jnp.where(lo, oa, ob)
    else:
        lam = (jnp.exp(jnp.sum(lq1_ref[...] * lk1_ref[...], axis=-1, keepdims=True))
               - jnp.exp(jnp.sum(lq2_ref[...] * lk2_ref[...], axis=-1, keepdims=True))
               + lambda_init)
        o = oa - lam * ob
        o_ref[...] = _rms(o, sg_ref[...]) * (1.0 - lambda_init)


def pair_attention(proj, batch, seq, mode, qcol, kcol, vcol, npairs, q_gain, k_gain, extras=(),
                   lambda_init=0.0, tq=256, tk=256):
    nq = seq // tq
    small = [q_gain.reshape(1, LANES), k_gain.reshape(1, LANES)] + [e.reshape(1, -1) for e in extras]
    small_specs = [pl.BlockSpec(s.shape, lambda b, p, i: (0, 0)) for s in small]
    kernel = functools.partial(_attn_kernel, mode=mode, tq=tq, tk=tk, lambda_init=lambda_init)
    return pl.pallas_call(
        kernel,
        grid=(batch, npairs, nq),
        in_specs=[
            pl.BlockSpec((tq, LANES), lambda b, p, i: (b * nq + i, qcol + p)),
            pl.BlockSpec((seq, LANES), lambda b, p, i: (b, kcol + p)),
            pl.BlockSpec((seq, LANES), lambda b, p, i: (b, vcol + p)),
        ] + small_specs,
        out_specs=pl.BlockSpec((tq, LANES), lambda b, p, i: (b * nq + i, p)),
        out_shape=jax.ShapeDtypeStruct((batch * seq, npairs * LANES), F32),
        scratch_shapes=[pltpu.VMEM((seq, LANES), BF16)],
        compiler_params=pltpu.CompilerParams(
            dimension_semantics=("arbitrary", "arbitrary", "arbitrary"),
            vmem_limit_bytes=VMEM_LIMIT),
        name="attn_" + mode,
    )(proj, proj, proj, *small)


def _out_proj_kernel(x_ref, a_ref, b_ref, wa_ref, wb_ref, o_ref):
    o_ref[...] = (x_ref[...]
                  + jnp.dot(a_ref[...].astype(BF16), wa_ref[...], preferred_element_type=F32)
                  + jnp.dot(b_ref[...].astype(BF16), wb_ref[...], preferred_element_type=F32))


def out_proj(x, a, b, w, *, tm=512):
    n, d = x.shape
    ka, kb = a.shape[1], b.shape[1]
    w = w.astype(BF16)
    tm = min(tm, n)
    return pl.pallas_call(
        _out_proj_kernel,
        grid=(n // tm,),
        in_specs=[
            pl.BlockSpec((tm, d), lambda i: (i, 0)),
            pl.BlockSpec((tm, ka), lambda i: (i, 0)),
            pl.BlockSpec((tm, kb), lambda i: (i, 0)),
            pl.BlockSpec((ka, d), lambda i: (0, 0)),
            pl.BlockSpec((kb, d), lambda i: (0, 0)),
        ],
        out_specs=pl.BlockSpec((tm, d), lambda i: (i, 0)),
        out_shape=jax.ShapeDtypeStruct((n, d), F32),
        compiler_params=pltpu.CompilerParams(
            dimension_semantics=("arbitrary",), vmem_limit_bytes=VMEM_LIMIT),
        name="out_proj",
    )(x, a, b, w[:ka], w[ka:])


def _sgu_kernel(x_ref, g_ref, win_ref, vg_ref, ws_ref, bs_ref, wout_ref, o_ref, gated_ref, *, tm):
    x = x_ref[...]
    xn = _rms(x, g_ref[...]).astype(BF16)
    z = _gelu(jnp.dot(xn, win_ref[...], preferred_element_type=F32))
    u = z[:, :C_WIDTH]
    v = _rms(z[:, C_WIDTH:], vg_ref[...]).astype(BF16)
    r = lax.broadcasted_iota(jnp.int32, (C_CHUNK, C_CHUNK), 0)
    c = lax.broadcasted_iota(jnp.int32, (C_CHUNK, C_CHUNK), 1)
    causal = c <= r
    for grp in range(C_GROUPS):
        ws = jnp.where(causal, ws_ref[grp], 0.0).astype(BF16)
        bias = bs_ref[:, grp:grp + 1]
        cols = slice(grp * C_GROUP_DIM, (grp + 1) * C_GROUP_DIM)
        for ch in range(tm // C_CHUNK):
            rws = slice(ch * C_CHUNK, (ch + 1) * C_CHUNK)
            gate = jnp.dot(ws, v[rws, cols], preferred_element_type=F32) + bias
            gated_ref[rws, cols] = (u[rws, cols] * gate).astype(BF16)
    o_ref[...] = x + jnp.dot(gated_ref[...], wout_ref[...], preferred_element_type=F32)


def sgu_layer(x, norm_g, w_in, v_gain, w_spatial, b_spatial, w_out, *, tm=256):
    n, d = x.shape
    kernel = functools.partial(_sgu_kernel, tm=tm)
    return pl.pallas_call(
        kernel,
        grid=(n // tm,),
        in_specs=[
            pl.BlockSpec((tm, d), lambda i: (i, 0)),
            pl.BlockSpec((1, d), lambda i: (0, 0)),
            pl.BlockSpec((d, 2 * C_WIDTH), lambda i: (0, 0)),
            pl.BlockSpec((1, C_WIDTH), lambda i: (0, 0)),
            pl.BlockSpec((C_GROUPS, C_CHUNK, C_CHUNK), lambda i: (0, 0, 0)),
            pl.BlockSpec((C_CHUNK, C_GROUPS), lambda i: (0, 0)),
            pl.BlockSpec((C_WIDTH, d), lambda i: (0, 0)),
        ],
        out_specs=pl.BlockSpec((tm, d), lambda i: (i, 0)),
        out_shape=jax.ShapeDtypeStruct((n, d), F32),
        scratch_shapes=[pltpu.VMEM((tm, C_WIDTH), BF16)],
        compiler_params=pltpu.CompilerParams(
            dimension_semantics=("arbitrary",), vmem_limit_bytes=VMEM_LIMIT),
        name="sgu",
    )(x, norm_g.reshape(1, d), w_in.astype(BF16), v_gain.reshape(1, C_WIDTH), w_spatial,
      b_spatial.T, w_out.astype(BF16))


def _peer_scores_kernel(x_ref, g_ref, wq_ref, sk_ref, hn_ref, sc_ref):
    hn = _rms(x_ref[...], g_ref[...])
    hn_ref[...] = hn
    q = jnp.dot(hn.astype(BF16), wq_ref[...], preferred_element_type=F32)
    for hp in range(2 * PEER_HEADS):
        cols = slice(hp * PEER_NKEYS, (hp + 1) * PEER_NKEYS)
        sc_ref[:, cols] = _dot_nt(q[:, cols].astype(BF16), sk_ref[hp].astype(BF16))


def peer_scores(x, norm_g, w_query, sub_keys, *, tm=512):
    n, d = x.shape
    nq = w_query.shape[1]
    tm = min(tm, n)
    sk = sub_keys.reshape(2 * PEER_HEADS, PEER_NKEYS, PEER_NKEYS)
    return pl.pallas_call(
        _peer_scores_kernel,
        grid=(n // tm,),
        in_specs=[
            pl.BlockSpec((tm, d), lambda i: (i, 0)),
            pl.BlockSpec((1, d), lambda i: (0, 0)),
            pl.BlockSpec((d, nq), lambda i: (0, 0)),
            pl.BlockSpec(sk.shape, lambda i: (0, 0, 0)),
        ],
        out_specs=[pl.BlockSpec((tm, d), lambda i: (i, 0)),
                   pl.BlockSpec((tm, nq), lambda i: (i, 0))],
        out_shape=[jax.ShapeDtypeStruct((n, d), F32), jax.ShapeDtypeStruct((n, nq), F32)],
        compiler_params=pltpu.CompilerParams(
            dimension_semantics=("arbitrary",), vmem_limit_bytes=VMEM_LIMIT),
        name="peer_scores",
    )(x, norm_g.reshape(1, d), w_query.astype(BF16), sk)


def _peer_topk_kernel(sc_ref, idx_ref, gate_ref):
    tm = sc_ref.shape[0]
    lane_i = lax.broadcasted_iota(jnp.int32, (tm, LANES), 1)
    lane = lane_i.astype(F32)
    grp = lane_i // PEER_TOPK
    sub = lane_i % PEER_TOPK
    ninf = jnp.float32(-jnp.inf)

    def head(h, carry):
        idx_acc, top_acc, max_acc, den_acc = carry

        def extract(part):
            off = pl.multiple_of((2 * h + part) * PEER_NKEYS, PEER_NKEYS)
            s = sc_ref[:, pl.ds(off, PEER_NKEYS)]
            vals, ids = [], []
            for _ in range(PEER_TOPK):
                m = jnp.max(s, axis=-1, keepdims=True)
                am = jnp.min(jnp.where(s == m, lane, float(PEER_NKEYS)), axis=-1, keepdims=True)
                s = jnp.where(lane == am, ninf, s)
                vals.append(m)
                ids.append(am)
            return vals, ids

        v1, i1 = extract(0)
        v2, i2 = extract(1)
        s2t = jnp.zeros((tm, LANES), F32)
        i2t = jnp.zeros((tm, LANES), F32)
        for j in range(PEER_TOPK):
            s2t = jnp.where(sub == j, v2[j], s2t)
            i2t = jnp.where(sub == j, i2[j], i2t)
        cand, eid = [], []
        for hf in range(2):
            s1r = jnp.zeros((tm, LANES), F32)
            i1r = jnp.zeros((tm, LANES), F32)
            for ii in range(LANES // PEER_TOPK):
                s1r = jnp.where(grp == ii, v1[8 * hf + ii], s1r)
                i1r = jnp.where(grp == ii, i1[8 * hf + ii], i1r)
            cand.append(s1r + s2t)
            eid.append(i1r * float(PEER_NKEYS) + i2t)
        c_lo, c_hi = cand
        e_lo, e_hi = eid
        den = jnp.zeros((tm, 1), F32)
        m_first = None
        for k in range(PEER_TOPK):
            m = jnp.max(jnp.maximum(c_lo, c_hi), axis=-1, keepdims=True)
            pos = jnp.min(jnp.minimum(jnp.where(c_lo == m, lane, 2.0 * LANES),
                                      jnp.where(c_hi == m, lane + float(LANES), 2.0 * LANES)),
                          axis=-1, keepdims=True)
            hit_lo = lane == pos
            hit_hi = (lane + float(LANES)) == pos
            e = jnp.max(jnp.maximum(jnp.where(hit_lo, e_lo, -1.0), jnp.where(hit_hi, e_hi, -1.0)),
                        axis=-1, keepdims=True)
            c_lo = jnp.where(hit_lo, ninf, c_lo)
            c_hi = jnp.where(hit_hi, ninf, c_hi)
            if k == 0:
                m_first = m
            den = den + jnp.exp(m - m_first)
            slot = lane_i == (h * PEER_TOPK + k)
            idx_acc = jnp.where(slot, e, idx_acc)
            top_acc = jnp.where(slot, m, top_acc)
        mine = grp == h
        max_acc = jnp.where(mine, m_first, max_acc)
        den_acc = jnp.where(mine, den, den_acc)
        return idx_acc, top_acc, max_acc, den_acc

    zf = jnp.zeros((tm, LANES), F32)
    idx, top, mx, den = lax.fori_loop(0, PEER_HEADS, head, (zf, zf, zf, zf + 1.0))
    idx_ref[...] = idx.astype(jnp.int32)
    gate_ref[...] = jnp.exp(top - mx) / den


def peer_topk(scores, *, tm=128):
    n = scores.shape[0]
    return pl.pallas_call(
        _peer_topk_kernel,
        grid=(n // tm,),
        in_specs=[pl.BlockSpec((tm, scores.shape[1]), lambda i: (i, 0))],
        out_specs=[pl.BlockSpec((tm, PEER_PICKS), lambda i: (i, 0)),
                   pl.BlockSpec((tm, PEER_PICKS), lambda i: (i, 0))],
        out_shape=[jax.ShapeDtypeStruct((n, PEER_PICKS), jnp.int32),
                   jax.ShapeDtypeStruct((n, PEER_PICKS), F32)],
        compiler_params=pltpu.CompilerParams(
            dimension_semantics=("arbitrary",), vmem_limit_bytes=VMEM_LIMIT),
        name="peer_topk",
    )(scores)


def pack_expert_table(down, up):
    e, d = down.shape
    def pairs(t):
        return t.astype(BF16).reshape(e, 2, d // 2).transpose(0, 2, 1)
    both = jnp.concatenate([pairs(down), pairs(up)], axis=1)
    return lax.bitcast_convert_type(both, jnp.uint32)


def _row_copy(tab_ref, buf_ref, sem_ref, expert, slot, row):
    return pltpu.make_async_copy(tab_ref.at[pl.ds(expert, 1), :],
                                 buf_ref.at[slot, pl.ds(row, 1), :], sem_ref.at[slot])


APPLY_TOKENS = 8


def _peer_apply_kernel(idx0_ref, idx1_ref, hn_ref, gate_ref, res_ref, tab_ref, o_ref, buf_ref,
                       sem_ref):
    step = pl.program_id(0)
    nsteps = pl.num_programs(0)
    tb = APPLY_TOKENS
    rows = tb * PEER_PICKS

    def issue(idx_ref, first_token, slot):
        for t in range(tb):
            for j in range(PEER_PICKS):
                _row_copy(tab_ref, buf_ref, sem_ref, idx_ref[first_token + t, j], slot,
                          t * PEER_PICKS + j).start(priority=j % 2)

    def wait(slot):
        pltpu.make_async_copy(tab_ref.at[pl.ds(0, rows), :], buf_ref.at[slot],
                              sem_ref.at[slot]).wait()

    lane_even = (lax.broadcasted_iota(jnp.int32, (1, LANES), 1) % 2) == 0
    tok_of_row = lax.broadcasted_iota(jnp.int32, (2 * tb, 1), 0) % tb

    def compute(slot, first_token):
        tok = pl.ds(first_token, tb)
        x = hn_ref[tok, :]
        xx = jnp.concatenate([x[:, :HALF], x[:, HALF:]], axis=0).astype(BF16)
        tbls = []
        rsel = jnp.zeros((2 * tb, 2 * PEER_PICKS), F32)
        for t in range(tb):
            words = buf_ref[slot, pl.ds(t * PEER_PICKS, PEER_PICKS), :]
            tbl = pltpu.bitcast(words, BF16)
            tbls.append(tbl)
            rsel = rsel + jnp.where(tok_of_row == t, _dot_nt(xx, tbl[:, :HALF]), 0.0)
        coefs = []
        for c in range(2 * PEER_PICKS // LANES):
            cols = slice(c * LANES, (c + 1) * LANES)
            part = jnp.where(lane_even, rsel[:tb, cols], rsel[tb:, cols])
            hid = part + jnp.where(lane_even, pltpu.roll(part, LANES - 1, 1), pltpu.roll(part, 1, 1))
            coefs.append(gate_ref[tok, cols] * _gelu(hid))
        coef = jnp.concatenate(coefs, axis=1)
        even2 = jnp.concatenate([lane_even, lane_even], axis=1)
        cc = jnp.concatenate([jnp.where(even2, coef, 0.0), jnp.where(even2, 0.0, coef)],
                             axis=0).astype(BF16)
        ysel = jnp.zeros((2 * tb, HALF), F32)
        for t in range(tb):
            ysel = ysel + jnp.where(tok_of_row == t,
                                    jnp.dot(cc, tbls[t][:, HALF:], preferred_element_type=F32), 0.0)
        y = jnp.concatenate([ysel[:tb], ysel[tb:]], axis=1)
        o_ref[tok, :] = res_ref[tok, :] + y

    @pl.when(step == 0)
    def _():
        issue(idx0_ref, 0, 0)

    wait(0)
    issue(idx0_ref, tb, 1)
    compute(0, 0)
    wait(1)

    @pl.when(step + 1 < nsteps)
    def _():
        issue(idx1_ref, 0, 0)

    compute(1, tb)


def peer_apply(idx, hn, gate, res, table):
    n, d = hn.shape
    tb = 2 * APPLY_TOKENS
    nsteps = n // tb
    gate2 = jnp.repeat(gate, 2, axis=1)
    smem = pltpu.SMEM
    return pl.pallas_call(
        _peer_apply_kernel,
        grid=(nsteps,),
        in_specs=[
            pl.BlockSpec((tb, PEER_PICKS), lambda i: (i, 0), memory_space=smem),
            pl.BlockSpec((tb, PEER_PICKS), lambda i: (jnp.minimum(i + 1, nsteps - 1), 0),
                         memory_space=smem),
            pl.BlockSpec((tb, d), lambda i: (i, 0)),
            pl.BlockSpec((tb, 2 * PEER_PICKS), lambda i: (i, 0)),
            pl.BlockSpec((tb, d), lambda i: (i, 0)),
            pl.BlockSpec(memory_space=pl.ANY),
        ],
        out_specs=pl.BlockSpec((tb, d), lambda i: (i, 0)),
        out_shape=jax.ShapeDtypeStruct((n, d), F32),
        scratch_shapes=[pltpu.VMEM((2, APPLY_TOKENS * PEER_PICKS, d), jnp.uint32),
                        pltpu.SemaphoreType.DMA((2,))],
        compiler_params=pltpu.CompilerParams(
            dimension_semantics=("arbitrary",), vmem_limit_bytes=VMEM_LIMIT),
        name="peer_apply",
    )(idx, idx, hn, gate2, res, table)


def peer_layer(x, norm_g, w_query, sub_keys, down, up):
    hn, scores = peer_scores(x, norm_g, w_query, sub_keys)
    idx, gate = peer_topk(scores)
    return peer_apply(idx, hn, gate, x, pack_expert_table(down, up))


def attention_layer(x, batch, seq, norm_g, w_in, a_q_gain, a_k_gain, b_q_gain, b_k_gain,
                    lam_q1, lam_k1, lam_q2, lam_k2, b_sub_gain, w_out, lambda_init):
    proj = norm_matmul(x, norm_g, w_in)
    na = A_HEADS // 2
    tile2 = lambda g: jnp.concatenate([g, g])
    out_a = pair_attention(proj, batch, seq, "dilated", 0, na, 2 * na, na,
                           tile2(a_q_gain), tile2(a_k_gain))
    out_b = pair_attention(proj, batch, seq, "diff", 3 * na, 3 * na + B_HEADS, 3 * na + 2 * B_HEADS,
                           B_HEADS, b_q_gain.reshape(-1), b_k_gain.reshape(-1),
                           extras=(lam_q1, lam_k1, lam_q2, lam_k2, b_sub_gain),
                           lambda_init=lambda_init)
    return out_proj(x, out_a, out_b, w_out)


def kernel(x, attn_norm_g, attn_w_in, a_q_gain, a_k_gain, b_q_gain, b_k_gain, lam_q1, lam_k1,
           lam_q2, lam_k2, b_sub_gain, attn_w_out, sgu_norm_g, sgu_w_in, sgu_v_gain, sgu_w_spatial,
           sgu_b_spatial, sgu_w_out, ffn_norm_g, peer_w_query, peer_sub_keys, peer_down, peer_up):
    batch, seq, d = x.shape
    depth = ffn_norm_g.shape[0]
    h = x.reshape(batch * seq, d)
    for layer in range(depth):
        i = layer // 2
        if layer % 2 == 0:
            lambda_init = 0.8 - 0.6 * math.exp(-0.3 * layer)
            h = attention_layer(h, batch, seq, attn_norm_g[i], attn_w_in[i], a_q_gain[i],
                                a_k_gain[i], b_q_gain[i], b_k_gain[i], lam_q1[i], lam_k1[i],
                                lam_q2[i], lam_k2[i], b_sub_gain[i], attn_w_out[i], lambda_init)
        else:
            h = sgu_layer(h, sgu_norm_g[i], sgu_w_in[i], sgu_v_gain[i], sgu_w_spatial[i],
                          sgu_b_spatial[i], sgu_w_out[i])
        h = peer_layer(h, ffn_norm_g[layer], peer_w_query[layer], peer_sub_keys[layer],
                       peer_down[layer], peer_up[layer])
    return h.reshape(batch, seq, d)
```

```python
import functools
import math

import jax
import jax.numpy as jnp
from jax import lax
from jax.experimental import pallas as pl
from jax.experimental.pallas import tpu as pltpu

D_MODEL = 1024
HEAD_DIM = 64
EPS = 1e-6
NEG = -1e30
A_HEADS = 8
B_HEADS = 4
C_CHUNK = 128
C_GROUPS = 8
C_WIDTH = 2 * D_MODEL
C_GROUP_DIM = C_WIDTH // C_GROUPS
PEER_HEADS = 8
PEER_NKEYS = 128
PEER_TOPK = 16
PEER_PICKS = PEER_HEADS * PEER_TOPK

LANES = 128
HALF = D_MODEL // 2
VMEM_LIMIT = 56 * 1024 * 1024

BF16 = jnp.bfloat16
F32 = jnp.float32


def _gelu(x):
    return 0.5 * x * (1.0 + jnp.tanh(math.sqrt(2.0 / math.pi) * (x + 0.044715 * (x * x * x))))


def _rms(x, g):
    return x * lax.rsqrt(jnp.mean(x * x, axis=-1, keepdims=True) + EPS) * g


def _dot_nt(a, b):
    return lax.dot_general(a, b, (((1,), (1,)), ((), ())), preferred_element_type=F32)


def _norm_matmul_kernel(x_ref, g_ref, w_ref, o_ref, xn_ref):
    @pl.when(pl.program_id(1) == 0)
    def _():
        xn_ref[...] = _rms(x_ref[...], g_ref[...]).astype(BF16)

    o_ref[...] = jnp.dot(xn_ref[...], w_ref[...], preferred_element_type=F32)


def norm_matmul(x, g, w, *, tm=512, tn=512):
    n, d = x.shape
    nout = w.shape[1]
    tm = min(tm, n)
    return pl.pallas_call(
        _norm_matmul_kernel,
        grid=(n // tm, nout // tn),
        in_specs=[
            pl.BlockSpec((tm, d), lambda i, j: (i, 0)),
            pl.BlockSpec((1, d), lambda i, j: (0, 0)),
            pl.BlockSpec((d, tn), lambda i, j: (0, j)),
        ],
        out_specs=pl.BlockSpec((tm, tn), lambda i, j: (i, j)),
        out_shape=jax.ShapeDtypeStruct((n, nout), F32),
        scratch_shapes=[pltpu.VMEM((tm, d), BF16)],
        compiler_params=pltpu.CompilerParams(
            dimension_semantics=("arbitrary", "arbitrary"), vmem_limit_bytes=VMEM_LIMIT),
        name="norm_matmul",
    )(x, g.reshape(1, d), w.astype(BF16))


def _pair_norm(t, gain, lo):
    sq = t * t
    s_lo = jnp.sum(jnp.where(lo, sq, 0.0), axis=-1, keepdims=True)
    s_hi = jnp.sum(jnp.where(lo, 0.0, sq), axis=-1, keepdims=True)
    ms = jnp.where(lo, s_lo, s_hi) * (1.0 / HEAD_DIM)
    return t * lax.rsqrt(ms + EPS) * gain


def _attn_kernel(*refs, mode, tq, tk, lambda_init):
    if mode == "dilated":
        q_ref, k_ref, v_ref, qg_ref, kg_ref, o_ref, kn_ref = refs
    else:
        (q_ref, k_ref, v_ref, qg_ref, kg_ref, lq1_ref, lk1_ref, lq2_ref, lk2_ref, sg_ref,
         o_ref, kn_ref) = refs
    i = pl.program_id(2)
    lo = lax.broadcasted_iota(jnp.int32, (1, LANES), 1) < HEAD_DIM

    @pl.when(i == 0)
    def _():
        kn_ref[...] = _pair_norm(k_ref[...], kg_ref[...], lo).astype(BF16)

    qn = _pair_norm(q_ref[...], qg_ref[...], lo) * (HEAD_DIM ** -0.5)
    qa = jnp.where(lo, qn, 0.0).astype(BF16)
    qb = jnp.where(lo, 0.0, qn).astype(BF16)
    rows = i * tq + lax.broadcasted_iota(jnp.int32, (tq, tk), 0)
    col0 = lax.broadcasted_iota(jnp.int32, (tq, tk), 1)

    def body(j, carry):
        ma, la, acca, mb, lb, accb = carry
        off = pl.multiple_of(j * tk, tk)
        kb = kn_ref[pl.ds(off, tk), :]
        vb = v_ref[pl.ds(off, tk), :].astype(BF16)
        dist = rows - (col0 + j * tk)
        if mode == "dilated":
            cnt = ((dist <= 128).astype(F32)
                   + ((dist <= 512) & ((dist & 3) == 0)).astype(F32)
                   + ((dist & 15) == 0).astype(F32))
            cnt = jnp.where(dist >= 0, cnt, 0.0)
        else:
            cnt = (dist >= 0).astype(F32)
        valid = cnt > 0.0

        def update(qh, m, l, acc):
            s = jnp.where(valid, _dot_nt(qh, kb), NEG)
            m_new = jnp.maximum(m, jnp.max(s, axis=-1, keepdims=True))
            alpha = jnp.exp(m - m_new)
            p = cnt * jnp.exp(s - m_new)
            l_new = alpha * l + jnp.sum(p, axis=-1, keepdims=True)
            acc_new = alpha * acc + jnp.dot(p.astype(BF16), vb, preferred_element_type=F32)
            return m_new, l_new, acc_new

        ma, la, acca = update(qa, ma, la, acca)
        mb, lb, accb = update(qb, mb, lb, accb)
        return ma, la, acca, mb, lb, accb

    m0 = jnp.full((tq, 1), NEG, F32)
    l0 = jnp.zeros((tq, 1), F32)
    a0 = jnp.zeros((tq, LANES), F32)
    nkv = (i * tq + tq + tk - 1) // tk
    ma, la, acca, mb, lb, accb = lax.fori_loop(0, nkv, body, (m0, l0, a0, m0, l0, a0))
    oa = acca / la
    ob = accb / lb
    if mode == "dilated":
        o_ref[...] = jnp.where(lo, oa, ob)
    else:
        lam = (jnp.exp(jnp.sum(lq1_ref[...] * lk1_ref[...], axis=-1, keepdims=True))
               - jnp.exp(jnp.sum(lq2_ref[...] * lk2_ref[...], axis=-1, keepdims=True))
               + lambda_init)
        o = oa - lam * ob
        o_ref[...] = _rms(o, sg_ref[...]) * (1.0 - lambda_init)


def pair_attention(proj, batch, seq, mode, qcol, kcol, vcol, npairs, q_gain, k_gain, extras=(),
                   lambda_init=0.0, tq=256, tk=256):
    nq = seq // tq
    small = [q_gain.reshape(1, LANES), k_gain.reshape(1, LANES)] + [e.reshape(1, -1) for e in extras]
    small_specs = [pl.BlockSpec(s.shape, lambda b, p, i: (0, 0)) for s in small]
    kernel = functools.partial(_attn_kernel, mode=mode, tq=tq, tk=tk, lambda_init=lambda_init)
    return pl.pallas_call(
        kernel,
        grid=(batch, npairs, nq),
        in_specs=[
            pl.BlockSpec((tq, LANES), lambda b, p, i: (b * nq + i, qcol + p)),
            pl.BlockSpec((seq, LANES), lambda b, p, i: (b, kcol + p)),
            pl.BlockSpec((seq, LANES), lambda b, p, i: (b, vcol + p)),
        ] + small_specs,
        out_specs=pl.BlockSpec((tq, LANES), lambda b, p, i: (b * nq + i, p)),
        out_shape=jax.ShapeDtypeStruct((batch * seq, npairs * LANES), F32),
        scratch_shapes=[pltpu.VMEM((seq, LANES), BF16)],
        compiler_params=pltpu.CompilerParams(
            dimension_semantics=("arbitrary", "arbitrary", "arbitrary"),
            vmem_limit_bytes=VMEM_LIMIT),
        name="attn_" + mode,
    )(proj, proj, proj, *small)


def _out_proj_kernel(x_ref, a_ref, b_ref, wa_ref, wb_ref, o_ref):
    o_ref[...] = (x_ref[...]
                  + jnp.dot(a_ref[...].astype(BF16), wa_ref[...], preferred_element_type=F32)
                  + jnp.dot(b_ref[...].astype(BF16), wb_ref[...], preferred_element_type=F32))


def out_proj(x, a, b, w, *, tm=512):
    n, d = x.shape
    ka, kb = a.shape[1], b.shape[1]
    w = w.astype(BF16)
    tm = min(tm, n)
    return pl.pallas_call(
        _out_proj_kernel,
        grid=(n // tm,),
        in_specs=[
            pl.BlockSpec((tm, d), lambda i: (i, 0)),
            pl.BlockSpec((tm, ka), lambda i: (i, 0)),
            pl.BlockSpec((tm, kb), lambda i: (i, 0)),
            pl.BlockSpec((ka, d), lambda i: (0, 0)),
            pl.BlockSpec((kb, d), lambda i: (0, 0)),
        ],
        out_specs=pl.BlockSpec((tm, d), lambda i: (i, 0)),
        out_shape=jax.ShapeDtypeStruct((n, d), F32),
        compiler_params=pltpu.CompilerParams(
            dimension_semantics=("arbitrary",), vmem_limit_bytes=VMEM_LIMIT),
        name="out_proj",
    )(x, a, b, w[:ka], w[ka:])


def _sgu_kernel(x_ref, g_ref, win_ref, vg_ref, ws_ref, bs_ref, wout_ref, o_ref, gated_ref, *, tm):
    x = x_ref[...]
    xn = _rms(x, g_ref[...]).astype(BF16)
    z = _gelu(jnp.dot(xn, win_ref[...], preferred_element_type=F32))
    u = z[:, :C_WIDTH]
    v = _rms(z[:, C_WIDTH:], vg_ref[...]).astype(BF16)
    r = lax.broadcasted_iota(jnp.int32, (C_CHUNK, C_CHUNK), 0)
    c = lax.broadcasted_iota(jnp.int32, (C_CHUNK, C_CHUNK), 1)
    causal = c <= r
    for grp in range(C_GROUPS):
        ws = jnp.where(causal, ws_ref[grp], 0.0).astype(BF16)
        bias = bs_ref[:, grp:grp + 1]
        cols = slice(grp * C_GROUP_DIM, (grp + 1) * C_GROUP_DIM)
        for ch in range(tm // C_CHUNK):
            rws = slice(ch * C_CHUNK, (ch + 1) * C_CHUNK)
            gate = jnp.dot(ws, v[rws, cols], preferred_element_type=F32) + bias
            gated_ref[rws, cols] = (u[rws, cols] * gate).astype(BF16)
    o_ref[...] = x + jnp.dot(gated_ref[...], wout_ref[...], preferred_element_type=F32)


def sgu_layer(x, norm_g, w_in, v_gain, w_spatial, b_spatial, w_out, *, tm=256):
    n, d = x.shape
    kernel = functools.partial(_sgu_kernel, tm=tm)
    return pl.pallas_call(
        kernel,
        grid=(n // tm,),
        in_specs=[
            pl.BlockSpec((tm, d), lambda i: (i, 0)),
            pl.BlockSpec((1, d), lambda i: (0, 0)),
            pl.BlockSpec((d, 2 * C_WIDTH), lambda i: (0, 0)),
            pl.BlockSpec((1, C_WIDTH), lambda i: (0, 0)),
            pl.BlockSpec((C_GROUPS, C_CHUNK, C_CHUNK), lambda i: (0, 0, 0)),
            pl.BlockSpec((C_CHUNK, C_GROUPS), lambda i: (0, 0)),
            pl.BlockSpec((C_WIDTH, d), lambda i: (0, 0)),
        ],
        out_specs=pl.BlockSpec((tm, d), lambda i: (i, 0)),
        out_shape=jax.ShapeDtypeStruct((n, d), F32),
        scratch_shapes=[pltpu.VMEM((tm, C_WIDTH), BF16)],
        compiler_params=pltpu.CompilerParams(
            dimension_semantics=("arbitrary",), vmem_limit_bytes=VMEM_LIMIT),
        name="sgu",
    )(x, norm_g.reshape(1, d), w_in.astype(BF16), v_gain.reshape(1, C_WIDTH), w_spatial,
      b_spatial.T, w_out.astype(BF16))


def _peer_scores_kernel(x_ref, g_ref, wq_ref, sk_ref, hn_ref, sc_ref):
    hn = _rms(x_ref[...], g_ref[...])
    hn_ref[...] = hn
    q = jnp.dot(hn.astype(BF16), wq_ref[...], preferred_element_type=F32)
    for hp in range(2 * PEER_HEADS):
        cols = slice(hp * PEER_NKEYS, (hp + 1) * PEER_NKEYS)
        sc_ref[cols, :] = _dot_nt(sk_ref[hp].astype(BF16), q[:, cols].astype(BF16))


def peer_scores(x, norm_g, w_query, sub_keys, *, tm=512):
    n, d = x.shape
    nq = w_query.shape[1]
    tm = min(tm, n)
    sk = sub_keys.reshape(2 * PEER_HEADS, PEER_NKEYS, PEER_NKEYS)
    return pl.pallas_call(
        _peer_scores_kernel,
        grid=(n // tm,),
        in_specs=[
            pl.BlockSpec((tm, d), lambda i: (i, 0)),
            pl.BlockSpec((1, d), lambda i: (0, 0)),
            pl.BlockSpec((d, nq), lambda i: (0, 0)),
            pl.BlockSpec(sk.shape, lambda i: (0, 0, 0)),
        ],
        out_specs=[pl.BlockSpec((tm, d), lambda i: (i, 0)),
                   pl.BlockSpec((nq, tm), lambda i: (0, i))],
        out_shape=[jax.ShapeDtypeStruct((n, d), F32), jax.ShapeDtypeStruct((nq, n), F32)],
        compiler_params=pltpu.CompilerParams(
            dimension_semantics=("arbitrary",), vmem_limit_bytes=VMEM_LIMIT),
        name="peer_scores",
    )(x, norm_g.reshape(1, d), w_query.astype(BF16), sk)


SUBLANES = 8


def _peer_topk_kernel(sc_ref, idx_ref, gate_ref):
    tt = sc_ref.shape[1]
    key = lax.broadcasted_iota(jnp.int32, (PEER_NKEYS, tt), 0).astype(F32)
    row16 = lax.broadcasted_iota(jnp.int32, (PEER_TOPK, tt), 0)
    row8 = lax.broadcasted_iota(jnp.int32, (SUBLANES, tt), 0)
    row8f = row8.astype(F32)
    ninf = jnp.float32(-jnp.inf)

    def extract16(s):
        vals = jnp.zeros((PEER_TOPK, tt), F32)
        ids = jnp.zeros((PEER_TOPK, tt), F32)
        for k in range(PEER_TOPK):
            m = jnp.max(s, axis=0, keepdims=True)
            am = jnp.min(jnp.where(s == m, key, float(PEER_NKEYS)), axis=0, keepdims=True)
            s = jnp.where(key == am, ninf, s)
            vals = jnp.where(row16 == k, m, vals)
            ids = jnp.where(row16 == k, am, ids)
        return vals, ids

    def head(h, carry):
        off = pl.multiple_of(h * 2 * PEER_NKEYS, 2 * PEER_NKEYS)
        v1, i1 = extract16(sc_ref[pl.ds(off, PEER_NKEYS), :])
        v2, i2 = extract16(sc_ref[pl.ds(off + PEER_NKEYS, PEER_NKEYS), :])
        e1 = i1 * float(PEER_NKEYS)
        cand = [v1[0:1] + v2]
        eid = [e1[0:1] + i2]
        pos = [row16.astype(F32)]
        for i in range(1, SUBLANES):
            keep = row8 < (PEER_TOPK // (i + 1))
            cand.append(jnp.where(keep, v1[i:i + 1] + v2[0:SUBLANES], ninf))
            eid.append(e1[i:i + 1] + i2[0:SUBLANES])
            pos.append(row8f + float(i * PEER_TOPK))
        cand.append(v1[SUBLANES:] + v2[0:1])
        eid.append(e1[SUBLANES:] + i2[0:1])
        pos.append((row8f + float(SUBLANES)) * float(PEER_TOPK))
        cand = jnp.concatenate(cand, axis=0)
        eid = jnp.concatenate(eid, axis=0)
        pos = jnp.concatenate(pos, axis=0)
        top = jnp.zeros((PEER_TOPK, tt), F32)
        idx = jnp.zeros((PEER_TOPK, tt), F32)
        for k in range(PEER_TOPK):
            m = jnp.max(cand, axis=0, keepdims=True)
            p = jnp.min(jnp.where(cand == m, pos, 1e9), axis=0, keepdims=True)
            hit = pos == p
            e = jnp.max(jnp.where(hit, eid, -1.0), axis=0, keepdims=True)
            cand = jnp.where(hit, ninf, cand)
            top = jnp.where(row16 == k, m, top)
            idx = jnp.where(row16 == k, e, idx)
        w = jnp.exp(top - top[0:1])
        out = pl.ds(pl.multiple_of(h * PEER_TOPK, PEER_TOPK), PEER_TOPK)
        idx_ref[out, :] = idx.astype(jnp.int32)
        gate_ref[out, :] = w / jnp.sum(w, axis=0, keepdims=True)
        return carry

    lax.fori_loop(0, PEER_HEADS, head, 0)


def peer_topk(scores, *, tt=128):
    rows, n = scores.shape
    idx_t, gate_t = pl.pallas_call(
        _peer_topk_kernel,
        grid=(n // tt,),
        in_specs=[pl.BlockSpec((rows, tt), lambda i: (0, i))],
        out_specs=[pl.BlockSpec((PEER_PICKS, tt), lambda i: (0, i)),
                   pl.BlockSpec((PEER_PICKS, tt), lambda i: (0, i))],
        out_shape=[jax.ShapeDtypeStruct((PEER_PICKS, n), jnp.int32),
                   jax.ShapeDtypeStruct((PEER_PICKS, n), F32)],
        compiler_params=pltpu.CompilerParams(
            dimension_semantics=("arbitrary",), vmem_limit_bytes=VMEM_LIMIT),
        name="peer_topk",
    )(scores)
    return idx_t.T, gate_t.T


def pack_expert_table(down, up):
    e, d = down.shape
    def pairs(t):
        return t.astype(BF16).reshape(e, 2, d // 2).transpose(0, 2, 1)
    both = jnp.concatenate([pairs(down), pairs(up)], axis=1)
    words = lax.bitcast_convert_type(both, jnp.uint32)
    return words.reshape(e, d // LANES, 1, LANES)


def _row_copy(tab_ref, buf_ref, sem_ref, expert, slot, row):
    return pltpu.make_async_copy(tab_ref.at[expert],
                                 buf_ref.at[slot, :, pl.ds(row, 1), :], sem_ref.at[slot])


APPLY_TOKENS = 8


def _peer_apply_kernel(idx0_ref, idx1_ref, hn_ref, gate_ref, res_ref, tab_ref, o_ref, buf_ref,
                       sem_ref):
    step = pl.program_id(0)
    nsteps = pl.num_programs(0)
    tb = APPLY_TOKENS
    rows = tb * PEER_PICKS

    def issue_token(idx_ref, first_token, slot, t):
        for j in range(PEER_PICKS):
            _row_copy(tab_ref, buf_ref, sem_ref, idx_ref[first_token + t, j], slot,
                      t * PEER_PICKS + j).start(priority=j % 2)

    def wait(slot):
        pltpu.make_async_copy(buf_ref.at[slot], buf_ref.at[slot], sem_ref.at[slot]).wait()

    lane_even = (lax.broadcasted_iota(jnp.int32, (1, LANES), 1) % 2) == 0
    tok_of_row = lax.broadcasted_iota(jnp.int32, (2 * tb, 1), 0) % tb

    def compute(slot, first_token, issue_next):
        tok = pl.ds(first_token, tb)
        x = hn_ref[tok, :]
        xx = jnp.concatenate([x[:, :HALF], x[:, HALF:]], axis=0).astype(BF16)
        tbls = []
        rsel = jnp.zeros((2 * tb, 2 * PEER_PICKS), F32)
        for t in range(tb):
            words = jnp.concatenate(
                [buf_ref[slot, c, pl.ds(t * PEER_PICKS, PEER_PICKS), :] for c in range(D_MODEL // LANES)],
                axis=1)
            tbl = pltpu.bitcast(words, BF16)
            tbls.append(tbl)
            rsel = rsel + jnp.where(tok_of_row == t, _dot_nt(xx, tbl[:, :HALF]), 0.0)
            issue_next(t)
        coefs = []
        for c in range(2 * PEER_PICKS // LANES):
            cols = slice(c * LANES, (c + 1) * LANES)
            part = jnp.where(lane_even, rsel[:tb, cols], rsel[tb:, cols])
            hid = part + jnp.where(lane_even, pltpu.roll(part, LANES - 1, 1), pltpu.roll(part, 1, 1))
            coefs.append(gate_ref[tok, cols] * _gelu(hid))
        coef = jnp.concatenate(coefs, axis=1)
        even2 = jnp.concatenate([lane_even, lane_even], axis=1)
        cc = jnp.concatenate([jnp.where(even2, coef, 0.0), jnp.where(even2, 0.0, coef)],
                             axis=0).astype(BF16)
        ysel = jnp.zeros((2 * tb, HALF), F32)
        for t in range(tb):
            ysel = ysel + jnp.where(tok_of_row == t,
                                    jnp.dot(cc, tbls[t][:, HALF:], preferred_element_type=F32), 0.0)
        y = jnp.concatenate([ysel[:tb], ysel[tb:]], axis=1)
        o_ref[tok, :] = res_ref[tok, :] + y

    @pl.when(step == 0)
    def _():
        for t in range(tb):
            issue_token(idx0_ref, 0, 0, t)

    wait(0)
    compute(0, 0, functools.partial(issue_token, idx0_ref, tb, 1))
    wait(1)
    compute(1, tb, functools.partial(issue_token, idx1_ref, 0, 0))

    @pl.when(step == nsteps - 1)
    def _():
        wait(0)


def peer_apply(idx, hn, gate, res, table):
    n, d = hn.shape
    tb = 2 * APPLY_TOKENS
    nsteps = n // tb
    gate2 = jnp.repeat(gate, 2, axis=1)
    smem = pltpu.SMEM
    return pl.pallas_call(
        _peer_apply_kernel,
        grid=(nsteps,),
        in_specs=[
            pl.BlockSpec((tb, PEER_PICKS), lambda i: (i, 0), memory_space=smem),
            pl.BlockSpec((tb, PEER_PICKS), lambda i: (jnp.minimum(i + 1, nsteps - 1), 0),
                         memory_space=smem),
            pl.BlockSpec((tb, d), lambda i: (i, 0)),
            pl.BlockSpec((tb, 2 * PEER_PICKS), lambda i: (i, 0)),
            pl.BlockSpec((tb, d), lambda i: (i, 0)),
            pl.BlockSpec(memory_space=pl.ANY),
        ],
        out_specs=pl.BlockSpec((tb, d), lambda i: (i, 0)),
        out_shape=jax.ShapeDtypeStruct((n, d), F32),
        scratch_shapes=[pltpu.VMEM((2, d // LANES, APPLY_TOKENS * PEER_PICKS, LANES), jnp.uint32),
                        pltpu.SemaphoreType.DMA((2,))],
        compiler_params=pltpu.CompilerParams(
            dimension_semantics=("arbitrary",), vmem_limit_bytes=VMEM_LIMIT),
        name="peer_apply",
    )(idx, idx, hn, gate2, res, table)


def peer_layer(x, norm_g, w_query, sub_keys, down, up):
    hn, scores = peer_scores(x, norm_g, w_query, sub_keys)
    idx, gate = peer_topk(scores)
    return peer_apply(idx, hn, gate, x, pack_expert_table(down, up))


def attention_layer(x, batch, seq, norm_g, w_in, a_q_gain, a_k_gain, b_q_gain, b_k_gain,
                    lam_q1, lam_k1, lam_q2, lam_k2, b_sub_gain, w_out, lambda_init):
    proj = norm_matmul(x, norm_g, w_in)
    na = A_HEADS // 2
    tile2 = lambda g: jnp.concatenate([g, g])
    out_a = pair_attention(proj, batch, seq, "dilated", 0, na, 2 * na, na,
                           tile2(a_q_gain), tile2(a_k_gain))
    out_b = pair_attention(proj, batch, seq, "diff", 3 * na, 3 * na + B_HEADS, 3 * na + 2 * B_HEADS,
                           B_HEADS, b_q_gain.reshape(-1), b_k_gain.reshape(-1),
                           extras=(lam_q1, lam_k1, lam_q2, lam_k2, b_sub_gain),
                           lambda_init=lambda_init)
    return out_proj(x, out_a, out_b, w_out)


def kernel(x, attn_norm_g, attn_w_in, a_q_gain, a_k_gain, b_q_gain, b_k_gain, lam_q1, lam_k1,
           lam_q2, lam_k2, b_sub_gain, attn_w_out, sgu_norm_g, sgu_w_in, sgu_v_gain, sgu_w_spatial,
           sgu_b_spatial, sgu_w_out, ffn_norm_g, peer_w_query, peer_sub_keys, peer_down, peer_up):
    batch, seq, d = x.shape
    depth = ffn_norm_g.shape[0]
    h = x.reshape(batch * seq, d)
    for layer in range(depth):
        i = layer // 2
        if layer % 2 == 0:
            lambda_init = 0.8 - 0.6 * math.exp(-0.3 * layer)
            h = attention_layer(h, batch, seq, attn_norm_g[i], attn_w_in[i], a_q_gain[i],
                                a_k_gain[i], b_q_gain[i], b_k_gain[i], lam_q1[i], lam_k1[i],
                                lam_q2[i], lam_k2[i], b_sub_gain[i], attn_w_out[i], lambda_init)
        else:
            h = sgu_layer(h, sgu_norm_g[i], sgu_w_in[i], sgu_v_gain[i], sgu_w_spatial[i],
                          sgu_b_spatial[i], sgu_w_out[i])
        h = peer_layer(h, ffn_norm_g[layer], peer_w_query[layer], peer_sub_keys[layer],
                       peer_down[layer], peer_up[layer])
    return h.reshape(batch, seq, d)
```

```python
import functools
import math

import jax
import jax.numpy as jnp
from jax import lax
from jax.experimental import pallas as pl
from jax.experimental.pallas import tpu as pltpu
from jax.experimental.pallas import tpu_sc as plsc

D_MODEL = 1024
HEAD_DIM = 64
EPS = 1e-6
NEG = -1e30
A_HEADS = 8
B_HEADS = 4
C_CHUNK = 128
C_GROUPS = 8
C_WIDTH = 2 * D_MODEL
C_GROUP_DIM = C_WIDTH // C_GROUPS
PEER_HEADS = 8
PEER_NKEYS = 128
PEER_TOPK = 16
PEER_PICKS = PEER_HEADS * PEER_TOPK

LANES = 128
HALF = D_MODEL // 2
VMEM_LIMIT = 56 * 1024 * 1024

BF16 = jnp.bfloat16
F32 = jnp.float32


def _gelu(x):
    return 0.5 * x * (1.0 + jnp.tanh(math.sqrt(2.0 / math.pi) * (x + 0.044715 * (x * x * x))))


def _rms(x, g):
    return x * lax.rsqrt(jnp.mean(x * x, axis=-1, keepdims=True) + EPS) * g


def _dot_nt(a, b):
    return lax.dot_general(a, b, (((1,), (1,)), ((), ())), preferred_element_type=F32)


def _norm_matmul_kernel(x_ref, g_ref, w_ref, o_ref, xn_ref):
    @pl.when(pl.program_id(1) == 0)
    def _():
        xn_ref[...] = _rms(x_ref[...], g_ref[...]).astype(BF16)

    o_ref[...] = jnp.dot(xn_ref[...], w_ref[...], preferred_element_type=F32)


def norm_matmul(x, g, w, *, tm=512, tn=512):
    n, d = x.shape
    nout = w.shape[1]
    tm = min(tm, n)
    return pl.pallas_call(
        _norm_matmul_kernel,
        grid=(n // tm, nout // tn),
        in_specs=[
            pl.BlockSpec((tm, d), lambda i, j: (i, 0)),
            pl.BlockSpec((1, d), lambda i, j: (0, 0)),
            pl.BlockSpec((d, tn), lambda i, j: (0, j)),
        ],
        out_specs=pl.BlockSpec((tm, tn), lambda i, j: (i, j)),
        out_shape=jax.ShapeDtypeStruct((n, nout), F32),
        scratch_shapes=[pltpu.VMEM((tm, d), BF16)],
        compiler_params=pltpu.CompilerParams(
            dimension_semantics=("arbitrary", "arbitrary"), vmem_limit_bytes=VMEM_LIMIT),
        name="norm_matmul",
    )(x, g.reshape(1, d), w.astype(BF16))


def _pair_norm(t, gain, lo):
    sq = t * t
    s_lo = jnp.sum(jnp.where(lo, sq, 0.0), axis=-1, keepdims=True)
    s_hi = jnp.sum(jnp.where(lo, 0.0, sq), axis=-1, keepdims=True)
    ms = jnp.where(lo, s_lo, s_hi) * (1.0 / HEAD_DIM)
    return t * lax.rsqrt(ms + EPS) * gain


def _attn_kernel(*refs, mode, tq, tk, lambda_init):
    if mode == "dilated":
        q_ref, k_ref, v_ref, qg_ref, kg_ref, o_ref, kn_ref = refs
    else:
        (q_ref, k_ref, v_ref, qg_ref, kg_ref, lq1_ref, lk1_ref, lq2_ref, lk2_ref, sg_ref,
         o_ref, kn_ref) = refs
    i = pl.program_id(2)
    lo = lax.broadcasted_iota(jnp.int32, (1, LANES), 1) < HEAD_DIM

    @pl.when(i == 0)
    def _():
        kn_ref[...] = _pair_norm(k_ref[...], kg_ref[...], lo).astype(BF16)

    qn = _pair_norm(q_ref[...], qg_ref[...], lo) * (HEAD_DIM ** -0.5)
    qa = jnp.where(lo, qn, 0.0).astype(BF16)
    qb = jnp.where(lo, 0.0, qn).astype(BF16)
    rows = i * tq + lax.broadcasted_iota(jnp.int32, (tq, tk), 0)
    col0 = lax.broadcasted_iota(jnp.int32, (tq, tk), 1)

    def body(j, carry):
        ma, la, acca, mb, lb, accb = carry
        off = pl.multiple_of(j * tk, tk)
        kb = kn_ref[pl.ds(off, tk), :]
        vb = v_ref[pl.ds(off, tk), :].astype(BF16)
        dist = rows - (col0 + j * tk)
        if mode == "dilated":
            cnt = ((dist <= 128).astype(F32)
                   + ((dist <= 512) & ((dist & 3) == 0)).astype(F32)
                   + ((dist & 15) == 0).astype(F32))
            cnt = jnp.where(dist >= 0, cnt, 0.0)
        else:
            cnt = (dist >= 0).astype(F32)
        valid = cnt > 0.0

        def update(qh, m, l, acc):
            s = jnp.where(valid, _dot_nt(qh, kb), NEG)
            m_new = jnp.maximum(m, jnp.max(s, axis=-1, keepdims=True))
            alpha = jnp.exp(m - m_new)
            p = cnt * jnp.exp(s - m_new)
            l_new = alpha * l + jnp.sum(p, axis=-1, keepdims=True)
            acc_new = alpha * acc + jnp.dot(p.astype(BF16), vb, preferred_element_type=F32)
            return m_new, l_new, acc_new

        ma, la, acca = update(qa, ma, la, acca)
        mb, lb, accb = update(qb, mb, lb, accb)
        return ma, la, acca, mb, lb, accb

    m0 = jnp.full((tq, 1), NEG, F32)
    l0 = jnp.zeros((tq, 1), F32)
    a0 = jnp.zeros((tq, LANES), F32)
    nkv = (i * tq + tq + tk - 1) // tk
    ma, la, acca, mb, lb, accb = lax.fori_loop(0, nkv, body, (m0, l0, a0, m0, l0, a0))
    oa = acca / la
    ob = accb / lb
    if mode == "dilated":
        o_ref[...] = jnp.where(lo, oa, ob)
    else:
        lam = (jnp.exp(jnp.sum(lq1_ref[...] * lk1_ref[...], axis=-1, keepdims=True))
               - jnp.exp(jnp.sum(lq2_ref[...] * lk2_ref[...], axis=-1, keepdims=True))
               + lambda_init)
        o = oa - lam * ob
        o_ref[...] = _rms(o, sg_ref[...]) * (1.0 - lambda_init)


def pair_attention(proj, batch, seq, mode, qcol, kcol, vcol, npairs, q_gain, k_gain, extras=(),
                   lambda_init=0.0, tq=256, tk=256):
    nq = seq // tq
    small = [q_gain.reshape(1, LANES), k_gain.reshape(1, LANES)] + [e.reshape(1, -1) for e in extras]
    small_specs = [pl.BlockSpec(s.shape, lambda b, p, i: (0, 0)) for s in small]
    kernel = functools.partial(_attn_kernel, mode=mode, tq=tq, tk=tk, lambda_init=lambda_init)
    return pl.pallas_call(
        kernel,
        grid=(batch, npairs, nq),
        in_specs=[
            pl.BlockSpec((tq, LANES), lambda b, p, i: (b * nq + i, qcol + p)),
            pl.BlockSpec((seq, LANES), lambda b, p, i: (b, kcol + p)),
            pl.BlockSpec((seq, LANES), lambda b, p, i: (b, vcol + p)),
        ] + small_specs,
        out_specs=pl.BlockSpec((tq, LANES), lambda b, p, i: (b * nq + i, p)),
        out_shape=jax.ShapeDtypeStruct((batch * seq, npairs * LANES), F32),
        scratch_shapes=[pltpu.VMEM((seq, LANES), BF16)],
        compiler_params=pltpu.CompilerParams(
            dimension_semantics=("arbitrary", "arbitrary", "arbitrary"),
            vmem_limit_bytes=VMEM_LIMIT),
        name="attn_" + mode,
    )(proj, proj, proj, *small)


def _out_proj_kernel(x_ref, a_ref, b_ref, wa_ref, wb_ref, o_ref):
    o_ref[...] = (x_ref[...]
                  + jnp.dot(a_ref[...].astype(BF16), wa_ref[...], preferred_element_type=F32)
                  + jnp.dot(b_ref[...].astype(BF16), wb_ref[...], preferred_element_type=F32))


def out_proj(x, a, b, w, *, tm=512):
    n, d = x.shape
    ka, kb = a.shape[1], b.shape[1]
    w = w.astype(BF16)
    tm = min(tm, n)
    return pl.pallas_call(
        _out_proj_kernel,
        grid=(n // tm,),
        in_specs=[
            pl.BlockSpec((tm, d), lambda i: (i, 0)),
            pl.BlockSpec((tm, ka), lambda i: (i, 0)),
            pl.BlockSpec((tm, kb), lambda i: (i, 0)),
            pl.BlockSpec((ka, d), lambda i: (0, 0)),
            pl.BlockSpec((kb, d), lambda i: (0, 0)),
        ],
        out_specs=pl.BlockSpec((tm, d), lambda i: (i, 0)),
        out_shape=jax.ShapeDtypeStruct((n, d), F32),
        compiler_params=pltpu.CompilerParams(
            dimension_semantics=("arbitrary",), vmem_limit_bytes=VMEM_LIMIT),
        name="out_proj",
    )(x, a, b, w[:ka], w[ka:])


def _sgu_kernel(x_ref, g_ref, win_ref, vg_ref, ws_ref, bs_ref, wout_ref, o_ref, gated_ref, *, tm):
    x = x_ref[...]
    xn = _rms(x, g_ref[...]).astype(BF16)
    z = _gelu(jnp.dot(xn, win_ref[...], preferred_element_type=F32))
    u = z[:, :C_WIDTH]
    v = _rms(z[:, C_WIDTH:], vg_ref[...]).astype(BF16)
    r = lax.broadcasted_iota(jnp.int32, (C_CHUNK, C_CHUNK), 0)
    c = lax.broadcasted_iota(jnp.int32, (C_CHUNK, C_CHUNK), 1)
    causal = c <= r
    for grp in range(C_GROUPS):
        ws = jnp.where(causal, ws_ref[grp], 0.0).astype(BF16)
        bias = bs_ref[:, grp:grp + 1]
        cols = slice(grp * C_GROUP_DIM, (grp + 1) * C_GROUP_DIM)
        for ch in range(tm // C_CHUNK):
            rws = slice(ch * C_CHUNK, (ch + 1) * C_CHUNK)
            gate = jnp.dot(ws, v[rws, cols], preferred_element_type=F32) + bias
            gated_ref[rws, cols] = (u[rws, cols] * gate).astype(BF16)
    o_ref[...] = x + jnp.dot(gated_ref[...], wout_ref[...], preferred_element_type=F32)


def sgu_layer(x, norm_g, w_in, v_gain, w_spatial, b_spatial, w_out, *, tm=256):
    n, d = x.shape
    kernel = functools.partial(_sgu_kernel, tm=tm)
    return pl.pallas_call(
        kernel,
        grid=(n // tm,),
        in_specs=[
            pl.BlockSpec((tm, d), lambda i: (i, 0)),
            pl.BlockSpec((1, d), lambda i: (0, 0)),
            pl.BlockSpec((d, 2 * C_WIDTH), lambda i: (0, 0)),
            pl.BlockSpec((1, C_WIDTH), lambda i: (0, 0)),
            pl.BlockSpec((C_GROUPS, C_CHUNK, C_CHUNK), lambda i: (0, 0, 0)),
            pl.BlockSpec((C_CHUNK, C_GROUPS), lambda i: (0, 0)),
            pl.BlockSpec((C_WIDTH, d), lambda i: (0, 0)),
        ],
        out_specs=pl.BlockSpec((tm, d), lambda i: (i, 0)),
        out_shape=jax.ShapeDtypeStruct((n, d), F32),
        scratch_shapes=[pltpu.VMEM((tm, C_WIDTH), BF16)],
        compiler_params=pltpu.CompilerParams(
            dimension_semantics=("arbitrary",), vmem_limit_bytes=VMEM_LIMIT),
        name="sgu",
    )(x, norm_g.reshape(1, d), w_in.astype(BF16), v_gain.reshape(1, C_WIDTH), w_spatial,
      b_spatial.T, w_out.astype(BF16))


def _peer_scores_kernel(x_ref, g_ref, wq_ref, sk_ref, hn_ref, sc_ref):
    hn = _rms(x_ref[...], g_ref[...])
    hn_ref[...] = hn
    q = jnp.dot(hn.astype(BF16), wq_ref[...], preferred_element_type=F32)
    for hp in range(2 * PEER_HEADS):
        cols = slice(hp * PEER_NKEYS, (hp + 1) * PEER_NKEYS)
        sc_ref[cols, :] = _dot_nt(sk_ref[hp].astype(BF16), q[:, cols].astype(BF16))


def peer_scores(x, norm_g, w_query, sub_keys, *, tm=512):
    n, d = x.shape
    nq = w_query.shape[1]
    tm = min(tm, n)
    sk = sub_keys.reshape(2 * PEER_HEADS, PEER_NKEYS, PEER_NKEYS)
    return pl.pallas_call(
        _peer_scores_kernel,
        grid=(n // tm,),
        in_specs=[
            pl.BlockSpec((tm, d), lambda i: (i, 0)),
            pl.BlockSpec((1, d), lambda i: (0, 0)),
            pl.BlockSpec((d, nq), lambda i: (0, 0)),
            pl.BlockSpec(sk.shape, lambda i: (0, 0, 0)),
        ],
        out_specs=[pl.BlockSpec((tm, d), lambda i: (i, 0)),
                   pl.BlockSpec((nq, tm), lambda i: (0, i))],
        out_shape=[jax.ShapeDtypeStruct((n, d), F32), jax.ShapeDtypeStruct((nq, n), F32)],
        compiler_params=pltpu.CompilerParams(
            dimension_semantics=("arbitrary",), vmem_limit_bytes=VMEM_LIMIT),
        name="peer_scores",
    )(x, norm_g.reshape(1, d), w_query.astype(BF16), sk)


SUBLANES = 8


def _peer_topk_kernel(sc_ref, idx_ref, gate_ref):
    tt = sc_ref.shape[1]
    key = lax.broadcasted_iota(jnp.int32, (PEER_NKEYS, tt), 0).astype(F32)
    row16 = lax.broadcasted_iota(jnp.int32, (PEER_TOPK, tt), 0)
    row8 = lax.broadcasted_iota(jnp.int32, (SUBLANES, tt), 0)
    row8f = row8.astype(F32)
    ninf = jnp.float32(-jnp.inf)

    def extract16(s):
        vals = jnp.zeros((PEER_TOPK, tt), F32)
        ids = jnp.zeros((PEER_TOPK, tt), F32)
        for k in range(PEER_TOPK):
            m = jnp.max(s, axis=0, keepdims=True)
            am = jnp.min(jnp.where(s == m, key, float(PEER_NKEYS)), axis=0, keepdims=True)
            s = jnp.where(key == am, ninf, s)
            vals = jnp.where(row16 == k, m, vals)
            ids = jnp.where(row16 == k, am, ids)
        return vals, ids

    def head(h, carry):
        off = pl.multiple_of(h * 2 * PEER_NKEYS, 2 * PEER_NKEYS)
        v1, i1 = extract16(sc_ref[pl.ds(off, PEER_NKEYS), :])
        v2, i2 = extract16(sc_ref[pl.ds(off + PEER_NKEYS, PEER_NKEYS), :])
        e1 = i1 * float(PEER_NKEYS)
        cand = [v1[0:1] + v2]
        eid = [e1[0:1] + i2]
        pos = [row16.astype(F32)]
        for i in range(1, SUBLANES):
            keep = row8 < (PEER_TOPK // (i + 1))
            cand.append(jnp.where(keep, v1[i:i + 1] + v2[0:SUBLANES], ninf))
            eid.append(e1[i:i + 1] + i2[0:SUBLANES])
            pos.append(row8f + float(i * PEER_TOPK))
        cand.append(v1[SUBLANES:] + v2[0:1])
        eid.append(e1[SUBLANES:] + i2[0:1])
        pos.append((row8f + float(SUBLANES)) * float(PEER_TOPK))
        cand = jnp.concatenate(cand, axis=0)
        eid = jnp.concatenate(eid, axis=0)
        pos = jnp.concatenate(pos, axis=0)
        top = jnp.zeros((PEER_TOPK, tt), F32)
        idx = jnp.zeros((PEER_TOPK, tt), F32)
        for k in range(PEER_TOPK):
            m = jnp.max(cand, axis=0, keepdims=True)
            p = jnp.min(jnp.where(cand == m, pos, 1e9), axis=0, keepdims=True)
            hit = pos == p
            e = jnp.max(jnp.where(hit, eid, -1.0), axis=0, keepdims=True)
            cand = jnp.where(hit, ninf, cand)
            top = jnp.where(row16 == k, m, top)
            idx = jnp.where(row16 == k, e, idx)
        w = jnp.exp(top - top[0:1])
        out = pl.ds(pl.multiple_of(h * PEER_TOPK, PEER_TOPK), PEER_TOPK)
        idx_ref[out, :] = idx.astype(jnp.int32)
        gate_ref[out, :] = w / jnp.sum(w, axis=0, keepdims=True)
        return carry

    lax.fori_loop(0, PEER_HEADS, head, 0)


def peer_topk(scores, *, tt=128):
    rows, n = scores.shape
    idx_t, gate_t = pl.pallas_call(
        _peer_topk_kernel,
        grid=(n // tt,),
        in_specs=[pl.BlockSpec((rows, tt), lambda i: (0, i))],
        out_specs=[pl.BlockSpec((PEER_PICKS, tt), lambda i: (0, i)),
                   pl.BlockSpec((PEER_PICKS, tt), lambda i: (0, i))],
        out_shape=[jax.ShapeDtypeStruct((PEER_PICKS, n), jnp.int32),
                   jax.ShapeDtypeStruct((PEER_PICKS, n), F32)],
        compiler_params=pltpu.CompilerParams(
            dimension_semantics=("arbitrary",), vmem_limit_bytes=VMEM_LIMIT),
        name="peer_topk",
    )(scores)
    return idx_t.T, gate_t.T


def pack_expert_table(down, up):
    e, d = down.shape
    def pairs(t):
        return t.astype(BF16).reshape(e, 2, d // 2).transpose(0, 2, 1)
    both = jnp.concatenate([pairs(down), pairs(up)], axis=1)
    return lax.bitcast_convert_type(both, jnp.int32).reshape(e, d // LANES, LANES)


def _row_copy(tab_ref, buf_ref, sem_ref, expert, slot, row):
    return pltpu.make_async_copy(tab_ref.at[expert],
                                 buf_ref.at[slot, :, pl.ds(row, 1), :], sem_ref.at[slot])


APPLY_TOKENS = 8
CHUNKS = D_MODEL // LANES
STEP_TOKENS = 4 * APPLY_TOKENS


def _peer_apply_kernel(idx0_ref, idx1_ref, hn_ref, gate_ref, res_ref, tab_ref, st_ref, o_ref,
                       buf_ref, sem_ref):
    step = pl.program_id(0)
    nsteps = pl.num_programs(0)
    tb = APPLY_TOKENS

    def issue_token(idx_ref, first_token, slot, t):
        for j in range(PEER_PICKS):
            _row_copy(tab_ref, buf_ref, sem_ref, idx_ref[first_token + t, j], slot,
                      t * PEER_PICKS + j).start(priority=j % 2)

    def wait(slot):
        pltpu.make_async_copy(buf_ref.at[slot], buf_ref.at[slot], sem_ref.at[slot]).wait()

    def copied_words(slot, t):
        return [buf_ref[slot, c, pl.ds(t * PEER_PICKS, PEER_PICKS), :] for c in range(CHUNKS)]

    def staged_words(group, t):
        base = (group * tb + t) * PEER_PICKS * CHUNKS
        return [st_ref[pl.ds(base + c, PEER_PICKS, stride=CHUNKS), :] for c in range(CHUNKS)]

    lane_even = (lax.broadcasted_iota(jnp.int32, (1, LANES), 1) % 2) == 0
    even2 = jnp.concatenate([lane_even, lane_even], axis=1)
    tok_of_row = lax.broadcasted_iota(jnp.int32, (2 * tb, 1), 0) % tb

    def compute(groups, between):
        xxs, rsels, tbls = [], [], []
        for first, _ in groups:
            x = hn_ref[pl.ds(first, tb), :]
            xxs.append(jnp.concatenate([x[:, :HALF], x[:, HALF:]], axis=0).astype(BF16))
            rsels.append(jnp.zeros((2 * tb, 2 * PEER_PICKS), F32))
            tbls.append([])
        for t in range(tb):
            for g, (_, words_fn) in enumerate(groups):
                words = jnp.concatenate(words_fn(t), axis=1)
                tbl = pltpu.bitcast(words, BF16)
                tbls[g].append(tbl)
                rsels[g] = rsels[g] + jnp.where(tok_of_row == t, _dot_nt(xxs[g], tbl[:, :HALF]), 0.0)
            between(t)
        for g, (first, _) in enumerate(groups):
            tok = pl.ds(first, tb)
            coefs = []
            for c in range(2 * PEER_PICKS // LANES):
                cols = slice(c * LANES, (c + 1) * LANES)
                part = jnp.where(lane_even, rsels[g][:tb, cols], rsels[g][tb:, cols])
                hid = part + jnp.where(lane_even, pltpu.roll(part, LANES - 1, 1),
                                       pltpu.roll(part, 1, 1))
                coefs.append(gate_ref[tok, cols] * _gelu(hid))
            coef = jnp.concatenate(coefs, axis=1)
            cc = jnp.concatenate([jnp.where(even2, coef, 0.0), jnp.where(even2, 0.0, coef)],
                                 axis=0).astype(BF16)
            ysel = jnp.zeros((2 * tb, HALF), F32)
            for t in range(tb):
                ysel = ysel + jnp.where(
                    tok_of_row == t,
                    jnp.dot(cc, tbls[g][t][:, HALF:], preferred_element_type=F32), 0.0)
            y = jnp.concatenate([ysel[:tb], ysel[tb:]], axis=1)
            o_ref[tok, :] = res_ref[tok, :] + y

    @pl.when(step == 0)
    def _():
        for t in range(tb):
            issue_token(idx0_ref, 0, 0, t)

    wait(0)
    compute([(0, functools.partial(copied_words, 0)), (2 * tb, functools.partial(staged_words, 0))],
            functools.partial(issue_token, idx0_ref, tb, 1))
    wait(1)
    compute([(tb, functools.partial(copied_words, 1)), (3 * tb, functools.partial(staged_words, 1))],
            functools.partial(issue_token, idx1_ref, 0, 0))

    @pl.when(step == nsteps - 1)
    def _():
        wait(0)


def peer_apply(idx, hn, gate2, res, table, staged):
    n, d = hn.shape
    tb = STEP_TOKENS
    nsteps = n // tb
    st_rows = (tb // 2) * PEER_PICKS * CHUNKS
    smem = pltpu.SMEM
    return pl.pallas_call(
        _peer_apply_kernel,
        grid=(nsteps,),
        in_specs=[
            pl.BlockSpec((tb, PEER_PICKS), lambda i: (i, 0), memory_space=smem),
            pl.BlockSpec((tb, PEER_PICKS), lambda i: (jnp.minimum(i + 1, nsteps - 1), 0),
                         memory_space=smem),
            pl.BlockSpec((tb, d), lambda i: (i, 0)),
            pl.BlockSpec((tb, 2 * PEER_PICKS), lambda i: (i, 0)),
            pl.BlockSpec((tb, d), lambda i: (i, 0)),
            pl.BlockSpec(memory_space=pl.ANY),
            pl.BlockSpec((st_rows, LANES), lambda i: (i, 0)),
        ],
        out_specs=pl.BlockSpec((tb, d), lambda i: (i, 0)),
        out_shape=jax.ShapeDtypeStruct((n, d), F32),
        scratch_shapes=[pltpu.VMEM((2, CHUNKS, APPLY_TOKENS * PEER_PICKS, LANES), jnp.int32),
                        pltpu.SemaphoreType.DMA((2,))],
        compiler_params=pltpu.CompilerParams(
            dimension_semantics=("arbitrary",), vmem_limit_bytes=VMEM_LIMIT),
        name="peer_apply",
    )(idx, idx, hn, gate2, res, table.reshape(table.shape[0], CHUNKS, 1, LANES), staged)


SC_CORES = 2
SC_SUBCORES = 16
SC_CHUNK = 32


def sc_gather_rows(table, idx):
    b = idx.shape[0]
    nw = SC_CORES * SC_SUBCORES
    per_w = b // nw
    nchunks = per_w // SC_CHUNK
    assert per_w * nw == b and nchunks * SC_CHUNK == per_w and nchunks % 2 == 0
    mesh = plsc.VectorSubcoreMesh(core_axis_name="c", subcore_axis_name="s")

    @functools.partial(
        pl.kernel, mesh=mesh,
        out_type=jax.ShapeDtypeStruct((b,) + table.shape[1:], table.dtype),
        scratch_types=[
            pltpu.VMEM((per_w,), jnp.int32),
            pltpu.VMEM((SC_CHUNK,) + table.shape[1:], table.dtype),
            pltpu.VMEM((SC_CHUNK,) + table.shape[1:], table.dtype),
            pltpu.SemaphoreType.DMA,
            pltpu.SemaphoreType.DMA,
        ],
        name="sc_gather_rows",
    )
    def gather(tab_hbm, idx_hbm, out_hbm, idx_v, rows0, rows1, sem0, sem1):
        wid = lax.axis_index("s") * SC_CORES + lax.axis_index("c")
        base = wid * per_w
        pltpu.sync_copy(idx_hbm.at[pl.ds(base, per_w)], idx_v)

        def start(chunk, rows, sem):
            off = pl.multiple_of(chunk * SC_CHUNK, SC_CHUNK)
            pltpu.async_copy(tab_hbm.at[idx_v.at[pl.ds(off, SC_CHUNK)]], rows, sem)

        def finish(chunk, rows, sem):
            off = pl.multiple_of(chunk * SC_CHUNK, SC_CHUNK)
            pltpu.make_async_copy(tab_hbm.at[idx_v.at[pl.ds(off, SC_CHUNK)]], rows, sem).wait()
            pltpu.sync_copy(rows, out_hbm.at[pl.ds(base + off, SC_CHUNK)])

        start(0, rows0, sem0)

        @pl.loop(0, nchunks, step=2)
        def _(i):
            start(i + 1, rows1, sem1)
            finish(i, rows0, sem0)

            @pl.when(i + 2 < nchunks)
            def _():
                start(i + 2, rows0, sem0)

            finish(i + 1, rows1, sem1)

    return gather(table, idx)


PEER_GROUP = 4096


def peer_layer(x, norm_g, w_query, sub_keys, down, up):
    n = x.shape[0]
    hn, scores = peer_scores(x, norm_g, w_query, sub_keys)
    idx, gate = peer_topk(scores)
    gate2 = jnp.repeat(gate, 2, axis=1)
    table = pack_expert_table(down, up)
    tg = min(PEER_GROUP, n)
    outs = []
    for g in range(n // tg):
        rows = slice(g * tg, (g + 1) * tg)
        idx_g = idx[rows]
        idx_staged = idx_g.reshape(tg // STEP_TOKENS, STEP_TOKENS, PEER_PICKS)[:, STEP_TOKENS // 2:]
        staged = sc_gather_rows(table, idx_staged.reshape(-1))
        outs.append(peer_apply(idx_g, hn[rows], gate2[rows], x[rows], table,
                               staged.reshape(-1, LANES)))
    return jnp.concatenate(outs, axis=0)


def attention_layer(x, batch, seq, norm_g, w_in, a_q_gain, a_k_gain, b_q_gain, b_k_gain,
                    lam_q1, lam_k1, lam_q2, lam_k2, b_sub_gain, w_out, lambda_init):
    proj = norm_matmul(x, norm_g, w_in)
    na = A_HEADS // 2
    tile2 = lambda g: jnp.concatenate([g, g])
    out_a = pair_attention(proj, batch, seq, "dilated", 0, na, 2 * na, na,
                           tile2(a_q_gain), tile2(a_k_gain))
    out_b = pair_attention(proj, batch, seq, "diff", 3 * na, 3 * na + B_HEADS, 3 * na + 2 * B_HEADS,
                           B_HEADS, b_q_gain.reshape(-1), b_k_gain.reshape(-1),
                           extras=(lam_q1, lam_k1, lam_q2, lam_k2, b_sub_gain),
                           lambda_init=lambda_init)
    return out_proj(x, out_a, out_b, w_out)


def kernel(x, attn_norm_g, attn_w_in, a_q_gain, a_k_gain, b_q_gain, b_k_gain, lam_q1, lam_k1,
           lam_q2, lam_k2, b_sub_gain, attn_w_out, sgu_norm_g, sgu_w_in, sgu_v_gain, sgu_w_spatial,
           sgu_b_spatial, sgu_w_out, ffn_norm_g, peer_w_query, peer_sub_keys, peer_down, peer_up):
    batch, seq, d = x.shape
    depth = ffn_norm_g.shape[0]
    h = x.reshape(batch * seq, d)
    for layer in range(depth):
        i = layer // 2
        if layer % 2 == 0:
            lambda_init = 0.8 - 0.6 * math.exp(-0.3 * layer)
            h = attention_layer(h, batch, seq, attn_norm_g[i], attn_w_in[i], a_q_gain[i],
                                a_k_gain[i], b_q_gain[i], b_k_gain[i], lam_q1[i], lam_k1[i],
                                lam_q2[i], lam_k2[i], b_sub_gain[i], attn_w_out[i], lambda_init)
        else:
            h = sgu_layer(h, sgu_norm_g[i], sgu_w_in[i], sgu_v_gain[i], sgu_w_spatial[i],
                          sgu_b_spatial[i], sgu_w_out[i])
        h = peer_layer(h, ffn_norm_g[layer], peer_w_query[layer], peer_sub_keys[layer],
                       peer_down[layer], peer_up[layer])
    return h.reshape(batch, seq, d)
```

```python
import functools
import math

import jax
import jax.numpy as jnp
from jax import lax
from jax.experimental import pallas as pl
from jax.experimental.pallas import tpu as pltpu
from jax.experimental.pallas import tpu_sc as plsc

D_MODEL = 1024
HEAD_DIM = 64
EPS = 1e-6
NEG = -1e30
A_HEADS = 8
B_HEADS = 4
C_CHUNK = 128
C_GROUPS = 8
C_WIDTH = 2 * D_MODEL
C_GROUP_DIM = C_WIDTH // C_GROUPS
PEER_HEADS = 8
PEER_NKEYS = 128
PEER_TOPK = 16
PEER_PICKS = PEER_HEADS * PEER_TOPK

LANES = 128
HALF = D_MODEL // 2
VMEM_LIMIT = 56 * 1024 * 1024

BF16 = jnp.bfloat16
F32 = jnp.float32


def _gelu(x):
    return 0.5 * x * (1.0 + jnp.tanh(math.sqrt(2.0 / math.pi) * (x + 0.044715 * (x * x * x))))


def _rms(x, g):
    return x * lax.rsqrt(jnp.mean(x * x, axis=-1, keepdims=True) + EPS) * g


def _dot_nt(a, b):
    return lax.dot_general(a, b, (((1,), (1,)), ((), ())), preferred_element_type=F32)


def _norm_matmul_kernel(x_ref, g_ref, w_ref, o_ref, xn_ref):
    @pl.when(pl.program_id(1) == 0)
    def _():
        xn_ref[...] = _rms(x_ref[...], g_ref[...]).astype(BF16)

    o_ref[...] = jnp.dot(xn_ref[...], w_ref[...], preferred_element_type=F32)


def norm_matmul(x, g, w, *, tm=512, tn=512):
    n, d = x.shape
    nout = w.shape[1]
    tm = min(tm, n)
    return pl.pallas_call(
        _norm_matmul_kernel,
        grid=(n // tm, nout // tn),
        in_specs=[
            pl.BlockSpec((tm, d), lambda i, j: (i, 0)),
            pl.BlockSpec((1, d), lambda i, j: (0, 0)),
            pl.BlockSpec((d, tn), lambda i, j: (0, j)),
        ],
        out_specs=pl.BlockSpec((tm, tn), lambda i, j: (i, j)),
        out_shape=jax.ShapeDtypeStruct((n, nout), F32),
        scratch_shapes=[pltpu.VMEM((tm, d), BF16)],
        compiler_params=pltpu.CompilerParams(
            dimension_semantics=("arbitrary", "arbitrary"), vmem_limit_bytes=VMEM_LIMIT),
        name="norm_matmul",
    )(x, g.reshape(1, d), w.astype(BF16))


def _pair_norm(t, gain, lo):
    sq = t * t
    s_lo = jnp.sum(jnp.where(lo, sq, 0.0), axis=-1, keepdims=True)
    s_hi = jnp.sum(jnp.where(lo, 0.0, sq), axis=-1, keepdims=True)
    ms = jnp.where(lo, s_lo, s_hi) * (1.0 / HEAD_DIM)
    return t * lax.rsqrt(ms + EPS) * gain


def _attn_kernel(*refs, mode, tq, tk, lambda_init):
    if mode == "dilated":
        q_ref, k_ref, v_ref, qg_ref, kg_ref, o_ref, kn_ref = refs
    else:
        (q_ref, k_ref, v_ref, qg_ref, kg_ref, lq1_ref, lk1_ref, lq2_ref, lk2_ref, sg_ref,
         o_ref, kn_ref) = refs
    i = pl.program_id(2)
    lo = lax.broadcasted_iota(jnp.int32, (1, LANES), 1) < HEAD_DIM

    @pl.when(i == 0)
    def _():
        kn_ref[...] = _pair_norm(k_ref[...], kg_ref[...], lo).astype(BF16)

    qn = _pair_norm(q_ref[...], qg_ref[...], lo) * (HEAD_DIM ** -0.5)
    qa = jnp.where(lo, qn, 0.0).astype(BF16)
    qb = jnp.where(lo, 0.0, qn).astype(BF16)
    rows = i * tq + lax.broadcasted_iota(jnp.int32, (tq, tk), 0)
    col0 = lax.broadcasted_iota(jnp.int32, (tq, tk), 1)

    def body(j, carry):
        ma, la, acca, mb, lb, accb = carry
        off = pl.multiple_of(j * tk, tk)
        kb = kn_ref[pl.ds(off, tk), :]
        vb = v_ref[pl.ds(off, tk), :].astype(BF16)
        dist = rows - (col0 + j * tk)
        if mode == "dilated":
            cnt = ((dist <= 128).astype(F32)
                   + ((dist <= 512) & ((dist & 3) == 0)).astype(F32)
                   + ((dist & 15) == 0).astype(F32))
            cnt = jnp.where(dist >= 0, cnt, 0.0)
        else:
            cnt = (dist >= 0).astype(F32)
        valid = cnt > 0.0

        def update(qh, m, l, acc):
            s = jnp.where(valid, _dot_nt(qh, kb), NEG)
            m_new = jnp.maximum(m, jnp.max(s, axis=-1, keepdims=True))
            alpha = jnp.exp(m - m_new)
            p = cnt * jnp.exp(s - m_new)
            l_new = alpha * l + jnp.sum(p, axis=-1, keepdims=True)
            acc_new = alpha * acc + jnp.dot(p.astype(BF16), vb, preferred_element_type=F32)
            return m_new, l_new, acc_new

        ma, la, acca = update(qa, ma, la, acca)
        mb, lb, accb = update(qb, mb, lb, accb)
        return ma, la, acca, mb, lb, accb

    m0 = jnp.full((tq, 1), NEG, F32)
    l0 = jnp.zeros((tq, 1), F32)
    a0 = jnp.zeros((tq, LANES), F32)
    nkv = (i * tq + tq + tk - 1) // tk
    ma, la, acca, mb, lb, accb = lax.fori_loop(0, nkv, body, (m0, l0, a0, m0, l0, a0))
    oa = acca / la
    ob = accb / lb
    if mode == "dilated":
        o_ref[...] = jnp.where(lo, oa, ob)
    else:
        lam = (jnp.exp(jnp.sum(lq1_ref[...] * lk1_ref[...], axis=-1, keepdims=True))
               - jnp.exp(jnp.sum(lq2_ref[...] * lk2_ref[...], axis=-1, keepdims=True))
               + lambda_init)
        o = oa - lam * ob
        o_ref[...] = _rms(o, sg_ref[...]) * (1.0 - lambda_init)


def pair_attention(proj, batch, seq, mode, qcol, kcol, vcol, npairs, q_gain, k_gain, extras=(),
                   lambda_init=0.0, tq=256, tk=256):
    nq = seq // tq
    small = [q_gain.reshape(1, LANES), k_gain.reshape(1, LANES)] + [e.reshape(1, -1) for e in extras]
    small_specs = [pl.BlockSpec(s.shape, lambda b, p, i: (0, 0)) for s in small]
    kernel = functools.partial(_attn_kernel, mode=mode, tq=tq, tk=tk, lambda_init=lambda_init)
    return pl.pallas_call(
        kernel,
        grid=(batch, npairs, nq),
        in_specs=[
            pl.BlockSpec((tq, LANES), lambda b, p, i: (b * nq + i, qcol + p)),
            pl.BlockSpec((seq, LANES), lambda b, p, i: (b, kcol + p)),
            pl.BlockSpec((seq, LANES), lambda b, p, i: (b, vcol + p)),
        ] + small_specs,
        out_specs=pl.BlockSpec((tq, LANES), lambda b, p, i: (b * nq + i, p)),
        out_shape=jax.ShapeDtypeStruct((batch * seq, npairs * LANES), F32),
        scratch_shapes=[pltpu.VMEM((seq, LANES), BF16)],
        compiler_params=pltpu.CompilerParams(
            dimension_semantics=("arbitrary", "arbitrary", "arbitrary"),
            vmem_limit_bytes=VMEM_LIMIT),
        name="attn_" + mode,
    )(proj, proj, proj, *small)


def _out_proj_kernel(x_ref, a_ref, b_ref, wa_ref, wb_ref, o_ref):
    o_ref[...] = (x_ref[...]
                  + jnp.dot(a_ref[...].astype(BF16), wa_ref[...], preferred_element_type=F32)
                  + jnp.dot(b_ref[...].astype(BF16), wb_ref[...], preferred_element_type=F32))


def out_proj(x, a, b, w, *, tm=512):
    n, d = x.shape
    ka, kb = a.shape[1], b.shape[1]
    w = w.astype(BF16)
    tm = min(tm, n)
    return pl.pallas_call(
        _out_proj_kernel,
        grid=(n // tm,),
        in_specs=[
            pl.BlockSpec((tm, d), lambda i: (i, 0)),
            pl.BlockSpec((tm, ka), lambda i: (i, 0)),
            pl.BlockSpec((tm, kb), lambda i: (i, 0)),
            pl.BlockSpec((ka, d), lambda i: (0, 0)),
            pl.BlockSpec((kb, d), lambda i: (0, 0)),
        ],
        out_specs=pl.BlockSpec((tm, d), lambda i: (i, 0)),
        out_shape=jax.ShapeDtypeStruct((n, d), F32),
        compiler_params=pltpu.CompilerParams(
            dimension_semantics=("arbitrary",), vmem_limit_bytes=VMEM_LIMIT),
        name="out_proj",
    )(x, a, b, w[:ka], w[ka:])


def _sgu_kernel(x_ref, g_ref, win_ref, vg_ref, ws_ref, bs_ref, wout_ref, o_ref, gated_ref, *, tm):
    x = x_ref[...]
    xn = _rms(x, g_ref[...]).astype(BF16)
    z = _gelu(jnp.dot(xn, win_ref[...], preferred_element_type=F32))
    u = z[:, :C_WIDTH]
    v = _rms(z[:, C_WIDTH:], vg_ref[...]).astype(BF16)
    r = lax.broadcasted_iota(jnp.int32, (C_CHUNK, C_CHUNK), 0)
    c = lax.broadcasted_iota(jnp.int32, (C_CHUNK, C_CHUNK), 1)
    causal = c <= r
    for grp in range(C_GROUPS):
        ws = jnp.where(causal, ws_ref[grp], 0.0).astype(BF16)
        bias = bs_ref[:, grp:grp + 1]
        cols = slice(grp * C_GROUP_DIM, (grp + 1) * C_GROUP_DIM)
        for ch in range(tm // C_CHUNK):
            rws = slice(ch * C_CHUNK, (ch + 1) * C_CHUNK)
            gate = jnp.dot(ws, v[rws, cols], preferred_element_type=F32) + bias
            gated_ref[rws, cols] = (u[rws, cols] * gate).astype(BF16)
    o_ref[...] = x + jnp.dot(gated_ref[...], wout_ref[...], preferred_element_type=F32)


def sgu_layer(x, norm_g, w_in, v_gain, w_spatial, b_spatial, w_out, *, tm=256):
    n, d = x.shape
    kernel = functools.partial(_sgu_kernel, tm=tm)
    return pl.pallas_call(
        kernel,
        grid=(n // tm,),
        in_specs=[
            pl.BlockSpec((tm, d), lambda i: (i, 0)),
            pl.BlockSpec((1, d), lambda i: (0, 0)),
            pl.BlockSpec((d, 2 * C_WIDTH), lambda i: (0, 0)),
            pl.BlockSpec((1, C_WIDTH), lambda i: (0, 0)),
            pl.BlockSpec((C_GROUPS, C_CHUNK, C_CHUNK), lambda i: (0, 0, 0)),
            pl.BlockSpec((C_CHUNK, C_GROUPS), lambda i: (0, 0)),
            pl.BlockSpec((C_WIDTH, d), lambda i: (0, 0)),
        ],
        out_specs=pl.BlockSpec((tm, d), lambda i: (i, 0)),
        out_shape=jax.ShapeDtypeStruct((n, d), F32),
        scratch_shapes=[pltpu.VMEM((tm, C_WIDTH), BF16)],
        compiler_params=pltpu.CompilerParams(
            dimension_semantics=("arbitrary",), vmem_limit_bytes=VMEM_LIMIT),
        name="sgu",
    )(x, norm_g.reshape(1, d), w_in.astype(BF16), v_gain.reshape(1, C_WIDTH), w_spatial,
      b_spatial.T, w_out.astype(BF16))


def _peer_scores_kernel(x_ref, g_ref, wq_ref, sk_ref, hn_ref, sc_ref):
    hn = _rms(x_ref[...], g_ref[...])
    hn_ref[...] = hn
    q = jnp.dot(hn.astype(BF16), wq_ref[...], preferred_element_type=F32)
    for hp in range(2 * PEER_HEADS):
        cols = slice(hp * PEER_NKEYS, (hp + 1) * PEER_NKEYS)
        sc_ref[cols, :] = _dot_nt(sk_ref[hp].astype(BF16), q[:, cols].astype(BF16))


def peer_scores(x, norm_g, w_query, sub_keys, *, tm=512):
    n, d = x.shape
    nq = w_query.shape[1]
    tm = min(tm, n)
    sk = sub_keys.reshape(2 * PEER_HEADS, PEER_NKEYS, PEER_NKEYS)
    return pl.pallas_call(
        _peer_scores_kernel,
        grid=(n // tm,),
        in_specs=[
            pl.BlockSpec((tm, d), lambda i: (i, 0)),
            pl.BlockSpec((1, d), lambda i: (0, 0)),
            pl.BlockSpec((d, nq), lambda i: (0, 0)),
            pl.BlockSpec(sk.shape, lambda i: (0, 0, 0)),
        ],
        out_specs=[pl.BlockSpec((tm, d), lambda i: (i, 0)),
                   pl.BlockSpec((nq, tm), lambda i: (0, i))],
        out_shape=[jax.ShapeDtypeStruct((n, d), F32), jax.ShapeDtypeStruct((nq, n), F32)],
        compiler_params=pltpu.CompilerParams(
            dimension_semantics=("arbitrary",), vmem_limit_bytes=VMEM_LIMIT),
        name="peer_scores",
    )(x, norm_g.reshape(1, d), w_query.astype(BF16), sk)


SUBLANES = 8


def _peer_topk_kernel(sc_ref, idx_ref, gate_ref):
    tt = sc_ref.shape[1]
    key = lax.broadcasted_iota(jnp.int32, (PEER_NKEYS, tt), 0).astype(F32)
    row16 = lax.broadcasted_iota(jnp.int32, (PEER_TOPK, tt), 0)
    row8 = lax.broadcasted_iota(jnp.int32, (SUBLANES, tt), 0)
    row8f = row8.astype(F32)
    ninf = jnp.float32(-jnp.inf)

    def extract16(s):
        vals = jnp.zeros((PEER_TOPK, tt), F32)
        ids = jnp.zeros((PEER_TOPK, tt), F32)
        for k in range(PEER_TOPK):
            m = jnp.max(s, axis=0, keepdims=True)
            am = jnp.min(jnp.where(s == m, key, float(PEER_NKEYS)), axis=0, keepdims=True)
            s = jnp.where(key == am, ninf, s)
            vals = jnp.where(row16 == k, m, vals)
            ids = jnp.where(row16 == k, am, ids)
        return vals, ids

    def head(h, carry):
        off = pl.multiple_of(h * 2 * PEER_NKEYS, 2 * PEER_NKEYS)
        v1, i1 = extract16(sc_ref[pl.ds(off, PEER_NKEYS), :])
        v2, i2 = extract16(sc_ref[pl.ds(off + PEER_NKEYS, PEER_NKEYS), :])
        e1 = i1 * float(PEER_NKEYS)
        cand = [v1[0:1] + v2]
        eid = [e1[0:1] + i2]
        pos = [row16.astype(F32)]
        for i in range(1, SUBLANES):
            keep = row8 < (PEER_TOPK // (i + 1))
            cand.append(jnp.where(keep, v1[i:i + 1] + v2[0:SUBLANES], ninf))
            eid.append(e1[i:i + 1] + i2[0:SUBLANES])
            pos.append(row8f + float(i * PEER_TOPK))
        cand.append(v1[SUBLANES:] + v2[0:1])
        eid.append(e1[SUBLANES:] + i2[0:1])
        pos.append((row8f + float(SUBLANES)) * float(PEER_TOPK))
        cand = jnp.concatenate(cand, axis=0)
        eid = jnp.concatenate(eid, axis=0)
        pos = jnp.concatenate(pos, axis=0)
        top = jnp.zeros((PEER_TOPK, tt), F32)
        idx = jnp.zeros((PEER_TOPK, tt), F32)
        for k in range(PEER_TOPK):
            m = jnp.max(cand, axis=0, keepdims=True)
            p = jnp.min(jnp.where(cand == m, pos, 1e9), axis=0, keepdims=True)
            hit = pos == p
            e = jnp.max(jnp.where(hit, eid, -1.0), axis=0, keepdims=True)
            cand = jnp.where(hit, ninf, cand)
            top = jnp.where(row16 == k, m, top)
            idx = jnp.where(row16 == k, e, idx)
        w = jnp.exp(top - top[0:1])
        out = pl.ds(pl.multiple_of(h * PEER_TOPK, PEER_TOPK), PEER_TOPK)
        idx_ref[out, :] = idx.astype(jnp.int32)
        gate_ref[out, :] = w / jnp.sum(w, axis=0, keepdims=True)
        return carry

    lax.fori_loop(0, PEER_HEADS, head, 0)


def peer_topk(scores, *, tt=128):
    rows, n = scores.shape
    idx_t, gate_t = pl.pallas_call(
        _peer_topk_kernel,
        grid=(n // tt,),
        in_specs=[pl.BlockSpec((rows, tt), lambda i: (0, i))],
        out_specs=[pl.BlockSpec((PEER_PICKS, tt), lambda i: (0, i)),
                   pl.BlockSpec((PEER_PICKS, tt), lambda i: (0, i))],
        out_shape=[jax.ShapeDtypeStruct((PEER_PICKS, n), jnp.int32),
                   jax.ShapeDtypeStruct((PEER_PICKS, n), F32)],
        compiler_params=pltpu.CompilerParams(
            dimension_semantics=("arbitrary",), vmem_limit_bytes=VMEM_LIMIT),
        name="peer_topk",
    )(scores)
    return idx_t.T, gate_t.T


def pack_expert_table(down, up):
    e, d = down.shape
    def pairs(t):
        return t.astype(BF16).reshape(e, 2, d // 2).transpose(0, 2, 1)
    both = jnp.concatenate([pairs(down), pairs(up)], axis=1)
    return lax.bitcast_convert_type(both, jnp.int32).reshape(e, d // LANES, LANES)


def _row_copy(tab_ref, buf_ref, sem_ref, expert, slot, row):
    return pltpu.make_async_copy(tab_ref.at[expert],
                                 buf_ref.at[slot, :, pl.ds(row, 1), :], sem_ref.at[slot])


APPLY_TOKENS = 8
CHUNKS = D_MODEL // LANES
STEP_TOKENS = 4 * APPLY_TOKENS


def _staged_words(st_ref, group, t):
    base = (group * APPLY_TOKENS + t) * PEER_PICKS * CHUNKS
    return [st_ref[pl.ds(base + c, PEER_PICKS, stride=CHUNKS), :] for c in range(CHUNKS)]


def _apply_groups(hn_ref, gate_ref, res_ref, o_ref, groups, between):
    tb = APPLY_TOKENS
    lane_even = (lax.broadcasted_iota(jnp.int32, (1, LANES), 1) % 2) == 0
    even2 = jnp.concatenate([lane_even, lane_even], axis=1)
    tok_of_row = lax.broadcasted_iota(jnp.int32, (2 * tb, 1), 0) % tb
    xxs, rsels, tbls = [], [], []
    for first, _ in groups:
        x = hn_ref[pl.ds(first, tb), :]
        xxs.append(jnp.concatenate([x[:, :HALF], x[:, HALF:]], axis=0).astype(BF16))
        rsels.append(jnp.zeros((2 * tb, 2 * PEER_PICKS), F32))
        tbls.append([])
    for t in range(tb):
        for g, (_, words_fn) in enumerate(groups):
            words = jnp.concatenate(words_fn(t), axis=1)
            tbl = pltpu.bitcast(words, BF16)
            tbls[g].append(tbl)
            rsels[g] = rsels[g] + jnp.where(tok_of_row == t, _dot_nt(xxs[g], tbl[:, :HALF]), 0.0)
        between(t)
    for g, (first, _) in enumerate(groups):
        tok = pl.ds(first, tb)
        coefs = []
        for c in range(2 * PEER_PICKS // LANES):
            cols = slice(c * LANES, (c + 1) * LANES)
            part = jnp.where(lane_even, rsels[g][:tb, cols], rsels[g][tb:, cols])
            hid = part + jnp.where(lane_even, pltpu.roll(part, LANES - 1, 1), pltpu.roll(part, 1, 1))
            coefs.append(gate_ref[tok, cols] * _gelu(hid))
        coef = jnp.concatenate(coefs, axis=1)
        cc = jnp.concatenate([jnp.where(even2, coef, 0.0), jnp.where(even2, 0.0, coef)],
                             axis=0).astype(BF16)
        ysel = jnp.zeros((2 * tb, HALF), F32)
        for t in range(tb):
            ysel = ysel + jnp.where(
                tok_of_row == t, jnp.dot(cc, tbls[g][t][:, HALF:], preferred_element_type=F32), 0.0)
        y = jnp.concatenate([ysel[:tb], ysel[tb:]], axis=1)
        o_ref[tok, :] = res_ref[tok, :] + y


def _peer_apply_kernel(idx0_ref, idx1_ref, hn_ref, gate_ref, res_ref, tab_ref, st_ref, o_ref,
                       buf_ref, sem_ref):
    step = pl.program_id(0)
    nsteps = pl.num_programs(0)
    tb = APPLY_TOKENS

    def issue_token(idx_ref, first_token, slot, t):
        for j in range(PEER_PICKS):
            _row_copy(tab_ref, buf_ref, sem_ref, idx_ref[first_token + t, j], slot,
                      t * PEER_PICKS + j).start(priority=j % 2)

    def wait(slot):
        pltpu.make_async_copy(buf_ref.at[slot], buf_ref.at[slot], sem_ref.at[slot]).wait()

    def copied_words(slot, t):
        return [buf_ref[slot, c, pl.ds(t * PEER_PICKS, PEER_PICKS), :] for c in range(CHUNKS)]

    compute = functools.partial(_apply_groups, hn_ref, gate_ref, res_ref, o_ref)

    @pl.when(step == 0)
    def _():
        for t in range(tb):
            issue_token(idx0_ref, 0, 0, t)

    wait(0)
    compute([(0, functools.partial(copied_words, 0)),
             (2 * tb, functools.partial(_staged_words, st_ref, 0))],
            functools.partial(issue_token, idx0_ref, tb, 1))
    wait(1)
    compute([(tb, functools.partial(copied_words, 1)),
             (3 * tb, functools.partial(_staged_words, st_ref, 1))],
            functools.partial(issue_token, idx1_ref, 0, 0))

    @pl.when(step == nsteps - 1)
    def _():
        wait(0)


def _peer_apply_staged_kernel(hn_ref, gate_ref, res_ref, st_ref, o_ref):
    tb = APPLY_TOKENS
    _apply_groups(hn_ref, gate_ref, res_ref, o_ref,
                  [(0, functools.partial(_staged_words, st_ref, 0)),
                   (tb, functools.partial(_staged_words, st_ref, 1))],
                  lambda t: None)


STAGED_BLOCK_ROWS = 2 * APPLY_TOKENS * PEER_PICKS * CHUNKS


def peer_apply(idx, hn, gate2, acc, table, staged, first_token, ntokens):
    n, d = hn.shape
    tb = STEP_TOKENS
    nsteps = ntokens // tb
    s0 = first_token // tb
    smem = pltpu.SMEM
    return pl.pallas_call(
        _peer_apply_kernel,
        grid=(nsteps,),
        in_specs=[
            pl.BlockSpec((tb, PEER_PICKS), lambda i: (s0 + i, 0), memory_space=smem),
            pl.BlockSpec((tb, PEER_PICKS), lambda i: (s0 + jnp.minimum(i + 1, nsteps - 1), 0),
                         memory_space=smem),
            pl.BlockSpec((tb, d), lambda i: (s0 + i, 0)),
            pl.BlockSpec((tb, 2 * PEER_PICKS), lambda i: (s0 + i, 0)),
            pl.BlockSpec((tb, d), lambda i: (s0 + i, 0)),
            pl.BlockSpec(memory_space=pl.ANY),
            pl.BlockSpec((STAGED_BLOCK_ROWS, LANES), lambda i: (i, 0)),
        ],
        out_specs=pl.BlockSpec((tb, d), lambda i: (s0 + i, 0)),
        out_shape=jax.ShapeDtypeStruct((n, d), F32),
        input_output_aliases={4: 0},
        scratch_shapes=[pltpu.VMEM((2, CHUNKS, APPLY_TOKENS * PEER_PICKS, LANES), jnp.int32),
                        pltpu.SemaphoreType.DMA((2,))],
        compiler_params=pltpu.CompilerParams(
            dimension_semantics=("arbitrary",), vmem_limit_bytes=VMEM_LIMIT),
        name="peer_apply",
    )(idx, idx, hn, gate2, acc, table.reshape(table.shape[0], CHUNKS, 1, LANES), staged)


def peer_apply_staged(hn, gate2, acc, staged, first_block, first_token, ntokens):
    n, d = hn.shape
    tb = 2 * APPLY_TOKENS
    s0 = first_token // tb
    return pl.pallas_call(
        _peer_apply_staged_kernel,
        grid=(ntokens // tb,),
        in_specs=[
            pl.BlockSpec((tb, d), lambda i: (s0 + i, 0)),
            pl.BlockSpec((tb, 2 * PEER_PICKS), lambda i: (s0 + i, 0)),
            pl.BlockSpec((tb, d), lambda i: (s0 + i, 0)),
            pl.BlockSpec((STAGED_BLOCK_ROWS, LANES), lambda i: (first_block + i, 0)),
        ],
        out_specs=pl.BlockSpec((tb, d), lambda i: (s0 + i, 0)),
        out_shape=jax.ShapeDtypeStruct((n, d), F32),
        input_output_aliases={2: 0},
        compiler_params=pltpu.CompilerParams(
            dimension_semantics=("arbitrary",), vmem_limit_bytes=VMEM_LIMIT),
        name="peer_apply_staged",
    )(hn, gate2, acc, staged)


SC_CORES = 2
SC_SUBCORES = 16
SC_CHUNK = 32


def sc_gather_rows(table, idx):
    b = idx.shape[0]
    nw = SC_CORES * SC_SUBCORES
    per_w = b // nw
    nchunks = per_w // SC_CHUNK
    assert per_w * nw == b and nchunks * SC_CHUNK == per_w and nchunks % 2 == 0
    mesh = plsc.VectorSubcoreMesh(core_axis_name="c", subcore_axis_name="s")

    @functools.partial(
        pl.kernel, mesh=mesh,
        out_type=jax.ShapeDtypeStruct((b,) + table.shape[1:], table.dtype),
        scratch_types=[
            pltpu.VMEM((per_w,), jnp.int32),
            pltpu.VMEM((SC_CHUNK,) + table.shape[1:], table.dtype),
            pltpu.VMEM((SC_CHUNK,) + table.shape[1:], table.dtype),
            pltpu.SemaphoreType.DMA,
            pltpu.SemaphoreType.DMA,
        ],
        name="sc_gather_rows",
    )
    def gather(tab_hbm, idx_hbm, out_hbm, idx_v, rows0, rows1, sem0, sem1):
        wid = lax.axis_index("s") * SC_CORES + lax.axis_index("c")
        base = wid * per_w
        pltpu.sync_copy(idx_hbm.at[pl.ds(base, per_w)], idx_v)

        def start(chunk, rows, sem):
            off = pl.multiple_of(chunk * SC_CHUNK, SC_CHUNK)
            pltpu.async_copy(tab_hbm.at[idx_v.at[pl.ds(off, SC_CHUNK)]], rows, sem)

        def finish(chunk, rows, sem):
            off = pl.multiple_of(chunk * SC_CHUNK, SC_CHUNK)
            pltpu.make_async_copy(tab_hbm.at[idx_v.at[pl.ds(off, SC_CHUNK)]], rows, sem).wait()
            pltpu.sync_copy(rows, out_hbm.at[pl.ds(base + off, SC_CHUNK)])

        start(0, rows0, sem0)

        @pl.loop(0, nchunks, step=2)
        def _(i):
            start(i + 1, rows1, sem1)
            finish(i, rows0, sem0)

            @pl.when(i + 2 < nchunks)
            def _():
                start(i + 2, rows0, sem0)

            finish(i + 1, rows1, sem1)

    return gather(table, idx)


PEER_GROUP = 4096
PEER_GROUP_MIXED = 3072


def peer_layer(x, norm_g, w_query, sub_keys, down, up):
    n = x.shape[0]
    hn, scores = peer_scores(x, norm_g, w_query, sub_keys)
    idx, gate = peer_topk(scores)
    gate2 = jnp.repeat(gate, 2, axis=1)
    table = pack_expert_table(down, up)
    tg = min(PEER_GROUP, n)
    tm = tg * PEER_GROUP_MIXED // PEER_GROUP
    acc = x
    for g in range(n // tg):
        t0 = g * tg
        idx_mixed = idx[t0:t0 + tm].reshape(tm // STEP_TOKENS, STEP_TOKENS, PEER_PICKS)
        idx_staged = jnp.concatenate([idx_mixed[:, STEP_TOKENS // 2:].reshape(-1),
                                      idx[t0 + tm:t0 + tg].reshape(-1)])
        staged = sc_gather_rows(table, idx_staged).reshape(-1, LANES)
        acc = peer_apply(idx, hn, gate2, acc, table, staged, t0, tm)
        acc = peer_apply_staged(hn, gate2, acc, staged, tm // STEP_TOKENS, t0 + tm, tg - tm)
    return acc


def attention_layer(x, batch, seq, norm_g, w_in, a_q_gain, a_k_gain, b_q_gain, b_k_gain,
                    lam_q1, lam_k1, lam_q2, lam_k2, b_sub_gain, w_out, lambda_init):
    proj = norm_matmul(x, norm_g, w_in)
    na = A_HEADS // 2
    tile2 = lambda g: jnp.concatenate([g, g])
    out_a = pair_attention(proj, batch, seq, "dilated", 0, na, 2 * na, na,
                           tile2(a_q_gain), tile2(a_k_gain))
    out_b = pair_attention(proj, batch, seq, "diff", 3 * na, 3 * na + B_HEADS, 3 * na + 2 * B_HEADS,
                           B_HEADS, b_q_gain.reshape(-1), b_k_gain.reshape(-1),
                           extras=(lam_q1, lam_k1, lam_q2, lam_k2, b_sub_gain),
                           lambda_init=lambda_init)
    return out_proj(x, out_a, out_b, w_out)


def kernel(x, attn_norm_g, attn_w_in, a_q_gain, a_k_gain, b_q_gain, b_k_gain, lam_q1, lam_k1,
           lam_q2, lam_k2, b_sub_gain, attn_w_out, sgu_norm_g, sgu_w_in, sgu_v_gain, sgu_w_spatial,
           sgu_b_spatial, sgu_w_out, ffn_norm_g, peer_w_query, peer_sub_keys, peer_down, peer_up):
    batch, seq, d = x.shape
    depth = ffn_norm_g.shape[0]
    h = x.reshape(batch * seq, d)
    for layer in range(depth):
        i = layer // 2
        if layer % 2 == 0:
            lambda_init = 0.8 - 0.6 * math.exp(-0.3 * layer)
            h = attention_layer(h, batch, seq, attn_norm_g[i], attn_w_in[i], a_q_gain[i],
                                a_k_gain[i], b_q_gain[i], b_k_gain[i], lam_q1[i], lam_k1[i],
                                lam_q2[i], lam_k2[i], b_sub_gain[i], attn_w_out[i], lambda_init)
        else:
            h = sgu_layer(h, sgu_norm_g[i], sgu_w_in[i], sgu_v_gain[i], sgu_w_spatial[i],
                          sgu_b_spatial[i], sgu_w_out[i])
        h = peer_layer(h, ffn_norm_g[layer], peer_w_query[layer], peer_sub_keys[layer],
                       peer_down[layer], peer_up[layer])
    return h.reshape(batch, seq, d)
```

```python
import functools
import math

import jax
import jax.numpy as jnp
from jax import lax
from jax.experimental import pallas as pl
from jax.experimental.pallas import tpu as pltpu
from jax.experimental.pallas import tpu_sc as plsc

D_MODEL = 1024
HEAD_DIM = 64
EPS = 1e-6
NEG = -1e30
A_HEADS = 8
B_HEADS = 4
C_CHUNK = 128
C_GROUPS = 8
C_WIDTH = 2 * D_MODEL
C_GROUP_DIM = C_WIDTH // C_GROUPS
PEER_HEADS = 8
PEER_NKEYS = 128
PEER_TOPK = 16
PEER_PICKS = PEER_HEADS * PEER_TOPK

LANES = 128
HALF = D_MODEL // 2
VMEM_LIMIT = 56 * 1024 * 1024

BF16 = jnp.bfloat16
F32 = jnp.float32


def _gelu(x):
    return 0.5 * x * (1.0 + jnp.tanh(math.sqrt(2.0 / math.pi) * (x + 0.044715 * (x * x * x))))


def _rms(x, g):
    return x * lax.rsqrt(jnp.mean(x * x, axis=-1, keepdims=True) + EPS) * g


def _dot_nt(a, b):
    return lax.dot_general(a, b, (((1,), (1,)), ((), ())), preferred_element_type=F32)


def _norm_matmul_kernel(x_ref, g_ref, w_ref, o_ref, xn_ref):
    @pl.when(pl.program_id(1) == 0)
    def _():
        xn_ref[...] = _rms(x_ref[...], g_ref[...]).astype(BF16)

    o_ref[...] = jnp.dot(xn_ref[...], w_ref[...], preferred_element_type=F32)


def norm_matmul(x, g, w, *, tm=512, tn=512):
    n, d = x.shape
    nout = w.shape[1]
    tm = min(tm, n)
    return pl.pallas_call(
        _norm_matmul_kernel,
        grid=(n // tm, nout // tn),
        in_specs=[
            pl.BlockSpec((tm, d), lambda i, j: (i, 0)),
            pl.BlockSpec((1, d), lambda i, j: (0, 0)),
            pl.BlockSpec((d, tn), lambda i, j: (0, j)),
        ],
        out_specs=pl.BlockSpec((tm, tn), lambda i, j: (i, j)),
        out_shape=jax.ShapeDtypeStruct((n, nout), F32),
        scratch_shapes=[pltpu.VMEM((tm, d), BF16)],
        compiler_params=pltpu.CompilerParams(
            dimension_semantics=("arbitrary", "arbitrary"), vmem_limit_bytes=VMEM_LIMIT),
        name="norm_matmul",
    )(x, g.reshape(1, d), w.astype(BF16))


def _pair_norm(t, gain, lo):
    sq = t * t
    s_lo = jnp.sum(jnp.where(lo, sq, 0.0), axis=-1, keepdims=True)
    s_hi = jnp.sum(jnp.where(lo, 0.0, sq), axis=-1, keepdims=True)
    ms = jnp.where(lo, s_lo, s_hi) * (1.0 / HEAD_DIM)
    return t * lax.rsqrt(ms + EPS) * gain


def _attn_kernel(*refs, mode, tq, tk, lambda_init):
    if mode == "dilated":
        q_ref, k_ref, v_ref, qg_ref, kg_ref, o_ref, kn_ref = refs
    else:
        (q_ref, k_ref, v_ref, qg_ref, kg_ref, lq1_ref, lk1_ref, lq2_ref, lk2_ref, sg_ref,
         o_ref, kn_ref) = refs
    i = pl.program_id(2)
    lo = lax.broadcasted_iota(jnp.int32, (1, LANES), 1) < HEAD_DIM

    @pl.when(i == 0)
    def _():
        kn_ref[...] = _pair_norm(k_ref[...], kg_ref[...], lo).astype(BF16)

    qn = _pair_norm(q_ref[...], qg_ref[...], lo) * (HEAD_DIM ** -0.5)
    qa = jnp.where(lo, qn, 0.0).astype(BF16)
    qb = jnp.where(lo, 0.0, qn).astype(BF16)
    rows = i * tq + lax.broadcasted_iota(jnp.int32, (tq, tk), 0)
    col0 = lax.broadcasted_iota(jnp.int32, (tq, tk), 1)

    def body(j, carry):
        ma, la, acca, mb, lb, accb = carry
        off = pl.multiple_of(j * tk, tk)
        kb = kn_ref[pl.ds(off, tk), :]
        vb = v_ref[pl.ds(off, tk), :].astype(BF16)
        dist = rows - (col0 + j * tk)
        if mode == "dilated":
            cnt = ((dist <= 128).astype(F32)
                   + ((dist <= 512) & ((dist & 3) == 0)).astype(F32)
                   + ((dist & 15) == 0).astype(F32))
            cnt = jnp.where(dist >= 0, cnt, 0.0)
        else:
            cnt = (dist >= 0).astype(F32)
        valid = cnt > 0.0

        def update(qh, m, l, acc):
            s = jnp.where(valid, _dot_nt(qh, kb), NEG)
            m_new = jnp.maximum(m, jnp.max(s, axis=-1, keepdims=True))
            alpha = jnp.exp(m - m_new)
            p = cnt * jnp.exp(s - m_new)
            l_new = alpha * l + jnp.sum(p, axis=-1, keepdims=True)
            acc_new = alpha * acc + jnp.dot(p.astype(BF16), vb, preferred_element_type=F32)
            return m_new, l_new, acc_new

        ma, la, acca = update(qa, ma, la, acca)
        mb, lb, accb = update(qb, mb, lb, accb)
        return ma, la, acca, mb, lb, accb

    m0 = jnp.full((tq, 1), NEG, F32)
    l0 = jnp.zeros((tq, 1), F32)
    a0 = jnp.zeros((tq, LANES), F32)
    nkv = (i * tq + tq + tk - 1) // tk
    ma, la, acca, mb, lb, accb = lax.fori_loop(0, nkv, body, (m0, l0, a0, m0, l0, a0))
    oa = acca / la
    ob = accb / lb
    if mode == "dilated":
        o_ref[...] = jnp.where(lo, oa, ob)
    else:
        lam = (jnp.exp(jnp.sum(lq1_ref[...] * lk1_ref[...], axis=-1, keepdims=True))
               - jnp.exp(jnp.sum(lq2_ref[...] * lk2_ref[...], axis=-1, keepdims=True))
               + lambda_init)
        o = oa - lam * ob
        o_ref[...] = _rms(o, sg_ref[...]) * (1.0 - lambda_init)


def pair_attention(proj, batch, seq, mode, qcol, kcol, vcol, npairs, q_gain, k_gain, extras=(),
                   lambda_init=0.0, tq=256, tk=256):
    nq = seq // tq
    small = [q_gain.reshape(1, LANES), k_gain.reshape(1, LANES)] + [e.reshape(1, -1) for e in extras]
    small_specs = [pl.BlockSpec(s.shape, lambda b, p, i: (0, 0)) for s in small]
    kernel = functools.partial(_attn_kernel, mode=mode, tq=tq, tk=tk, lambda_init=lambda_init)
    return pl.pallas_call(
        kernel,
        grid=(batch, npairs, nq),
        in_specs=[
            pl.BlockSpec((tq, LANES), lambda b, p, i: (b * nq + i, qcol + p)),
            pl.BlockSpec((seq, LANES), lambda b, p, i: (b, kcol + p)),
            pl.BlockSpec((seq, LANES), lambda b, p, i: (b, vcol + p)),
        ] + small_specs,
        out_specs=pl.BlockSpec((tq, LANES), lambda b, p, i: (b * nq + i, p)),
        out_shape=jax.ShapeDtypeStruct((batch * seq, npairs * LANES), F32),
        scratch_shapes=[pltpu.VMEM((seq, LANES), BF16)],
        compiler_params=pltpu.CompilerParams(
            dimension_semantics=("arbitrary", "arbitrary", "arbitrary"),
            vmem_limit_bytes=VMEM_LIMIT),
        name="attn_" + mode,
    )(proj, proj, proj, *small)


def _out_proj_kernel(x_ref, a_ref, b_ref, wa_ref, wb_ref, o_ref):
    o_ref[...] = (x_ref[...]
                  + jnp.dot(a_ref[...].astype(BF16), wa_ref[...], preferred_element_type=F32)
                  + jnp.dot(b_ref[...].astype(BF16), wb_ref[...], preferred_element_type=F32))


def out_proj(x, a, b, w, *, tm=512):
    n, d = x.shape
    ka, kb = a.shape[1], b.shape[1]
    w = w.astype(BF16)
    tm = min(tm, n)
    return pl.pallas_call(
        _out_proj_kernel,
        grid=(n // tm,),
        in_specs=[
            pl.BlockSpec((tm, d), lambda i: (i, 0)),
            pl.BlockSpec((tm, ka), lambda i: (i, 0)),
            pl.BlockSpec((tm, kb), lambda i: (i, 0)),
            pl.BlockSpec((ka, d), lambda i: (0, 0)),
            pl.BlockSpec((kb, d), lambda i: (0, 0)),
        ],
        out_specs=pl.BlockSpec((tm, d), lambda i: (i, 0)),
        out_shape=jax.ShapeDtypeStruct((n, d), F32),
        compiler_params=pltpu.CompilerParams(
            dimension_semantics=("arbitrary",), vmem_limit_bytes=VMEM_LIMIT),
        name="out_proj",
    )(x, a, b, w[:ka], w[ka:])


def _sgu_kernel(x_ref, g_ref, win_ref, vg_ref, ws_ref, bs_ref, wout_ref, o_ref, gated_ref, *, tm):
    x = x_ref[...]
    xn = _rms(x, g_ref[...]).astype(BF16)
    z = _gelu(jnp.dot(xn, win_ref[...], preferred_element_type=F32))
    u = z[:, :C_WIDTH]
    v = _rms(z[:, C_WIDTH:], vg_ref[...]).astype(BF16)
    r = lax.broadcasted_iota(jnp.int32, (C_CHUNK, C_CHUNK), 0)
    c = lax.broadcasted_iota(jnp.int32, (C_CHUNK, C_CHUNK), 1)
    causal = c <= r
    for grp in range(C_GROUPS):
        ws = jnp.where(causal, ws_ref[grp], 0.0).astype(BF16)
        bias = bs_ref[:, grp:grp + 1]
        cols = slice(grp * C_GROUP_DIM, (grp + 1) * C_GROUP_DIM)
        for ch in range(tm // C_CHUNK):
            rws = slice(ch * C_CHUNK, (ch + 1) * C_CHUNK)
            gate = jnp.dot(ws, v[rws, cols], preferred_element_type=F32) + bias
            gated_ref[rws, cols] = (u[rws, cols] * gate).astype(BF16)
    o_ref[...] = x + jnp.dot(gated_ref[...], wout_ref[...], preferred_element_type=F32)


def sgu_layer(x, norm_g, w_in, v_gain, w_spatial, b_spatial, w_out, *, tm=256):
    n, d = x.shape
    kernel = functools.partial(_sgu_kernel, tm=tm)
    return pl.pallas_call(
        kernel,
        grid=(n // tm,),
        in_specs=[
            pl.BlockSpec((tm, d), lambda i: (i, 0)),
            pl.BlockSpec((1, d), lambda i: (0, 0)),
            pl.BlockSpec((d, 2 * C_WIDTH), lambda i: (0, 0)),
            pl.BlockSpec((1, C_WIDTH), lambda i: (0, 0)),
            pl.BlockSpec((C_GROUPS, C_CHUNK, C_CHUNK), lambda i: (0, 0, 0)),
            pl.BlockSpec((C_CHUNK, C_GROUPS), lambda i: (0, 0)),
            pl.BlockSpec((C_WIDTH, d), lambda i: (0, 0)),
        ],
        out_specs=pl.BlockSpec((tm, d), lambda i: (i, 0)),
        out_shape=jax.ShapeDtypeStruct((n, d), F32),
        scratch_shapes=[pltpu.VMEM((tm, C_WIDTH), BF16)],
        compiler_params=pltpu.CompilerParams(
            dimension_semantics=("arbitrary",), vmem_limit_bytes=VMEM_LIMIT),
        name="sgu",
    )(x, norm_g.reshape(1, d), w_in.astype(BF16), v_gain.reshape(1, C_WIDTH), w_spatial,
      b_spatial.T, w_out.astype(BF16))


def _peer_scores_kernel(x_ref, g_ref, wq_ref, sk_ref, hn_ref, sc_ref):
    hn = _rms(x_ref[...], g_ref[...])
    hn_ref[...] = hn
    q = jnp.dot(hn.astype(BF16), wq_ref[...], preferred_element_type=F32)
    for hp in range(2 * PEER_HEADS):
        cols = slice(hp * PEER_NKEYS, (hp + 1) * PEER_NKEYS)
        sc_ref[cols, :] = _dot_nt(sk_ref[hp].astype(BF16), q[:, cols].astype(BF16))


def peer_scores(x, norm_g, w_query, sub_keys, *, tm=512):
    n, d = x.shape
    nq = w_query.shape[1]
    tm = min(tm, n)
    sk = sub_keys.reshape(2 * PEER_HEADS, PEER_NKEYS, PEER_NKEYS)
    return pl.pallas_call(
        _peer_scores_kernel,
        grid=(n // tm,),
        in_specs=[
            pl.BlockSpec((tm, d), lambda i: (i, 0)),
            pl.BlockSpec((1, d), lambda i: (0, 0)),
            pl.BlockSpec((d, nq), lambda i: (0, 0)),
            pl.BlockSpec(sk.shape, lambda i: (0, 0, 0)),
        ],
        out_specs=[pl.BlockSpec((tm, d), lambda i: (i, 0)),
                   pl.BlockSpec((nq, tm), lambda i: (0, i))],
        out_shape=[jax.ShapeDtypeStruct((n, d), F32), jax.ShapeDtypeStruct((nq, n), F32)],
        compiler_params=pltpu.CompilerParams(
            dimension_semantics=("arbitrary",), vmem_limit_bytes=VMEM_LIMIT),
        name="peer_scores",
    )(x, norm_g.reshape(1, d), w_query.astype(BF16), sk)


SUBLANES = 8


def _peer_topk_kernel(sc_ref, idx_ref, gate_ref):
    tt = sc_ref.shape[1]
    key = lax.broadcasted_iota(jnp.int32, (PEER_NKEYS, tt), 0).astype(F32)
    row16 = lax.broadcasted_iota(jnp.int32, (PEER_TOPK, tt), 0)
    row8 = lax.broadcasted_iota(jnp.int32, (SUBLANES, tt), 0)
    row8f = row8.astype(F32)
    ninf = jnp.float32(-jnp.inf)

    def extract16(s):
        vals = jnp.zeros((PEER_TOPK, tt), F32)
        ids = jnp.zeros((PEER_TOPK, tt), F32)
        for k in range(PEER_TOPK):
            m = jnp.max(s, axis=0, keepdims=True)
            am = jnp.min(jnp.where(s == m, key, float(PEER_NKEYS)), axis=0, keepdims=True)
            s = jnp.where(key == am, ninf, s)
            vals = jnp.where(row16 == k, m, vals)
            ids = jnp.where(row16 == k, am, ids)
        return vals, ids

    def head(h, carry):
        off = pl.multiple_of(h * 2 * PEER_NKEYS, 2 * PEER_NKEYS)
        v1, i1 = extract16(sc_ref[pl.ds(off, PEER_NKEYS), :])
        v2, i2 = extract16(sc_ref[pl.ds(off + PEER_NKEYS, PEER_NKEYS), :])
        e1 = i1 * float(PEER_NKEYS)
        cand = [v1[0:1] + v2]
        eid = [e1[0:1] + i2]
        pos = [row16.astype(F32)]
        for i in range(1, SUBLANES):
            keep = row8 < (PEER_TOPK // (i + 1))
            cand.append(jnp.where(keep, v1[i:i + 1] + v2[0:SUBLANES], ninf))
            eid.append(e1[i:i + 1] + i2[0:SUBLANES])
            pos.append(row8f + float(i * PEER_TOPK))
        cand.append(v1[SUBLANES:] + v2[0:1])
        eid.append(e1[SUBLANES:] + i2[0:1])
        pos.append((row8f + float(SUBLANES)) * float(PEER_TOPK))
        cand = jnp.concatenate(cand, axis=0)
        eid = jnp.concatenate(eid, axis=0)
        pos = jnp.concatenate(pos, axis=0)
        top = jnp.zeros((PEER_TOPK, tt), F32)
        idx = jnp.zeros((PEER_TOPK, tt), F32)
        for k in range(PEER_TOPK):
            m = jnp.max(cand, axis=0, keepdims=True)
            p = jnp.min(jnp.where(cand == m, pos, 1e9), axis=0, keepdims=True)
            hit = pos == p
            e = jnp.max(jnp.where(hit, eid, -1.0), axis=0, keepdims=True)
            cand = jnp.where(hit, ninf, cand)
            top = jnp.where(row16 == k, m, top)
            idx = jnp.where(row16 == k, e, idx)
        w = jnp.exp(top - top[0:1])
        out = pl.ds(pl.multiple_of(h * PEER_TOPK, PEER_TOPK), PEER_TOPK)
        idx_ref[out, :] = idx.astype(jnp.int32)
        gate_ref[out, :] = w / jnp.sum(w, axis=0, keepdims=True)
        return carry

    lax.fori_loop(0, PEER_HEADS, head, 0)


def peer_topk(scores, first_token, n, *, tt=128):
    rows = scores.shape[0]
    b0 = first_token // tt
    idx_t, gate_t = pl.pallas_call(
        _peer_topk_kernel,
        grid=(n // tt,),
        in_specs=[pl.BlockSpec((rows, tt), lambda i: (0, b0 + i))],
        out_specs=[pl.BlockSpec((PEER_PICKS, tt), lambda i: (0, i)),
                   pl.BlockSpec((PEER_PICKS, tt), lambda i: (0, i))],
        out_shape=[jax.ShapeDtypeStruct((PEER_PICKS, n), jnp.int32),
                   jax.ShapeDtypeStruct((PEER_PICKS, n), F32)],
        compiler_params=pltpu.CompilerParams(
            dimension_semantics=("arbitrary",), vmem_limit_bytes=VMEM_LIMIT),
        name="peer_topk",
    )(scores)
    return idx_t.T, gate_t.T


def pack_expert_table(down, up):
    e, d = down.shape
    def pairs(t):
        return t.astype(BF16).reshape(e, 2, d // 2).transpose(0, 2, 1)
    both = jnp.concatenate([pairs(down), pairs(up)], axis=1)
    return lax.bitcast_convert_type(both, jnp.int32).reshape(e, d // LANES, LANES)


def _row_copy(tab_ref, buf_ref, sem_ref, expert, slot, row):
    return pltpu.make_async_copy(tab_ref.at[expert],
                                 buf_ref.at[slot, :, pl.ds(row, 1), :], sem_ref.at[slot])


APPLY_TOKENS = 8
CHUNKS = D_MODEL // LANES
STEP_TOKENS = 4 * APPLY_TOKENS


def _staged_words(st_ref, group, t):
    base = (group * APPLY_TOKENS + t) * PEER_PICKS * CHUNKS
    return [st_ref[pl.ds(base + c, PEER_PICKS, stride=CHUNKS), :] for c in range(CHUNKS)]


def _apply_groups(hn_ref, gate_ref, res_ref, o_ref, groups, between):
    tb = APPLY_TOKENS
    lane_even = (lax.broadcasted_iota(jnp.int32, (1, LANES), 1) % 2) == 0
    even2 = jnp.concatenate([lane_even, lane_even], axis=1)
    tok_of_row = lax.broadcasted_iota(jnp.int32, (2 * tb, 1), 0) % tb
    xxs, rsels, tbls = [], [], []
    for first, _ in groups:
        x = hn_ref[pl.ds(first, tb), :]
        xxs.append(jnp.concatenate([x[:, :HALF], x[:, HALF:]], axis=0).astype(BF16))
        rsels.append(jnp.zeros((2 * tb, 2 * PEER_PICKS), F32))
        tbls.append([])
    for t in range(tb):
        for g, (_, words_fn) in enumerate(groups):
            words = jnp.concatenate(words_fn(t), axis=1)
            tbl = pltpu.bitcast(words, BF16)
            tbls[g].append(tbl)
            rsels[g] = rsels[g] + jnp.where(tok_of_row == t, _dot_nt(xxs[g], tbl[:, :HALF]), 0.0)
        between(t)
    for g, (first, _) in enumerate(groups):
        tok = pl.ds(first, tb)
        coefs = []
        for c in range(2 * PEER_PICKS // LANES):
            cols = slice(c * LANES, (c + 1) * LANES)
            part = jnp.where(lane_even, rsels[g][:tb, cols], rsels[g][tb:, cols])
            hid = part + jnp.where(lane_even, pltpu.roll(part, LANES - 1, 1), pltpu.roll(part, 1, 1))
            coefs.append(gate_ref[tok, cols] * _gelu(hid))
        coef = jnp.concatenate(coefs, axis=1)
        cc = jnp.concatenate([jnp.where(even2, coef, 0.0), jnp.where(even2, 0.0, coef)],
                             axis=0).astype(BF16)
        ysel = jnp.zeros((2 * tb, HALF), F32)
        for t in range(tb):
            ysel = ysel + jnp.where(
                tok_of_row == t, jnp.dot(cc, tbls[g][t][:, HALF:], preferred_element_type=F32), 0.0)
        y = jnp.concatenate([ysel[:tb], ysel[tb:]], axis=1)
        o_ref[tok, :] = res_ref[tok, :] + y


def _peer_apply_kernel(idx0_ref, idx1_ref, hn_ref, gate_ref, res_ref, tab_ref, st_ref, o_ref,
                       buf_ref, sem_ref):
    step = pl.program_id(0)
    nsteps = pl.num_programs(0)
    tb = APPLY_TOKENS

    def issue_token(idx_ref, first_token, slot, t):
        for j in range(PEER_PICKS):
            _row_copy(tab_ref, buf_ref, sem_ref, idx_ref[first_token + t, j], slot,
                      t * PEER_PICKS + j).start(priority=j % 2)

    def wait(slot):
        pltpu.make_async_copy(buf_ref.at[slot], buf_ref.at[slot], sem_ref.at[slot]).wait()

    def copied_words(slot, t):
        return [buf_ref[slot, c, pl.ds(t * PEER_PICKS, PEER_PICKS), :] for c in range(CHUNKS)]

    compute = functools.partial(_apply_groups, hn_ref, gate_ref, res_ref, o_ref)

    @pl.when(step == 0)
    def _():
        for t in range(tb):
            issue_token(idx0_ref, 0, 0, t)

    wait(0)
    compute([(0, functools.partial(copied_words, 0)),
             (2 * tb, functools.partial(_staged_words, st_ref, 0))],
            functools.partial(issue_token, idx0_ref, tb, 1))
    wait(1)
    compute([(tb, functools.partial(copied_words, 1)),
             (3 * tb, functools.partial(_staged_words, st_ref, 1))],
            functools.partial(issue_token, idx1_ref, 0, 0))

    @pl.when(step == nsteps - 1)
    def _():
        wait(0)


def _peer_apply_staged_kernel(hn_ref, gate_ref, res_ref, st_ref, o_ref):
    tb = APPLY_TOKENS
    _apply_groups(hn_ref, gate_ref, res_ref, o_ref,
                  [(0, functools.partial(_staged_words, st_ref, 0)),
                   (tb, functools.partial(_staged_words, st_ref, 1))],
                  lambda t: None)


STAGED_BLOCK_ROWS = 2 * APPLY_TOKENS * PEER_PICKS * CHUNKS


def peer_apply(idx, hn, gate2, acc, table, staged, first_token, ntokens):
    n, d = hn.shape
    tb = STEP_TOKENS
    nsteps = ntokens // tb
    s0 = first_token // tb
    smem = pltpu.SMEM
    return pl.pallas_call(
        _peer_apply_kernel,
        grid=(nsteps,),
        in_specs=[
            pl.BlockSpec((tb, PEER_PICKS), lambda i: (s0 + i, 0), memory_space=smem),
            pl.BlockSpec((tb, PEER_PICKS), lambda i: (s0 + jnp.minimum(i + 1, nsteps - 1), 0),
                         memory_space=smem),
            pl.BlockSpec((tb, d), lambda i: (s0 + i, 0)),
            pl.BlockSpec((tb, 2 * PEER_PICKS), lambda i: (s0 + i, 0)),
            pl.BlockSpec((tb, d), lambda i: (s0 + i, 0)),
            pl.BlockSpec(memory_space=pl.ANY),
            pl.BlockSpec((STAGED_BLOCK_ROWS, LANES), lambda i: (i, 0)),
        ],
        out_specs=pl.BlockSpec((tb, d), lambda i: (s0 + i, 0)),
        out_shape=jax.ShapeDtypeStruct((n, d), F32),
        input_output_aliases={4: 0},
        scratch_shapes=[pltpu.VMEM((2, CHUNKS, APPLY_TOKENS * PEER_PICKS, LANES), jnp.int32),
                        pltpu.SemaphoreType.DMA((2,))],
        compiler_params=pltpu.CompilerParams(
            dimension_semantics=("arbitrary",), vmem_limit_bytes=VMEM_LIMIT),
        name="peer_apply",
    )(idx, idx, hn, gate2, acc, table.reshape(table.shape[0], CHUNKS, 1, LANES), staged)


def peer_apply_staged(hn, gate2, acc, staged, first_block, first_token, ntokens):
    n, d = hn.shape
    tb = 2 * APPLY_TOKENS
    s0 = first_token // tb
    return pl.pallas_call(
        _peer_apply_staged_kernel,
        grid=(ntokens // tb,),
        in_specs=[
            pl.BlockSpec((tb, d), lambda i: (s0 + i, 0)),
            pl.BlockSpec((tb, 2 * PEER_PICKS), lambda i: (s0 + i, 0)),
            pl.BlockSpec((tb, d), lambda i: (s0 + i, 0)),
            pl.BlockSpec((STAGED_BLOCK_ROWS, LANES), lambda i: (first_block + i, 0)),
        ],
        out_specs=pl.BlockSpec((tb, d), lambda i: (s0 + i, 0)),
        out_shape=jax.ShapeDtypeStruct((n, d), F32),
        input_output_aliases={2: 0},
        compiler_params=pltpu.CompilerParams(
            dimension_semantics=("arbitrary",), vmem_limit_bytes=VMEM_LIMIT),
        name="peer_apply_staged",
    )(hn, gate2, acc, staged)


SC_CORES = 2
SC_SUBCORES = 16
SC_CHUNK = 32


def sc_gather_rows(table, idx):
    b = idx.shape[0]
    nw = SC_CORES * SC_SUBCORES
    per_w = b // nw
    nchunks = per_w // SC_CHUNK
    assert per_w * nw == b and nchunks * SC_CHUNK == per_w and nchunks % 2 == 0
    mesh = plsc.VectorSubcoreMesh(core_axis_name="c", subcore_axis_name="s")

    @functools.partial(
        pl.kernel, mesh=mesh,
        out_type=jax.ShapeDtypeStruct((b,) + table.shape[1:], table.dtype),
        scratch_types=[
            pltpu.VMEM((per_w,), jnp.int32),
            pltpu.VMEM((SC_CHUNK,) + table.shape[1:], table.dtype),
            pltpu.VMEM((SC_CHUNK,) + table.shape[1:], table.dtype),
            pltpu.SemaphoreType.DMA,
            pltpu.SemaphoreType.DMA,
        ],
        name="sc_gather_rows",
    )
    def gather(tab_hbm, idx_hbm, out_hbm, idx_v, rows0, rows1, sem0, sem1):
        wid = lax.axis_index("s") * SC_CORES + lax.axis_index("c")
        base = wid * per_w
        pltpu.sync_copy(idx_hbm.at[pl.ds(base, per_w)], idx_v)

        def start(chunk, rows, sem):
            off = pl.multiple_of(chunk * SC_CHUNK, SC_CHUNK)
            pltpu.async_copy(tab_hbm.at[idx_v.at[pl.ds(off, SC_CHUNK)]], rows, sem)

        def finish(chunk, rows, sem):
            off = pl.multiple_of(chunk * SC_CHUNK, SC_CHUNK)
            pltpu.make_async_copy(tab_hbm.at[idx_v.at[pl.ds(off, SC_CHUNK)]], rows, sem).wait()
            pltpu.sync_copy(rows, out_hbm.at[pl.ds(base + off, SC_CHUNK)])

        start(0, rows0, sem0)

        @pl.loop(0, nchunks, step=2)
        def _(i):
            start(i + 1, rows1, sem1)
            finish(i, rows0, sem0)

            @pl.when(i + 2 < nchunks)
            def _():
                start(i + 2, rows0, sem0)

            finish(i + 1, rows1, sem1)

    return gather(table, idx)


PEER_GROUP = 4096
PEER_GROUP_MIXED = 4096


def peer_layer(x, norm_g, w_query, sub_keys, down, up):
    n = x.shape[0]
    hn, scores = peer_scores(x, norm_g, w_query, sub_keys)
    tg = min(PEER_GROUP, n)
    tm = tg * PEER_GROUP_MIXED // PEER_GROUP
    idx0, gate0 = peer_topk(scores, 0, tg)
    if n > tg:
        idx1, gate1 = peer_topk(scores, tg, n - tg)
        idx, gate = jnp.concatenate([idx0, idx1]), jnp.concatenate([gate0, gate1])
    else:
        idx, gate = idx0, gate0
    gate2 = jnp.repeat(gate, 2, axis=1)
    table = pack_expert_table(down, up)

    def staged_indices(g):
        src, t0 = (idx0, 0) if g == 0 else (idx, g * tg)
        mixed = src[t0:t0 + tm].reshape(tm // STEP_TOKENS, STEP_TOKENS, PEER_PICKS)
        return jnp.concatenate([mixed[:, STEP_TOKENS // 2:].reshape(-1),
                                src[t0 + tm:t0 + tg].reshape(-1)])

    acc = x
    ngroups = n // tg
    pending = [staged_indices(g) for g in range(min(2, ngroups))]
    for g in range(ngroups):
        t0 = g * tg
        staged = sc_gather_rows(table, pending.pop(0)).reshape(-1, LANES)
        acc = peer_apply(idx, hn, gate2, acc, table, staged, t0, tm)
        if tg > tm:
            acc = peer_apply_staged(hn, gate2, acc, staged, tm // STEP_TOKENS, t0 + tm, tg - tm)
        if g + 2 < ngroups:
            nxt, acc = lax.optimization_barrier((staged_indices(g + 2), acc))
            pending.append(nxt)
    return acc


def attention_layer(x, batch, seq, norm_g, w_in, a_q_gain, a_k_gain, b_q_gain, b_k_gain,
                    lam_q1, lam_k1, lam_q2, lam_k2, b_sub_gain, w_out, lambda_init):
    proj = norm_matmul(x, norm_g, w_in)
    na = A_HEADS // 2
    tile2 = lambda g: jnp.concatenate([g, g])
    out_a = pair_attention(proj, batch, seq, "dilated", 0, na, 2 * na, na,
                           tile2(a_q_gain), tile2(a_k_gain))
    out_b = pair_attention(proj, batch, seq, "diff", 3 * na, 3 * na + B_HEADS, 3 * na + 2 * B_HEADS,
                           B_HEADS, b_q_gain.reshape(-1), b_k_gain.reshape(-1),
                           extras=(lam_q1, lam_k1, lam_q2, lam_k2, b_sub_gain),
                           lambda_init=lambda_init)
    return out_proj(x, out_a, out_b, w_out)


def kernel(x, attn_norm_g, attn_w_in, a_q_gain, a_k_gain, b_q_gain, b_k_gain, lam_q1, lam_k1,
           lam_q2, lam_k2, b_sub_gain, attn_w_out, sgu_norm_g, sgu_w_in, sgu_v_gain, sgu_w_spatial,
           sgu_b_spatial, sgu_w_out, ffn_norm_g, peer_w_query, peer_sub_keys, peer_down, peer_up):
    batch, seq, d = x.shape
    depth = ffn_norm_g.shape[0]
    h = x.reshape(batch * seq, d)
    for layer in range(depth):
        i = layer // 2
        if layer % 2 == 0:
            lambda_init = 0.8 - 0.6 * math.exp(-0.3 * layer)
            h = attention_layer(h, batch, seq, attn_norm_g[i], attn_w_in[i], a_q_gain[i],
                                a_k_gain[i], b_q_gain[i], b_k_gain[i], lam_q1[i], lam_k1[i],
                                lam_q2[i], lam_k2[i], b_sub_gain[i], attn_w_out[i], lambda_init)
        else:
            h = sgu_layer(h, sgu_norm_g[i], sgu_w_in[i], sgu_v_gain[i], sgu_w_spatial[i],
                          sgu_b_spatial[i], sgu_w_out[i])
        h = peer_layer(h, ffn_norm_g[layer], peer_w_query[layer], peer_sub_keys[layer],
                       peer_down[layer], peer_up[layer])
    return h.reshape(batch, seq, d)
```

```python
import functools
import math

import jax
import jax.numpy as jnp
from jax import lax
from jax.experimental import pallas as pl
from jax.experimental.pallas import tpu as pltpu
from jax.experimental.pallas import tpu_sc as plsc

D_MODEL = 1024
HEAD_DIM = 64
EPS = 1e-6
NEG = -1e30
A_HEADS = 8
B_HEADS = 4
C_CHUNK = 128
C_GROUPS = 8
C_WIDTH = 2 * D_MODEL
C_GROUP_DIM = C_WIDTH // C_GROUPS
PEER_HEADS = 8
PEER_NKEYS = 128
PEER_TOPK = 16
PEER_PICKS = PEER_HEADS * PEER_TOPK

LANES = 128
HALF = D_MODEL // 2
VMEM_LIMIT = 56 * 1024 * 1024

BF16 = jnp.bfloat16
F32 = jnp.float32


def _gelu(x):
    return 0.5 * x * (1.0 + jnp.tanh(math.sqrt(2.0 / math.pi) * (x + 0.044715 * (x * x * x))))


def _rms(x, g):
    return x * lax.rsqrt(jnp.mean(x * x, axis=-1, keepdims=True) + EPS) * g


def _dot_nt(a, b):
    return lax.dot_general(a, b, (((1,), (1,)), ((), ())), preferred_element_type=F32)


def _norm_matmul_kernel(x_ref, g_ref, w_ref, o_ref, xn_ref):
    @pl.when(pl.program_id(1) == 0)
    def _():
        xn_ref[...] = _rms(x_ref[...], g_ref[...]).astype(BF16)

    o_ref[...] = jnp.dot(xn_ref[...], w_ref[...], preferred_element_type=F32)


def norm_matmul(x, g, w, *, tm=512, tn=512):
    n, d = x.shape
    nout = w.shape[1]
    tm = min(tm, n)
    return pl.pallas_call(
        _norm_matmul_kernel,
        grid=(n // tm, nout // tn),
        in_specs=[
            pl.BlockSpec((tm, d), lambda i, j: (i, 0)),
            pl.BlockSpec((1, d), lambda i, j: (0, 0)),
            pl.BlockSpec((d, tn), lambda i, j: (0, j)),
        ],
        out_specs=pl.BlockSpec((tm, tn), lambda i, j: (i, j)),
        out_shape=jax.ShapeDtypeStruct((n, nout), F32),
        scratch_shapes=[pltpu.VMEM((tm, d), BF16)],
        compiler_params=pltpu.CompilerParams(
            dimension_semantics=("arbitrary", "arbitrary"), vmem_limit_bytes=VMEM_LIMIT),
        name="norm_matmul",
    )(x, g.reshape(1, d), w.astype(BF16))


def _pair_norm(t, gain, lo):
    sq = t * t
    s_lo = jnp.sum(jnp.where(lo, sq, 0.0), axis=-1, keepdims=True)
    s_hi = jnp.sum(jnp.where(lo, 0.0, sq), axis=-1, keepdims=True)
    ms = jnp.where(lo, s_lo, s_hi) * (1.0 / HEAD_DIM)
    return t * lax.rsqrt(ms + EPS) * gain


def _attn_kernel(*refs, mode, tq, tk, lambda_init):
    if mode == "dilated":
        q_ref, k_ref, v_ref, qg_ref, kg_ref, o_ref, kn_ref = refs
    else:
        (q_ref, k_ref, v_ref, qg_ref, kg_ref, lq1_ref, lk1_ref, lq2_ref, lk2_ref, sg_ref,
         o_ref, kn_ref) = refs
    i = pl.program_id(2)
    lo = lax.broadcasted_iota(jnp.int32, (1, LANES), 1) < HEAD_DIM

    @pl.when(i == 0)
    def _():
        kn_ref[...] = _pair_norm(k_ref[...], kg_ref[...], lo).astype(BF16)

    qn = _pair_norm(q_ref[...], qg_ref[...], lo) * (HEAD_DIM ** -0.5)
    qa = jnp.where(lo, qn, 0.0).astype(BF16)
    qb = jnp.where(lo, 0.0, qn).astype(BF16)
    rows = i * tq + lax.broadcasted_iota(jnp.int32, (tq, tk), 0)
    col0 = lax.broadcasted_iota(jnp.int32, (tq, tk), 1)

    def body(j, carry):
        ma, la, acca, mb, lb, accb = carry
        off = pl.multiple_of(j * tk, tk)
        kb = kn_ref[pl.ds(off, tk), :]
        vb = v_ref[pl.ds(off, tk), :].astype(BF16)
        dist = rows - (col0 + j * tk)
        if mode == "dilated":
            cnt = ((dist <= 128).astype(F32)
                   + ((dist <= 512) & ((dist & 3) == 0)).astype(F32)
                   + ((dist & 15) == 0).astype(F32))
            cnt = jnp.where(dist >= 0, cnt, 0.0)
        else:
            cnt = (dist >= 0).astype(F32)
        valid = cnt > 0.0

        def update(qh, m, l, acc):
            s = jnp.where(valid, _dot_nt(qh, kb), NEG)
            m_new = jnp.maximum(m, jnp.max(s, axis=-1, keepdims=True))
            alpha = jnp.exp(m - m_new)
            p = cnt * jnp.exp(s - m_new)
            l_new = alpha * l + jnp.sum(p, axis=-1, keepdims=True)
            acc_new = alpha * acc + jnp.dot(p.astype(BF16), vb, preferred_element_type=F32)
            return m_new, l_new, acc_new

        ma, la, acca = update(qa, ma, la, acca)
        mb, lb, accb = update(qb, mb, lb, accb)
        return ma, la, acca, mb, lb, accb

    m0 = jnp.full((tq, 1), NEG, F32)
    l0 = jnp.zeros((tq, 1), F32)
    a0 = jnp.zeros((tq, LANES), F32)
    nkv = (i * tq + tq + tk - 1) // tk
    ma, la, acca, mb, lb, accb = lax.fori_loop(0, nkv, body, (m0, l0, a0, m0, l0, a0))
    oa = acca / la
    ob = accb / lb
    if mode == "dilated":
        o_ref[...] = jnp.where(lo, oa, ob)
    else:
        lam = (jnp.exp(jnp.sum(lq1_ref[...] * lk1_ref[...], axis=-1, keepdims=True))
               - jnp.exp(jnp.sum(lq2_ref[...] * lk2_ref[...], axis=-1, keepdims=True))
               + lambda_init)
        o = oa - lam * ob
        o_ref[...] = _rms(o, sg_ref[...]) * (1.0 - lambda_init)


def pair_attention(proj, batch, seq, mode, qcol, kcol, vcol, npairs, q_gain, k_gain, extras=(),
                   lambda_init=0.0, tq=256, tk=256):
    nq = seq // tq
    small = [q_gain.reshape(1, LANES), k_gain.reshape(1, LANES)] + [e.reshape(1, -1) for e in extras]
    small_specs = [pl.BlockSpec(s.shape, lambda b, p, i: (0, 0)) for s in small]
    kernel = functools.partial(_attn_kernel, mode=mode, tq=tq, tk=tk, lambda_init=lambda_init)
    return pl.pallas_call(
        kernel,
        grid=(batch, npairs, nq),
        in_specs=[
            pl.BlockSpec((tq, LANES), lambda b, p, i: (b * nq + i, qcol + p)),
            pl.BlockSpec((seq, LANES), lambda b, p, i: (b, kcol + p)),
            pl.BlockSpec((seq, LANES), lambda b, p, i: (b, vcol + p)),
        ] + small_specs,
        out_specs=pl.BlockSpec((tq, LANES), lambda b, p, i: (b * nq + i, p)),
        out_shape=jax.ShapeDtypeStruct((batch * seq, npairs * LANES), F32),
        scratch_shapes=[pltpu.VMEM((seq, LANES), BF16)],
        compiler_params=pltpu.CompilerParams(
            dimension_semantics=("arbitrary", "arbitrary", "arbitrary"),
            vmem_limit_bytes=VMEM_LIMIT),
        name="attn_" + mode,
    )(proj, proj, proj, *small)


def _out_proj_kernel(x_ref, a_ref, b_ref, wa_ref, wb_ref, o_ref):
    o_ref[...] = (x_ref[...]
                  + jnp.dot(a_ref[...].astype(BF16), wa_ref[...], preferred_element_type=F32)
                  + jnp.dot(b_ref[...].astype(BF16), wb_ref[...], preferred_element_type=F32))


def out_proj(x, a, b, w, *, tm=512):
    n, d = x.shape
    ka, kb = a.shape[1], b.shape[1]
    w = w.astype(BF16)
    tm = min(tm, n)
    return pl.pallas_call(
        _out_proj_kernel,
        grid=(n // tm,),
        in_specs=[
            pl.BlockSpec((tm, d), lambda i: (i, 0)),
            pl.BlockSpec((tm, ka), lambda i: (i, 0)),
            pl.BlockSpec((tm, kb), lambda i: (i, 0)),
            pl.BlockSpec((ka, d), lambda i: (0, 0)),
            pl.BlockSpec((kb, d), lambda i: (0, 0)),
        ],
        out_specs=pl.BlockSpec((tm, d), lambda i: (i, 0)),
        out_shape=jax.ShapeDtypeStruct((n, d), F32),
        compiler_params=pltpu.CompilerParams(
            dimension_semantics=("arbitrary",), vmem_limit_bytes=VMEM_LIMIT),
        name="out_proj",
    )(x, a, b, w[:ka], w[ka:])


def _sgu_kernel(x_ref, g_ref, win_ref, vg_ref, ws_ref, bs_ref, wout_ref, o_ref, gated_ref, *, tm):
    x = x_ref[...]
    xn = _rms(x, g_ref[...]).astype(BF16)
    z = _gelu(jnp.dot(xn, win_ref[...], preferred_element_type=F32))
    u = z[:, :C_WIDTH]
    v = _rms(z[:, C_WIDTH:], vg_ref[...]).astype(BF16)
    r = lax.broadcasted_iota(jnp.int32, (C_CHUNK, C_CHUNK), 0)
    c = lax.broadcasted_iota(jnp.int32, (C_CHUNK, C_CHUNK), 1)
    causal = c <= r
    for grp in range(C_GROUPS):
        ws = jnp.where(causal, ws_ref[grp], 0.0).astype(BF16)
        bias = bs_ref[:, grp:grp + 1]
        cols = slice(grp * C_GROUP_DIM, (grp + 1) * C_GROUP_DIM)
        for ch in range(tm // C_CHUNK):
            rws = slice(ch * C_CHUNK, (ch + 1) * C_CHUNK)
            gate = jnp.dot(ws, v[rws, cols], preferred_element_type=F32) + bias
            gated_ref[rws, cols] = (u[rws, cols] * gate).astype(BF16)
    o_ref[...] = x + jnp.dot(gated_ref[...], wout_ref[...], preferred_element_type=F32)


def sgu_layer(x, norm_g, w_in, v_gain, w_spatial, b_spatial, w_out, *, tm=256):
    n, d = x.shape
    kernel = functools.partial(_sgu_kernel, tm=tm)
    return pl.pallas_call(
        kernel,
        grid=(n // tm,),
        in_specs=[
            pl.BlockSpec((tm, d), lambda i: (i, 0)),
            pl.BlockSpec((1, d), lambda i: (0, 0)),
            pl.BlockSpec((d, 2 * C_WIDTH), lambda i: (0, 0)),
            pl.BlockSpec((1, C_WIDTH), lambda i: (0, 0)),
            pl.BlockSpec((C_GROUPS, C_CHUNK, C_CHUNK), lambda i: (0, 0, 0)),
            pl.BlockSpec((C_CHUNK, C_GROUPS), lambda i: (0, 0)),
            pl.BlockSpec((C_WIDTH, d), lambda i: (0, 0)),
        ],
        out_specs=pl.BlockSpec((tm, d), lambda i: (i, 0)),
        out_shape=jax.ShapeDtypeStruct((n, d), F32),
        scratch_shapes=[pltpu.VMEM((tm, C_WIDTH), BF16)],
        compiler_params=pltpu.CompilerParams(
            dimension_semantics=("arbitrary",), vmem_limit_bytes=VMEM_LIMIT),
        name="sgu",
    )(x, norm_g.reshape(1, d), w_in.astype(BF16), v_gain.reshape(1, C_WIDTH), w_spatial,
      b_spatial.T, w_out.astype(BF16))


def _peer_scores_kernel(x_ref, g_ref, wq_ref, sk_ref, hn_ref, sc_ref):
    hn = _rms(x_ref[...], g_ref[...])
    hn_ref[...] = hn
    q = jnp.dot(hn.astype(BF16), wq_ref[...], preferred_element_type=F32)
    for hp in range(2 * PEER_HEADS):
        cols = slice(hp * PEER_NKEYS, (hp + 1) * PEER_NKEYS)
        sc_ref[cols, :] = _dot_nt(sk_ref[hp].astype(BF16), q[:, cols].astype(BF16))


def peer_scores(x, norm_g, w_query, sub_keys, *, tm=512):
    n, d = x.shape
    nq = w_query.shape[1]
    tm = min(tm, n)
    sk = sub_keys.reshape(2 * PEER_HEADS, PEER_NKEYS, PEER_NKEYS)
    return pl.pallas_call(
        _peer_scores_kernel,
        grid=(n // tm,),
        in_specs=[
            pl.BlockSpec((tm, d), lambda i: (i, 0)),
            pl.BlockSpec((1, d), lambda i: (0, 0)),
            pl.BlockSpec((d, nq), lambda i: (0, 0)),
            pl.BlockSpec(sk.shape, lambda i: (0, 0, 0)),
        ],
        out_specs=[pl.BlockSpec((tm, d), lambda i: (i, 0)),
                   pl.BlockSpec((nq, tm), lambda i: (0, i))],
        out_shape=[jax.ShapeDtypeStruct((n, d), F32), jax.ShapeDtypeStruct((nq, n), F32)],
        compiler_params=pltpu.CompilerParams(
            dimension_semantics=("arbitrary",), vmem_limit_bytes=VMEM_LIMIT),
        name="peer_scores",
    )(x, norm_g.reshape(1, d), w_query.astype(BF16), sk)


SUBLANES = 8


def _peer_topk_kernel(sc_ref, idx_ref, gate_ref):
    tt = sc_ref.shape[1]
    key = lax.broadcasted_iota(jnp.int32, (PEER_NKEYS, tt), 0).astype(F32)
    row16 = lax.broadcasted_iota(jnp.int32, (PEER_TOPK, tt), 0)
    row8 = lax.broadcasted_iota(jnp.int32, (SUBLANES, tt), 0)
    row8f = row8.astype(F32)
    ninf = jnp.float32(-jnp.inf)

    def extract16(s):
        vals = jnp.zeros((PEER_TOPK, tt), F32)
        ids = jnp.zeros((PEER_TOPK, tt), F32)
        for k in range(PEER_TOPK):
            m = jnp.max(s, axis=0, keepdims=True)
            am = jnp.min(jnp.where(s == m, key, float(PEER_NKEYS)), axis=0, keepdims=True)
            s = jnp.where(key == am, ninf, s)
            vals = jnp.where(row16 == k, m, vals)
            ids = jnp.where(row16 == k, am, ids)
        return vals, ids

    def head(h, carry):
        off = pl.multiple_of(h * 2 * PEER_NKEYS, 2 * PEER_NKEYS)
        v1, i1 = extract16(sc_ref[pl.ds(off, PEER_NKEYS), :])
        v2, i2 = extract16(sc_ref[pl.ds(off + PEER_NKEYS, PEER_NKEYS), :])
        e1 = i1 * float(PEER_NKEYS)
        cand = [v1[0:1] + v2]
        eid = [e1[0:1] + i2]
        pos = [row16.astype(F32)]
        for i in range(1, SUBLANES):
            keep = row8 < (PEER_TOPK // (i + 1))
            cand.append(jnp.where(keep, v1[i:i + 1] + v2[0:SUBLANES], ninf))
            eid.append(e1[i:i + 1] + i2[0:SUBLANES])
            pos.append(row8f + float(i * PEER_TOPK))
        cand.append(v1[SUBLANES:] + v2[0:1])
        eid.append(e1[SUBLANES:] + i2[0:1])
        pos.append((row8f + float(SUBLANES)) * float(PEER_TOPK))
        cand = jnp.concatenate(cand, axis=0)
        eid = jnp.concatenate(eid, axis=0)
        pos = jnp.concatenate(pos, axis=0)
        top = jnp.zeros((PEER_TOPK, tt), F32)
        idx = jnp.zeros((PEER_TOPK, tt), F32)
        for k in range(PEER_TOPK):
            m = jnp.max(cand, axis=0, keepdims=True)
            p = jnp.min(jnp.where(cand == m, pos, 1e9), axis=0, keepdims=True)
            hit = pos == p
            e = jnp.max(jnp.where(hit, eid, -1.0), axis=0, keepdims=True)
            cand = jnp.where(hit, ninf, cand)
            top = jnp.where(row16 == k, m, top)
            idx = jnp.where(row16 == k, e, idx)
        w = jnp.exp(top - top[0:1])
        out = pl.ds(pl.multiple_of(h * PEER_TOPK, PEER_TOPK), PEER_TOPK)
        idx_ref[out, :] = idx.astype(jnp.int32)
        gate_ref[out, :] = w / jnp.sum(w, axis=0, keepdims=True)
        return carry

    lax.fori_loop(0, PEER_HEADS, head, 0)


def peer_topk(scores, first_token, n, *, tt=128):
    rows = scores.shape[0]
    b0 = first_token // tt
    idx_t, gate_t = pl.pallas_call(
        _peer_topk_kernel,
        grid=(n // tt,),
        in_specs=[pl.BlockSpec((rows, tt), lambda i: (0, b0 + i))],
        out_specs=[pl.BlockSpec((PEER_PICKS, tt), lambda i: (0, i)),
                   pl.BlockSpec((PEER_PICKS, tt), lambda i: (0, i))],
        out_shape=[jax.ShapeDtypeStruct((PEER_PICKS, n), jnp.int32),
                   jax.ShapeDtypeStruct((PEER_PICKS, n), F32)],
        compiler_params=pltpu.CompilerParams(
            dimension_semantics=("arbitrary",), vmem_limit_bytes=VMEM_LIMIT),
        name="peer_topk",
    )(scores)
    return idx_t.T, gate_t.T


def pack_expert_table(down, up):
    e, d = down.shape
    rows = min(PACK_ROWS, e)
    packed = pl.pallas_call(
        functools.partial(_pack_kernel, rows=rows),
        grid=(e // rows,),
        in_specs=[pl.BlockSpec((rows, d), lambda i: (i, 0)),
                  pl.BlockSpec((rows, d), lambda i: (i, 0))],
        out_specs=pl.BlockSpec(memory_space=pl.ANY),
        out_shape=jax.ShapeDtypeStruct((e, d // LANES, 1, LANES), jnp.int32),
        scratch_shapes=[pltpu.VMEM((d // LANES, rows, LANES), jnp.int32),
                        pltpu.SemaphoreType.DMA(())],
        compiler_params=pltpu.CompilerParams(
            dimension_semantics=("arbitrary",), vmem_limit_bytes=VMEM_LIMIT),
        name="pack_experts",
    )(down, up)
    return packed.reshape(e, d // LANES, LANES)


PACK_ROWS = 512


def _pack_kernel(down_ref, up_ref, out_ref, stage_ref, sem_ref, *, rows):
    step = pl.program_id(0)
    quarter = HALF // LANES

    def words(t):
        lo = pltpu.bitcast(t[:, :HALF].astype(BF16).astype(F32), jnp.uint32)
        hi = pltpu.bitcast(t[:, HALF:].astype(BF16).astype(F32), jnp.uint32)
        return pltpu.bitcast((hi & jnp.uint32(0xFFFF0000)) | (lo >> 16), jnp.int32)

    for k, ref in enumerate((down_ref, up_ref)):
        w = words(ref[...])
        for c in range(quarter):
            stage_ref[k * quarter + c] = w[:, c * LANES:(c + 1) * LANES]
    for r in range(rows):
        pltpu.make_async_copy(stage_ref.at[:, pl.ds(r, 1), :], out_ref.at[step * rows + r],
                              sem_ref).start(priority=r % 2)
    pltpu.make_async_copy(stage_ref, stage_ref, sem_ref).wait()


def _row_copy(tab_ref, buf_ref, sem_ref, expert, slot, row):
    return pltpu.make_async_copy(tab_ref.at[expert],
                                 buf_ref.at[slot, :, pl.ds(row, 1), :], sem_ref.at[slot])


APPLY_TOKENS = 8
CHUNKS = D_MODEL // LANES
STEP_TOKENS = 4 * APPLY_TOKENS


def _staged_words(st_ref, group, t):
    base = (group * APPLY_TOKENS + t) * PEER_PICKS * CHUNKS
    return [st_ref[pl.ds(base + c, PEER_PICKS, stride=CHUNKS), :] for c in range(CHUNKS)]


def _apply_groups(hn_ref, gate_ref, res_ref, o_ref, groups, between):
    tb = APPLY_TOKENS
    lane_even = (lax.broadcasted_iota(jnp.int32, (1, LANES), 1) % 2) == 0
    even2 = jnp.concatenate([lane_even, lane_even], axis=1)
    tok_of_row = lax.broadcasted_iota(jnp.int32, (2 * tb, 1), 0) % tb
    xxs, rsels, tbls = [], [], []
    for first, _ in groups:
        x = hn_ref[pl.ds(first, tb), :]
        xxs.append(jnp.concatenate([x[:, :HALF], x[:, HALF:]], axis=0).astype(BF16))
        rsels.append(jnp.zeros((2 * tb, 2 * PEER_PICKS), F32))
        tbls.append([])
    for t in range(tb):
        for g, (_, words_fn) in enumerate(groups):
            words = jnp.concatenate(words_fn(t), axis=1)
            tbl = pltpu.bitcast(words, BF16)
            tbls[g].append(tbl)
            rsels[g] = rsels[g] + jnp.where(tok_of_row == t, _dot_nt(xxs[g], tbl[:, :HALF]), 0.0)
        between(t)
    for g, (first, _) in enumerate(groups):
        tok = pl.ds(first, tb)
        coefs = []
        for c in range(2 * PEER_PICKS // LANES):
            cols = slice(c * LANES, (c + 1) * LANES)
            part = jnp.where(lane_even, rsels[g][:tb, cols], rsels[g][tb:, cols])
            hid = part + jnp.where(lane_even, pltpu.roll(part, LANES - 1, 1), pltpu.roll(part, 1, 1))
            coefs.append(gate_ref[tok, cols] * _gelu(hid))
        coef = jnp.concatenate(coefs, axis=1)
        cc = jnp.concatenate([jnp.where(even2, coef, 0.0), jnp.where(even2, 0.0, coef)],
                             axis=0).astype(BF16)
        ysel = jnp.zeros((2 * tb, HALF), F32)
        for t in range(tb):
            ysel = ysel + jnp.where(
                tok_of_row == t, jnp.dot(cc, tbls[g][t][:, HALF:], preferred_element_type=F32), 0.0)
        y = jnp.concatenate([ysel[:tb], ysel[tb:]], axis=1)
        o_ref[tok, :] = res_ref[tok, :] + y


def _peer_apply_kernel(idx0_ref, idx1_ref, hn_ref, gate_ref, res_ref, tab_ref, st_ref, o_ref,
                       buf_ref, sem_ref):
    step = pl.program_id(0)
    nsteps = pl.num_programs(0)
    tb = APPLY_TOKENS

    def issue_token(idx_ref, first_token, slot, t):
        for j in range(PEER_PICKS):
            _row_copy(tab_ref, buf_ref, sem_ref, idx_ref[first_token + t, j], slot,
                      t * PEER_PICKS + j).start(priority=1)

    def wait(slot):
        pltpu.make_async_copy(buf_ref.at[slot], buf_ref.at[slot], sem_ref.at[slot]).wait()

    def copied_words(slot, t):
        return [buf_ref[slot, c, pl.ds(t * PEER_PICKS, PEER_PICKS), :] for c in range(CHUNKS)]

    compute = functools.partial(_apply_groups, hn_ref, gate_ref, res_ref, o_ref)

    @pl.when(step == 0)
    def _():
        for t in range(tb):
            issue_token(idx0_ref, 0, 0, t)

    wait(0)
    compute([(0, functools.partial(copied_words, 0)),
             (2 * tb, functools.partial(_staged_words, st_ref, 0))],
            functools.partial(issue_token, idx0_ref, tb, 1))
    wait(1)
    compute([(tb, functools.partial(copied_words, 1)),
             (3 * tb, functools.partial(_staged_words, st_ref, 1))],
            functools.partial(issue_token, idx1_ref, 0, 0))

    @pl.when(step == nsteps - 1)
    def _():
        wait(0)


def _peer_apply_staged_kernel(hn_ref, gate_ref, res_ref, st_ref, o_ref):
    tb = APPLY_TOKENS
    _apply_groups(hn_ref, gate_ref, res_ref, o_ref,
                  [(0, functools.partial(_staged_words, st_ref, 0)),
                   (tb, functools.partial(_staged_words, st_ref, 1))],
                  lambda t: None)


STAGED_BLOCK_ROWS = 2 * APPLY_TOKENS * PEER_PICKS * CHUNKS


def peer_apply(idx, hn, gate2, acc, table, staged, first_token, ntokens):
    n, d = hn.shape
    tb = STEP_TOKENS
    nsteps = ntokens // tb
    s0 = first_token // tb
    smem = pltpu.SMEM
    return pl.pallas_call(
        _peer_apply_kernel,
        grid=(nsteps,),
        in_specs=[
            pl.BlockSpec((tb, PEER_PICKS), lambda i: (s0 + i, 0), memory_space=smem),
            pl.BlockSpec((tb, PEER_PICKS), lambda i: (s0 + jnp.minimum(i + 1, nsteps - 1), 0),
                         memory_space=smem),
            pl.BlockSpec((tb, d), lambda i: (s0 + i, 0)),
            pl.BlockSpec((tb, 2 * PEER_PICKS), lambda i: (s0 + i, 0)),
            pl.BlockSpec((tb, d), lambda i: (s0 + i, 0)),
            pl.BlockSpec(memory_space=pl.ANY),
            pl.BlockSpec((STAGED_BLOCK_ROWS, LANES), lambda i: (i, 0)),
        ],
        out_specs=pl.BlockSpec((tb, d), lambda i: (s0 + i, 0)),
        out_shape=jax.ShapeDtypeStruct((n, d), F32),
        input_output_aliases={4: 0},
        scratch_shapes=[pltpu.VMEM((2, CHUNKS, APPLY_TOKENS * PEER_PICKS, LANES), jnp.int32),
                        pltpu.SemaphoreType.DMA((2,))],
        compiler_params=pltpu.CompilerParams(
            dimension_semantics=("arbitrary",), vmem_limit_bytes=VMEM_LIMIT),
        name="peer_apply",
    )(idx, idx, hn, gate2, acc, table.reshape(table.shape[0], CHUNKS, 1, LANES), staged)


def peer_apply_staged(hn, gate2, acc, staged, first_block, first_token, ntokens):
    n, d = hn.shape
    tb = 2 * APPLY_TOKENS
    s0 = first_token // tb
    return pl.pallas_call(
        _peer_apply_staged_kernel,
        grid=(ntokens // tb,),
        in_specs=[
            pl.BlockSpec((tb, d), lambda i: (s0 + i, 0)),
            pl.BlockSpec((tb, 2 * PEER_PICKS), lambda i: (s0 + i, 0)),
            pl.BlockSpec((tb, d), lambda i: (s0 + i, 0)),
            pl.BlockSpec((STAGED_BLOCK_ROWS, LANES), lambda i: (first_block + i, 0)),
        ],
        out_specs=pl.BlockSpec((tb, d), lambda i: (s0 + i, 0)),
        out_shape=jax.ShapeDtypeStruct((n, d), F32),
        input_output_aliases={2: 0},
        compiler_params=pltpu.CompilerParams(
            dimension_semantics=("arbitrary",), vmem_limit_bytes=VMEM_LIMIT),
        name="peer_apply_staged",
    )(hn, gate2, acc, staged)


SC_CORES = 2
SC_SUBCORES = 16
SC_CHUNK = 32


def sc_gather_rows(table, idx):
    b = idx.shape[0]
    nw = SC_CORES * SC_SUBCORES
    per_w = b // nw
    nchunks = per_w // SC_CHUNK
    assert per_w * nw == b and nchunks * SC_CHUNK == per_w and nchunks % 2 == 0
    mesh = plsc.VectorSubcoreMesh(core_axis_name="c", subcore_axis_name="s")

    @functools.partial(
        pl.kernel, mesh=mesh,
        out_type=jax.ShapeDtypeStruct((b,) + table.shape[1:], table.dtype),
        scratch_types=[
            pltpu.VMEM((per_w,), jnp.int32),
            pltpu.VMEM((SC_CHUNK,) + table.shape[1:], table.dtype),
            pltpu.VMEM((SC_CHUNK,) + table.shape[1:], table.dtype),
            pltpu.SemaphoreType.DMA,
            pltpu.SemaphoreType.DMA,
        ],
        name="sc_gather_rows",
    )
    def gather(tab_hbm, idx_hbm, out_hbm, idx_v, rows0, rows1, sem0, sem1):
        wid = lax.axis_index("s") * SC_CORES + lax.axis_index("c")
        base = wid * per_w
        pltpu.sync_copy(idx_hbm.at[pl.ds(base, per_w)], idx_v)

        def start(chunk, rows, sem):
            off = pl.multiple_of(chunk * SC_CHUNK, SC_CHUNK)
            pltpu.async_copy(tab_hbm.at[idx_v.at[pl.ds(off, SC_CHUNK)]], rows, sem)

        def finish(chunk, rows, sem):
            off = pl.multiple_of(chunk * SC_CHUNK, SC_CHUNK)
            pltpu.make_async_copy(tab_hbm.at[idx_v.at[pl.ds(off, SC_CHUNK)]], rows, sem).wait()
            pltpu.sync_copy(rows, out_hbm.at[pl.ds(base + off, SC_CHUNK)])

        start(0, rows0, sem0)

        @pl.loop(0, nchunks, step=2)
        def _(i):
            start(i + 1, rows1, sem1)
            finish(i, rows0, sem0)

            @pl.when(i + 2 < nchunks)
            def _():
                start(i + 2, rows0, sem0)

            finish(i + 1, rows1, sem1)

    return gather(table, idx)


PEER_GROUP = 4096
PEER_GROUP_MIXED = 4096


def peer_layer(x, norm_g, w_query, sub_keys, down, up):
    n = x.shape[0]
    hn, scores = peer_scores(x, norm_g, w_query, sub_keys)
    tg = min(PEER_GROUP, n)
    tm = tg * PEER_GROUP_MIXED // PEER_GROUP
    idx0, gate0 = peer_topk(scores, 0, tg)
    if n > tg:
        idx1, gate1 = peer_topk(scores, tg, n - tg)
        idx, gate = jnp.concatenate([idx0, idx1]), jnp.concatenate([gate0, gate1])
    else:
        idx, gate = idx0, gate0
    gate2 = jnp.repeat(gate, 2, axis=1)
    table = pack_expert_table(down, up)

    def staged_indices(g):
        src, t0 = (idx0, 0) if g == 0 else (idx, g * tg)
        mixed = src[t0:t0 + tm].reshape(tm // STEP_TOKENS, STEP_TOKENS, PEER_PICKS)
        return jnp.concatenate([mixed[:, STEP_TOKENS // 2:].reshape(-1),
                                src[t0 + tm:t0 + tg].reshape(-1)])

    acc = x
    ngroups = n // tg
    pending = [staged_indices(g) for g in range(min(2, ngroups))]
    for g in range(ngroups):
        t0 = g * tg
        staged = sc_gather_rows(table, pending.pop(0)).reshape(-1, LANES)
        acc = peer_apply(idx, hn, gate2, acc, table, staged, t0, tm)
        if tg > tm:
            acc = peer_apply_staged(hn, gate2, acc, staged, tm // STEP_TOKENS, t0 + tm, tg - tm)
        if g + 2 < ngroups:
            nxt, acc = lax.optimization_barrier((staged_indices(g + 2), acc))
            pending.append(nxt)
    return acc


def attention_layer(x, batch, seq, norm_g, w_in, a_q_gain, a_k_gain, b_q_gain, b_k_gain,
                    lam_q1, lam_k1, lam_q2, lam_k2, b_sub_gain, w_out, lambda_init):
    proj = norm_matmul(x, norm_g, w_in)
    na = A_HEADS // 2
    tile2 = lambda g: jnp.concatenate([g, g])
    out_a = pair_attention(proj, batch, seq, "dilated", 0, na, 2 * na, na,
                           tile2(a_q_gain), tile2(a_k_gain))
    out_b = pair_attention(proj, batch, seq, "diff", 3 * na, 3 * na + B_HEADS, 3 * na + 2 * B_HEADS,
                           B_HEADS, b_q_gain.reshape(-1), b_k_gain.reshape(-1),
                           extras=(lam_q1, lam_k1, lam_q2, lam_k2, b_sub_gain),
                           lambda_init=lambda_init)
    return out_proj(x, out_a, out_b, w_out)


def kernel(x, attn_norm_g, attn_w_in, a_q_gain, a_k_gain, b_q_gain, b_k_gain, lam_q1, lam_k1,
           lam_q2, lam_k2, b_sub_gain, attn_w_out, sgu_norm_g, sgu_w_in, sgu_v_gain, sgu_w_spatial,
           sgu_b_spatial, sgu_w_out, ffn_norm_g, peer_w_query, peer_sub_keys, peer_down, peer_up):
    batch, seq, d = x.shape
    depth = ffn_norm_g.shape[0]
    h = x.reshape(batch * seq, d)
    for layer in range(depth):
        i = layer // 2
        if layer % 2 == 0:
            lambda_init = 0.8 - 0.6 * math.exp(-0.3 * layer)
            h = attention_layer(h, batch, seq, attn_norm_g[i], attn_w_in[i], a_q_gain[i],
                                a_k_gain[i], b_q_gain[i], b_k_gain[i], lam_q1[i], lam_k1[i],
                                lam_q2[i], lam_k2[i], b_sub_gain[i], attn_w_out[i], lambda_init)
        else:
            h = sgu_layer(h, sgu_norm_g[i], sgu_w_in[i], sgu_v_gain[i], sgu_w_spatial[i],
                          sgu_b_spatial[i], sgu_w_out[i])
        h = peer_layer(h, ffn_norm_g[layer], peer_w_query[layer], peer_sub_keys[layer],
                       peer_down[layer], peer_up[layer])
    return h.reshape(batch, seq, d)
```

```python
import functools
import math

import jax
import jax.numpy as jnp
from jax import lax
from jax.experimental import pallas as pl
from jax.experimental.pallas import tpu as pltpu
from jax.experimental.pallas import tpu_sc as plsc

D_MODEL = 1024
HEAD_DIM = 64
EPS = 1e-6
NEG = -1e30
A_HEADS = 8
B_HEADS = 4
C_CHUNK = 128
C_GROUPS = 8
C_WIDTH = 2 * D_MODEL
C_GROUP_DIM = C_WIDTH // C_GROUPS
PEER_HEADS = 8
PEER_NKEYS = 128
PEER_TOPK = 16
PEER_PICKS = PEER_HEADS * PEER_TOPK

LANES = 128
HALF = D_MODEL // 2
VMEM_LIMIT = 56 * 1024 * 1024

BF16 = jnp.bfloat16
F32 = jnp.float32


def _gelu(x):
    return 0.5 * x * (1.0 + jnp.tanh(math.sqrt(2.0 / math.pi) * (x + 0.044715 * (x * x * x))))


def _rms(x, g):
    return x * lax.rsqrt(jnp.mean(x * x, axis=-1, keepdims=True) + EPS) * g


def _dot_nt(a, b):
    return lax.dot_general(a, b, (((1,), (1,)), ((), ())), preferred_element_type=F32)


def _norm_matmul_kernel(x_ref, g_ref, w_ref, o_ref, xn_ref):
    @pl.when(pl.program_id(1) == 0)
    def _():
        xn_ref[...] = _rms(x_ref[...], g_ref[...]).astype(BF16)

    o_ref[...] = jnp.dot(xn_ref[...], w_ref[...], preferred_element_type=F32)


def norm_matmul(x, g, w, *, tm=512, tn=512):
    n, d = x.shape
    nout = w.shape[1]
    tm = min(tm, n)
    return pl.pallas_call(
        _norm_matmul_kernel,
        grid=(n // tm, nout // tn),
        in_specs=[
            pl.BlockSpec((tm, d), lambda i, j: (i, 0)),
            pl.BlockSpec((1, d), lambda i, j: (0, 0)),
            pl.BlockSpec((d, tn), lambda i, j: (0, j)),
        ],
        out_specs=pl.BlockSpec((tm, tn), lambda i, j: (i, j)),
        out_shape=jax.ShapeDtypeStruct((n, nout), F32),
        scratch_shapes=[pltpu.VMEM((tm, d), BF16)],
        compiler_params=pltpu.CompilerParams(
            dimension_semantics=("arbitrary", "arbitrary"), vmem_limit_bytes=VMEM_LIMIT),
        name="norm_matmul",
    )(x, g.reshape(1, d), w.astype(BF16))


def _pair_norm(t, gain, lo):
    sq = t * t
    s_lo = jnp.sum(jnp.where(lo, sq, 0.0), axis=-1, keepdims=True)
    s_hi = jnp.sum(jnp.where(lo, 0.0, sq), axis=-1, keepdims=True)
    ms = jnp.where(lo, s_lo, s_hi) * (1.0 / HEAD_DIM)
    return t * lax.rsqrt(ms + EPS) * gain


def _attn_kernel(*refs, mode, tq, tk, lambda_init):
    if mode == "dilated":
        q_ref, k_ref, v_ref, qg_ref, kg_ref, o_ref, kn_ref = refs
    else:
        (q_ref, k_ref, v_ref, qg_ref, kg_ref, lq1_ref, lk1_ref, lq2_ref, lk2_ref, sg_ref,
         o_ref, kn_ref) = refs
    i = pl.program_id(2)
    lo = lax.broadcasted_iota(jnp.int32, (1, LANES), 1) < HEAD_DIM

    @pl.when(i == 0)
    def _():
        kn_ref[...] = _pair_norm(k_ref[...], kg_ref[...], lo).astype(BF16)

    qn = _pair_norm(q_ref[...], qg_ref[...], lo) * (HEAD_DIM ** -0.5)
    qa = jnp.where(lo, qn, 0.0).astype(BF16)
    qb = jnp.where(lo, 0.0, qn).astype(BF16)
    assert tq == tk and tk % 16 == 0
    row = lax.broadcasted_iota(jnp.int32, (tq, tk), 0)
    col = lax.broadcasted_iota(jnp.int32, (tq, tk), 1)

    def step(j, carry, valid, weight):
        ma, la, acca, mb, lb, accb = carry
        off = pl.multiple_of(j * tk, tk)
        kb = kn_ref[pl.ds(off, tk), :]
        vb = v_ref[pl.ds(off, tk), :].astype(BF16)

        def update(qh, m, l, acc):
            s = _dot_nt(qh, kb)
            if valid is not None:
                s = jnp.where(valid, s, NEG)
            m_new = jnp.maximum(m, jnp.max(s, axis=-1, keepdims=True))
            alpha = jnp.exp(m - m_new)
            p = jnp.exp(s - m_new)
            if weight is not None:
                p = weight * p
            l_new = alpha * l + jnp.sum(p, axis=-1, keepdims=True)
            acc_new = alpha * acc + jnp.dot(p.astype(BF16), vb, preferred_element_type=F32)
            return m_new, l_new, acc_new

        ma, la, acca = update(qa, ma, la, acca)
        mb, lb, accb = update(qb, mb, lb, accb)
        return ma, la, acca, mb, lb, accb

    m0 = jnp.full((tq, 1), NEG, F32)
    l0 = jnp.zeros((tq, 1), F32)
    a0 = jnp.zeros((tq, LANES), F32)
    carry = (m0, l0, a0, m0, l0, a0)
    if mode == "dilated":
        near = (512 + tk - 1) // tk + 1
        first_near = jnp.maximum(i - (near - 1), 0)
        far_valid = ((row - col) & 15) == 0
        carry = lax.fori_loop(0, first_near, lambda j, c: step(j, c, far_valid, None), carry)

        def near_step(j, c):
            dist = (i - j) * tk + row - col
            cnt = ((dist <= 128).astype(F32)
                   + ((dist <= 512) & ((dist & 3) == 0)).astype(F32)
                   + ((dist & 15) == 0).astype(F32))
            cnt = jnp.where(dist >= 0, cnt, 0.0)
            return step(j, c, cnt > 0.0, cnt)

        carry = lax.fori_loop(first_near, i + 1, near_step, carry)
    else:
        carry = lax.fori_loop(0, i, lambda j, c: step(j, c, None, None), carry)
        carry = step(i, carry, col <= row, None)
    ma, la, acca, mb, lb, accb = carry
    oa = acca / la
    ob = accb / lb
    if mode == "dilated":
        o_ref[...] = jnp.where(lo, oa, ob)
    else:
        lam = (jnp.exp(jnp.sum(lq1_ref[...] * lk1_ref[...], axis=-1, keepdims=True))
               - jnp.exp(jnp.sum(lq2_ref[...] * lk2_ref[...], axis=-1, keepdims=True))
               + lambda_init)
        o = oa - lam * ob
        o_ref[...] = _rms(o, sg_ref[...]) * (1.0 - lambda_init)


def pair_attention(proj, batch, seq, mode, qcol, kcol, vcol, npairs, q_gain, k_gain, extras=(),
                   lambda_init=0.0, tq=256, tk=256):
    nq = seq // tq
    small = [q_gain.reshape(1, LANES), k_gain.reshape(1, LANES)] + [e.reshape(1, -1) for e in extras]
    small_specs = [pl.BlockSpec(s.shape, lambda b, p, i: (0, 0)) for s in small]
    kernel = functools.partial(_attn_kernel, mode=mode, tq=tq, tk=tk, lambda_init=lambda_init)
    return pl.pallas_call(
        kernel,
        grid=(batch, npairs, nq),
        in_specs=[
            pl.BlockSpec((tq, LANES), lambda b, p, i: (b * nq + i, qcol + p)),
            pl.BlockSpec((seq, LANES), lambda b, p, i: (b, kcol + p)),
            pl.BlockSpec((seq, LANES), lambda b, p, i: (b, vcol + p)),
        ] + small_specs,
        out_specs=pl.BlockSpec((tq, LANES), lambda b, p, i: (b * nq + i, p)),
        out_shape=jax.ShapeDtypeStruct((batch * seq, npairs * LANES), F32),
        scratch_shapes=[pltpu.VMEM((seq, LANES), BF16)],
        compiler_params=pltpu.CompilerParams(
            dimension_semantics=("arbitrary", "arbitrary", "arbitrary"),
            vmem_limit_bytes=VMEM_LIMIT),
        name="attn_" + mode,
    )(proj, proj, proj, *small)


def _out_proj_kernel(x_ref, a_ref, b_ref, wa_ref, wb_ref, o_ref):
    o_ref[...] = (x_ref[...]
                  + jnp.dot(a_ref[...].astype(BF16), wa_ref[...], preferred_element_type=F32)
                  + jnp.dot(b_ref[...].astype(BF16), wb_ref[...], preferred_element_type=F32))


def out_proj(x, a, b, w, *, tm=512):
    n, d = x.shape
    ka, kb = a.shape[1], b.shape[1]
    w = w.astype(BF16)
    tm = min(tm, n)
    return pl.pallas_call(
        _out_proj_kernel,
        grid=(n // tm,),
        in_specs=[
            pl.BlockSpec((tm, d), lambda i: (i, 0)),
            pl.BlockSpec((tm, ka), lambda i: (i, 0)),
            pl.BlockSpec((tm, kb), lambda i: (i, 0)),
            pl.BlockSpec((ka, d), lambda i: (0, 0)),
            pl.BlockSpec((kb, d), lambda i: (0, 0)),
        ],
        out_specs=pl.BlockSpec((tm, d), lambda i: (i, 0)),
        out_shape=jax.ShapeDtypeStruct((n, d), F32),
        compiler_params=pltpu.CompilerParams(
            dimension_semantics=("arbitrary",), vmem_limit_bytes=VMEM_LIMIT),
        name="out_proj",
    )(x, a, b, w[:ka], w[ka:])


def _sgu_kernel(x_ref, g_ref, win_ref, vg_ref, ws_ref, bs_ref, wout_ref, o_ref, gated_ref, *, tm):
    x = x_ref[...]
    xn = _rms(x, g_ref[...]).astype(BF16)
    z = _gelu(jnp.dot(xn, win_ref[...], preferred_element_type=F32))
    u = z[:, :C_WIDTH]
    v = _rms(z[:, C_WIDTH:], vg_ref[...]).astype(BF16)
    r = lax.broadcasted_iota(jnp.int32, (C_CHUNK, C_CHUNK), 0)
    c = lax.broadcasted_iota(jnp.int32, (C_CHUNK, C_CHUNK), 1)
    causal = c <= r
    for grp in range(C_GROUPS):
        ws = jnp.where(causal, ws_ref[grp], 0.0).astype(BF16)
        bias = bs_ref[:, grp:grp + 1]
        cols = slice(grp * C_GROUP_DIM, (grp + 1) * C_GROUP_DIM)
        for ch in range(tm // C_CHUNK):
            rws = slice(ch * C_CHUNK, (ch + 1) * C_CHUNK)
            gate = jnp.dot(ws, v[rws, cols], preferred_element_type=F32) + bias
            gated_ref[rws, cols] = (u[rws, cols] * gate).astype(BF16)
    o_ref[...] = x + jnp.dot(gated_ref[...], wout_ref[...], preferred_element_type=F32)


def sgu_layer(x, norm_g, w_in, v_gain, w_spatial, b_spatial, w_out, *, tm=256):
    n, d = x.shape
    kernel = functools.partial(_sgu_kernel, tm=tm)
    return pl.pallas_call(
        kernel,
        grid=(n // tm,),
        in_specs=[
            pl.BlockSpec((tm, d), lambda i: (i, 0)),
            pl.BlockSpec((1, d), lambda i: (0, 0)),
            pl.BlockSpec((d, 2 * C_WIDTH), lambda i: (0, 0)),
            pl.BlockSpec((1, C_WIDTH), lambda i: (0, 0)),
            pl.BlockSpec((C_GROUPS, C_CHUNK, C_CHUNK), lambda i: (0, 0, 0)),
            pl.BlockSpec((C_CHUNK, C_GROUPS), lambda i: (0, 0)),
            pl.BlockSpec((C_WIDTH, d), lambda i: (0, 0)),
        ],
        out_specs=pl.BlockSpec((tm, d), lambda i: (i, 0)),
        out_shape=jax.ShapeDtypeStruct((n, d), F32),
        scratch_shapes=[pltpu.VMEM((tm, C_WIDTH), BF16)],
        compiler_params=pltpu.CompilerParams(
            dimension_semantics=("arbitrary",), vmem_limit_bytes=VMEM_LIMIT),
        name="sgu",
    )(x, norm_g.reshape(1, d), w_in.astype(BF16), v_gain.reshape(1, C_WIDTH), w_spatial,
      b_spatial.T, w_out.astype(BF16))


def _peer_scores_kernel(x_ref, g_ref, wq_ref, sk_ref, hn_ref, sc_ref):
    hn = _rms(x_ref[...], g_ref[...])
    hn_ref[...] = hn
    q = jnp.dot(hn.astype(BF16), wq_ref[...], preferred_element_type=F32)
    for hp in range(2 * PEER_HEADS):
        cols = slice(hp * PEER_NKEYS, (hp + 1) * PEER_NKEYS)
        sc_ref[cols, :] = _dot_nt(sk_ref[hp].astype(BF16), q[:, cols].astype(BF16))


def peer_scores(x, norm_g, w_query, sub_keys, *, tm=512):
    n, d = x.shape
    nq = w_query.shape[1]
    tm = min(tm, n)
    sk = sub_keys.reshape(2 * PEER_HEADS, PEER_NKEYS, PEER_NKEYS)
    return pl.pallas_call(
        _peer_scores_kernel,
        grid=(n // tm,),
        in_specs=[
            pl.BlockSpec((tm, d), lambda i: (i, 0)),
            pl.BlockSpec((1, d), lambda i: (0, 0)),
            pl.BlockSpec((d, nq), lambda i: (0, 0)),
            pl.BlockSpec(sk.shape, lambda i: (0, 0, 0)),
        ],
        out_specs=[pl.BlockSpec((tm, d), lambda i: (i, 0)),
                   pl.BlockSpec((nq, tm), lambda i: (0, i))],
        out_shape=[jax.ShapeDtypeStruct((n, d), F32), jax.ShapeDtypeStruct((nq, n), F32)],
        compiler_params=pltpu.CompilerParams(
            dimension_semantics=("arbitrary",), vmem_limit_bytes=VMEM_LIMIT),
        name="peer_scores",
    )(x, norm_g.reshape(1, d), w_query.astype(BF16), sk)


SUBLANES = 8


def _peer_topk_kernel(sc_ref, idx_ref, gate_ref):
    tt = sc_ref.shape[1]
    key = lax.broadcasted_iota(jnp.int32, (PEER_NKEYS, tt), 0).astype(F32)
    row16 = lax.broadcasted_iota(jnp.int32, (PEER_TOPK, tt), 0)
    row8 = lax.broadcasted_iota(jnp.int32, (SUBLANES, tt), 0)
    row8f = row8.astype(F32)
    ninf = jnp.float32(-jnp.inf)

    def extract16(s):
        vals = jnp.zeros((PEER_TOPK, tt), F32)
        ids = jnp.zeros((PEER_TOPK, tt), F32)
        for k in range(PEER_TOPK):
            m = jnp.max(s, axis=0, keepdims=True)
            am = jnp.min(jnp.where(s == m, key, float(PEER_NKEYS)), axis=0, keepdims=True)
            s = jnp.where(key == am, ninf, s)
            vals = jnp.where(row16 == k, m, vals)
            ids = jnp.where(row16 == k, am, ids)
        return vals, ids

    def head(h, carry):
        off = pl.multiple_of(h * 2 * PEER_NKEYS, 2 * PEER_NKEYS)
        v1, i1 = extract16(sc_ref[pl.ds(off, PEER_NKEYS), :])
        v2, i2 = extract16(sc_ref[pl.ds(off + PEER_NKEYS, PEER_NKEYS), :])
        e1 = i1 * float(PEER_NKEYS)
        cand = [v1[0:1] + v2]
        eid = [e1[0:1] + i2]
        pos = [row16.astype(F32)]
        for i in range(1, SUBLANES):
            keep = row8 < (PEER_TOPK // (i + 1))
            cand.append(jnp.where(keep, v1[i:i + 1] + v2[0:SUBLANES], ninf))
            eid.append(e1[i:i + 1] + i2[0:SUBLANES])
            pos.append(row8f + float(i * PEER_TOPK))
        cand.append(v1[SUBLANES:] + v2[0:1])
        eid.append(e1[SUBLANES:] + i2[0:1])
        pos.append((row8f + float(SUBLANES)) * float(PEER_TOPK))
        cand = jnp.concatenate(cand, axis=0)
        eid = jnp.concatenate(eid, axis=0)
        pos = jnp.concatenate(pos, axis=0)
        top = jnp.zeros((PEER_TOPK, tt), F32)
        idx = jnp.zeros((PEER_TOPK, tt), F32)
        for k in range(PEER_TOPK):
            m = jnp.max(cand, axis=0, keepdims=True)
            p = jnp.min(jnp.where(cand == m, pos, 1e9), axis=0, keepdims=True)
            hit = pos == p
            e = jnp.max(jnp.where(hit, eid, -1.0), axis=0, keepdims=True)
            cand = jnp.where(hit, ninf, cand)
            top = jnp.where(row16 == k, m, top)
            idx = jnp.where(row16 == k, e, idx)
        w = jnp.exp(top - top[0:1])
        out = pl.ds(pl.multiple_of(h * PEER_TOPK, PEER_TOPK), PEER_TOPK)
        idx_ref[out, :] = idx.astype(jnp.int32)
        gate_ref[out, :] = w / jnp.sum(w, axis=0, keepdims=True)
        return carry

    lax.fori_loop(0, PEER_HEADS, head, 0)


def peer_topk(scores, first_token, n, *, tt=128):
    rows = scores.shape[0]
    b0 = first_token // tt
    idx_t, gate_t = pl.pallas_call(
        _peer_topk_kernel,
        grid=(n // tt,),
        in_specs=[pl.BlockSpec((rows, tt), lambda i: (0, b0 + i))],
        out_specs=[pl.BlockSpec((PEER_PICKS, tt), lambda i: (0, i)),
                   pl.BlockSpec((PEER_PICKS, tt), lambda i: (0, i))],
        out_shape=[jax.ShapeDtypeStruct((PEER_PICKS, n), jnp.int32),
                   jax.ShapeDtypeStruct((PEER_PICKS, n), F32)],
        compiler_params=pltpu.CompilerParams(
            dimension_semantics=("arbitrary",), vmem_limit_bytes=VMEM_LIMIT),
        name="peer_topk",
    )(scores)
    return idx_t.T, gate_t.T


def pack_expert_table(down, up):
    e, d = down.shape
    rows = min(PACK_ROWS, e)
    packed = pl.pallas_call(
        functools.partial(_pack_kernel, rows=rows),
        grid=(e // rows,),
        in_specs=[pl.BlockSpec((rows, d), lambda i: (i, 0)),
                  pl.BlockSpec((rows, d), lambda i: (i, 0))],
        out_specs=pl.BlockSpec(memory_space=pl.ANY),
        out_shape=jax.ShapeDtypeStruct((e, d // LANES, 1, LANES), jnp.int32),
        scratch_shapes=[pltpu.VMEM((d // LANES, rows, LANES), jnp.int32),
                        pltpu.SemaphoreType.DMA(())],
        compiler_params=pltpu.CompilerParams(
            dimension_semantics=("arbitrary",), vmem_limit_bytes=VMEM_LIMIT),
        name="pack_experts",
    )(down, up)
    return packed.reshape(e, d // LANES, LANES)


PACK_ROWS = 512


def _pack_kernel(down_ref, up_ref, out_ref, stage_ref, sem_ref, *, rows):
    step = pl.program_id(0)
    quarter = HALF // LANES

    def words(t):
        lo = pltpu.bitcast(t[:, :HALF].astype(BF16).astype(F32), jnp.uint32)
        hi = pltpu.bitcast(t[:, HALF:].astype(BF16).astype(F32), jnp.uint32)
        return pltpu.bitcast((hi & jnp.uint32(0xFFFF0000)) | (lo >> 16), jnp.int32)

    for k, ref in enumerate((down_ref, up_ref)):
        w = words(ref[...])
        for c in range(quarter):
            stage_ref[k * quarter + c] = w[:, c * LANES:(c + 1) * LANES]
    for r in range(rows):
        pltpu.make_async_copy(stage_ref.at[:, pl.ds(r, 1), :], out_ref.at[step * rows + r],
                              sem_ref).start(priority=r % 2)
    pltpu.make_async_copy(stage_ref, stage_ref, sem_ref).wait()


def _row_copy(tab_ref, buf_ref, sem_ref, expert, slot, row):
    return pltpu.make_async_copy(tab_ref.at[expert],
                                 buf_ref.at[slot, :, pl.ds(row, 1), :], sem_ref.at[slot])


APPLY_TOKENS = 8
CHUNKS = D_MODEL // LANES
STEP_TOKENS = 4 * APPLY_TOKENS


def _staged_words(st_ref, group, t):
    base = (group * APPLY_TOKENS + t) * PEER_PICKS * CHUNKS
    return [st_ref[pl.ds(base + c, PEER_PICKS, stride=CHUNKS), :] for c in range(CHUNKS)]


def _apply_groups(hn_ref, gate_ref, res_ref, o_ref, groups, between):
    tb = APPLY_TOKENS
    lane_even = (lax.broadcasted_iota(jnp.int32, (1, LANES), 1) % 2) == 0
    even2 = jnp.concatenate([lane_even, lane_even], axis=1)
    tok_of_row = lax.broadcasted_iota(jnp.int32, (2 * tb, 1), 0) % tb
    xxs, rsels, tbls = [], [], []
    for first, _ in groups:
        x = hn_ref[pl.ds(first, tb), :]
        xxs.append(jnp.concatenate([x[:, :HALF], x[:, HALF:]], axis=0).astype(BF16))
        rsels.append(jnp.zeros((2 * tb, 2 * PEER_PICKS), F32))
        tbls.append([])
    for t in range(tb):
        for g, (_, words_fn) in enumerate(groups):
            words = jnp.concatenate(words_fn(t), axis=1)
            tbl = pltpu.bitcast(words, BF16)
            tbls[g].append(tbl)
            rsels[g] = rsels[g] + jnp.where(tok_of_row == t, _dot_nt(xxs[g], tbl[:, :HALF]), 0.0)
        between(t)
    for g, (first, _) in enumerate(groups):
        tok = pl.ds(first, tb)
        coefs = []
        for c in range(2 * PEER_PICKS // LANES):
            cols = slice(c * LANES, (c + 1) * LANES)
            part = jnp.where(lane_even, rsels[g][:tb, cols], rsels[g][tb:, cols])
            hid = part + jnp.where(lane_even, pltpu.roll(part, LANES - 1, 1), pltpu.roll(part, 1, 1))
            coefs.append(gate_ref[tok, cols] * _gelu(hid))
        coef = jnp.concatenate(coefs, axis=1)
        cc = jnp.concatenate([jnp.where(even2, coef, 0.0), jnp.where(even2, 0.0, coef)],
                             axis=0).astype(BF16)
        ysel = jnp.zeros((2 * tb, HALF), F32)
        for t in range(tb):
            ysel = ysel + jnp.where(
                tok_of_row == t, jnp.dot(cc, tbls[g][t][:, HALF:], preferred_element_type=F32), 0.0)
        y = jnp.concatenate([ysel[:tb], ysel[tb:]], axis=1)
        o_ref[tok, :] = res_ref[tok, :] + y


def _peer_apply_kernel(idx0_ref, idx1_ref, hn_ref, gate_ref, res_ref, tab_ref, st_ref, o_ref,
                       buf_ref, sem_ref):
    step = pl.program_id(0)
    nsteps = pl.num_programs(0)
    tb = APPLY_TOKENS

    def issue_token(idx_ref, first_token, slot, t):
        for j in range(PEER_PICKS):
            _row_copy(tab_ref, buf_ref, sem_ref, idx_ref[first_token + t, j], slot,
                      t * PEER_PICKS + j).start(priority=j % 2)

    def wait(slot):
        pltpu.make_async_copy(buf_ref.at[slot], buf_ref.at[slot], sem_ref.at[slot]).wait()

    def copied_words(slot, t):
        return [buf_ref[slot, c, pl.ds(t * PEER_PICKS, PEER_PICKS), :] for c in range(CHUNKS)]

    compute = functools.partial(_apply_groups, hn_ref, gate_ref, res_ref, o_ref)

    @pl.when(step == 0)
    def _():
        for t in range(tb):
            issue_token(idx0_ref, 0, 0, t)

    wait(0)
    compute([(0, functools.partial(copied_words, 0)),
             (2 * tb, functools.partial(_staged_words, st_ref, 0))],
            functools.partial(issue_token, idx0_ref, tb, 1))
    wait(1)
    compute([(tb, functools.partial(copied_words, 1)),
             (3 * tb, functools.partial(_staged_words, st_ref, 1))],
            functools.partial(issue_token, idx1_ref, 0, 0))

    @pl.when(step == nsteps - 1)
    def _():
        wait(0)


def _peer_apply_staged_kernel(hn_ref, gate_ref, res_ref, st_ref, o_ref):
    tb = APPLY_TOKENS
    _apply_groups(hn_ref, gate_ref, res_ref, o_ref,
                  [(0, functools.partial(_staged_words, st_ref, 0)),
                   (tb, functools.partial(_staged_words, st_ref, 1))],
                  lambda t: None)


STAGED_BLOCK_ROWS = 2 * APPLY_TOKENS * PEER_PICKS * CHUNKS


def peer_apply(idx, hn, gate2, acc, table, staged, first_token, ntokens):
    n, d = hn.shape
    tb = STEP_TOKENS
    nsteps = ntokens // tb
    s0 = first_token // tb
    smem = pltpu.SMEM
    return pl.pallas_call(
        _peer_apply_kernel,
        grid=(nsteps,),
        in_specs=[
            pl.BlockSpec((tb, PEER_PICKS), lambda i: (s0 + i, 0), memory_space=smem),
            pl.BlockSpec((tb, PEER_PICKS), lambda i: (s0 + jnp.minimum(i + 1, nsteps - 1), 0),
                         memory_space=smem),
            pl.BlockSpec((tb, d), lambda i: (s0 + i, 0)),
            pl.BlockSpec((tb, 2 * PEER_PICKS), lambda i: (s0 + i, 0)),
            pl.BlockSpec((tb, d), lambda i: (s0 + i, 0)),
            pl.BlockSpec(memory_space=pl.ANY),
            pl.BlockSpec((STAGED_BLOCK_ROWS, LANES), lambda i: (i, 0)),
        ],
        out_specs=pl.BlockSpec((tb, d), lambda i: (s0 + i, 0)),
        out_shape=jax.ShapeDtypeStruct((n, d), F32),
        input_output_aliases={4: 0},
        scratch_shapes=[pltpu.VMEM((2, CHUNKS, APPLY_TOKENS * PEER_PICKS, LANES), jnp.int32),
                        pltpu.SemaphoreType.DMA((2,))],
        compiler_params=pltpu.CompilerParams(
            dimension_semantics=("arbitrary",), vmem_limit_bytes=VMEM_LIMIT),
        name="peer_apply",
    )(idx, idx, hn, gate2, acc, table.reshape(table.shape[0], CHUNKS, 1, LANES), staged)


def peer_apply_staged(hn, gate2, acc, staged, first_block, first_token, ntokens):
    n, d = hn.shape
    tb = 2 * APPLY_TOKENS
    s0 = first_token // tb
    return pl.pallas_call(
        _peer_apply_staged_kernel,
        grid=(ntokens // tb,),
        in_specs=[
            pl.BlockSpec((tb, d), lambda i: (s0 + i, 0)),
            pl.BlockSpec((tb, 2 * PEER_PICKS), lambda i: (s0 + i, 0)),
            pl.BlockSpec((tb, d), lambda i: (s0 + i, 0)),
            pl.BlockSpec((STAGED_BLOCK_ROWS, LANES), lambda i: (first_block + i, 0)),
        ],
        out_specs=pl.BlockSpec((tb, d), lambda i: (s0 + i, 0)),
        out_shape=jax.ShapeDtypeStruct((n, d), F32),
        input_output_aliases={2: 0},
        compiler_params=pltpu.CompilerParams(
            dimension_semantics=("arbitrary",), vmem_limit_bytes=VMEM_LIMIT),
        name="peer_apply_staged",
    )(hn, gate2, acc, staged)


SC_CORES = 2
SC_SUBCORES = 16
SC_CHUNK = 32


def sc_gather_rows(table, idx):
    b = idx.shape[0]
    nw = SC_CORES * SC_SUBCORES
    per_w = b // nw
    nchunks = per_w // SC_CHUNK
    assert per_w * nw == b and nchunks * SC_CHUNK == per_w and nchunks % 2 == 0
    mesh = plsc.VectorSubcoreMesh(core_axis_name="c", subcore_axis_name="s")

    @functools.partial(
        pl.kernel, mesh=mesh,
        out_type=jax.ShapeDtypeStruct((b,) + table.shape[1:], table.dtype),
        scratch_types=[
            pltpu.VMEM((per_w,), jnp.int32),
            pltpu.VMEM((SC_CHUNK,) + table.shape[1:], table.dtype),
            pltpu.VMEM((SC_CHUNK,) + table.shape[1:], table.dtype),
            pltpu.SemaphoreType.DMA,
            pltpu.SemaphoreType.DMA,
        ],
        name="sc_gather_rows",
    )
    def gather(tab_hbm, idx_hbm, out_hbm, idx_v, rows0, rows1, sem0, sem1):
        wid = lax.axis_index("s") * SC_CORES + lax.axis_index("c")
        base = wid * per_w
        pltpu.sync_copy(idx_hbm.at[pl.ds(base, per_w)], idx_v)

        def start(chunk, rows, sem):
            off = pl.multiple_of(chunk * SC_CHUNK, SC_CHUNK)
            pltpu.async_copy(tab_hbm.at[idx_v.at[pl.ds(off, SC_CHUNK)]], rows, sem)

        def finish(chunk, rows, sem):
            off = pl.multiple_of(chunk * SC_CHUNK, SC_CHUNK)
            pltpu.make_async_copy(tab_hbm.at[idx_v.at[pl.ds(off, SC_CHUNK)]], rows, sem).wait()
            pltpu.sync_copy(rows, out_hbm.at[pl.ds(base + off, SC_CHUNK)])

        start(0, rows0, sem0)

        @pl.loop(0, nchunks, step=2)
        def _(i):
            start(i + 1, rows1, sem1)
            finish(i, rows0, sem0)

            @pl.when(i + 2 < nchunks)
            def _():
                start(i + 2, rows0, sem0)

            finish(i + 1, rows1, sem1)

    return gather(table, idx)


PEER_GROUP = 4096
PEER_GROUP_MIXED = 4096


def peer_layer(x, norm_g, w_query, sub_keys, table):
    n = x.shape[0]
    hn, scores = peer_scores(x, norm_g, w_query, sub_keys)
    tg = min(PEER_GROUP, n)
    tm = tg * PEER_GROUP_MIXED // PEER_GROUP
    idx0, gate0 = peer_topk(scores, 0, tg)
    if n > tg:
        idx1, gate1 = peer_topk(scores, tg, n - tg)
        idx, gate = jnp.concatenate([idx0, idx1]), jnp.concatenate([gate0, gate1])
    else:
        idx, gate = idx0, gate0
    gate2 = jnp.repeat(gate, 2, axis=1)

    def staged_indices(g):
        src, t0 = (idx0, 0) if g == 0 else (idx, g * tg)
        mixed = src[t0:t0 + tm].reshape(tm // STEP_TOKENS, STEP_TOKENS, PEER_PICKS)
        return jnp.concatenate([mixed[:, STEP_TOKENS // 2:].reshape(-1),
                                src[t0 + tm:t0 + tg].reshape(-1)])

    acc = x
    ngroups = n // tg
    pending = [staged_indices(g) for g in range(min(2, ngroups))]
    for g in range(ngroups):
        t0 = g * tg
        staged = sc_gather_rows(table, pending.pop(0)).reshape(-1, LANES)
        acc = peer_apply(idx, hn, gate2, acc, table, staged, t0, tm)
        if tg > tm:
            acc = peer_apply_staged(hn, gate2, acc, staged, tm // STEP_TOKENS, t0 + tm, tg - tm)
        if g + 2 < ngroups:
            nxt, acc = lax.optimization_barrier((staged_indices(g + 2), acc))
            pending.append(nxt)
    return acc


def attention_layer(x, batch, seq, norm_g, w_in, a_q_gain, a_k_gain, b_q_gain, b_k_gain,
                    lam_q1, lam_k1, lam_q2, lam_k2, b_sub_gain, w_out, lambda_init):
    proj = norm_matmul(x, norm_g, w_in)
    na = A_HEADS // 2
    tile2 = lambda g: jnp.concatenate([g, g])
    out_a = pair_attention(proj, batch, seq, "dilated", 0, na, 2 * na, na,
                           tile2(a_q_gain), tile2(a_k_gain))
    out_b = pair_attention(proj, batch, seq, "diff", 3 * na, 3 * na + B_HEADS, 3 * na + 2 * B_HEADS,
                           B_HEADS, b_q_gain.reshape(-1), b_k_gain.reshape(-1),
                           extras=(lam_q1, lam_k1, lam_q2, lam_k2, b_sub_gain),
                           lambda_init=lambda_init)
    return out_proj(x, out_a, out_b, w_out)


def kernel(x, attn_norm_g, attn_w_in, a_q_gain, a_k_gain, b_q_gain, b_k_gain, lam_q1, lam_k1,
           lam_q2, lam_k2, b_sub_gain, attn_w_out, sgu_norm_g, sgu_w_in, sgu_v_gain, sgu_w_spatial,
           sgu_b_spatial, sgu_w_out, ffn_norm_g, peer_w_query, peer_sub_keys, peer_down, peer_up):
    batch, seq, d = x.shape
    depth = ffn_norm_g.shape[0]
    h = x.reshape(batch * seq, d)
    tables = [pack_expert_table(peer_down[layer], peer_up[layer]) for layer in range(depth)]
    for layer in range(depth):
        i = layer // 2
        if layer % 2 == 0:
            lambda_init = 0.8 - 0.6 * math.exp(-0.3 * layer)
            h = attention_layer(h, batch, seq, attn_norm_g[i], attn_w_in[i], a_q_gain[i],
                                a_k_gain[i], b_q_gain[i], b_k_gain[i], lam_q1[i], lam_k1[i],
                                lam_q2[i], lam_k2[i], b_sub_gain[i], attn_w_out[i], lambda_init)
        else:
            h = sgu_layer(h, sgu_norm_g[i], sgu_w_in[i], sgu_v_gain[i], sgu_w_spatial[i],
                          sgu_b_spatial[i], sgu_w_out[i])
        h = peer_layer(h, ffn_norm_g[layer], peer_w_query[layer], peer_sub_keys[layer],
                       tables[layer])
    return h.reshape(batch, seq, d)
```

```python
import functools
import math

import jax
import jax.numpy as jnp
from jax import lax
from jax.experimental import pallas as pl
from jax.experimental.pallas import tpu as pltpu
from jax.experimental.pallas import tpu_sc as plsc

D_MODEL = 1024
HEAD_DIM = 64
EPS = 1e-6
NEG = -1e30
A_HEADS = 8
B_HEADS = 4
C_CHUNK = 128
C_GROUPS = 8
C_WIDTH = 2 * D_MODEL
C_GROUP_DIM = C_WIDTH // C_GROUPS
PEER_HEADS = 8
PEER_NKEYS = 128
PEER_TOPK = 16
PEER_PICKS = PEER_HEADS * PEER_TOPK

LANES = 128
HALF = D_MODEL // 2
VMEM_LIMIT = 56 * 1024 * 1024

BF16 = jnp.bfloat16
F32 = jnp.float32


def _gelu(x):
    return 0.5 * x * (1.0 + jnp.tanh(math.sqrt(2.0 / math.pi) * (x + 0.044715 * (x * x * x))))


def _rms(x, g):
    return x * lax.rsqrt(jnp.mean(x * x, axis=-1, keepdims=True) + EPS) * g


def _dot_nt(a, b):
    return lax.dot_general(a, b, (((1,), (1,)), ((), ())), preferred_element_type=F32)


def _norm_matmul_kernel(x_ref, g_ref, w_ref, o_ref, xn_ref):
    @pl.when(pl.program_id(1) == 0)
    def _():
        xn_ref[...] = _rms(x_ref[...], g_ref[...]).astype(BF16)

    o_ref[...] = jnp.dot(xn_ref[...], w_ref[...], preferred_element_type=F32)


def norm_matmul(x, g, w, *, tm=512, tn=512):
    n, d = x.shape
    nout = w.shape[1]
    tm = min(tm, n)
    return pl.pallas_call(
        _norm_matmul_kernel,
        grid=(n // tm, nout // tn),
        in_specs=[
            pl.BlockSpec((tm, d), lambda i, j: (i, 0)),
            pl.BlockSpec((1, d), lambda i, j: (0, 0)),
            pl.BlockSpec((d, tn), lambda i, j: (0, j)),
        ],
        out_specs=pl.BlockSpec((tm, tn), lambda i, j: (i, j)),
        out_shape=jax.ShapeDtypeStruct((n, nout), F32),
        scratch_shapes=[pltpu.VMEM((tm, d), BF16)],
        compiler_params=pltpu.CompilerParams(
            dimension_semantics=("arbitrary", "arbitrary"), vmem_limit_bytes=VMEM_LIMIT),
        name="norm_matmul",
    )(x, g.reshape(1, d), w.astype(BF16))


def _pair_norm(t, gain, lo):
    sq = t * t
    s_lo = jnp.sum(jnp.where(lo, sq, 0.0), axis=-1, keepdims=True)
    s_hi = jnp.sum(jnp.where(lo, 0.0, sq), axis=-1, keepdims=True)
    ms = jnp.where(lo, s_lo, s_hi) * (1.0 / HEAD_DIM)
    return t * lax.rsqrt(ms + EPS) * gain


def _attn_kernel(*refs, mode, tq, tk, lambda_init):
    if mode == "dilated":
        q_ref, k_ref, v_ref, qg_ref, kg_ref, o_ref, kn_ref = refs
    else:
        (q_ref, k_ref, v_ref, qg_ref, kg_ref, lq1_ref, lk1_ref, lq2_ref, lk2_ref, sg_ref,
         o_ref, kn_ref) = refs
    i = pl.program_id(2)
    lo = lax.broadcasted_iota(jnp.int32, (1, LANES), 1) < HEAD_DIM

    @pl.when(i == 0)
    def _():
        kn_ref[...] = _pair_norm(k_ref[...], kg_ref[...], lo).astype(BF16)

    qn = _pair_norm(q_ref[...], qg_ref[...], lo) * (HEAD_DIM ** -0.5)
    qa = jnp.where(lo, qn, 0.0).astype(BF16)
    qb = jnp.where(lo, 0.0, qn).astype(BF16)
    assert tq == tk and tk % 16 == 0
    row = lax.broadcasted_iota(jnp.int32, (tq, tk), 0)
    col = lax.broadcasted_iota(jnp.int32, (tq, tk), 1)

    def step(j, carry, valid, weight):
        ma, la, acca, mb, lb, accb = carry
        off = pl.multiple_of(j * tk, tk)
        kb = kn_ref[pl.ds(off, tk), :]
        vb = v_ref[pl.ds(off, tk), :].astype(BF16)

        def update(qh, m, l, acc):
            s = _dot_nt(qh, kb)
            if valid is not None:
                s = jnp.where(valid, s, NEG)
            m_new = jnp.maximum(m, jnp.max(s, axis=-1, keepdims=True))
            alpha = jnp.exp(m - m_new)
            p = jnp.exp(s - m_new)
            if weight is not None:
                p = weight * p
            l_new = alpha * l + jnp.sum(p, axis=-1, keepdims=True)
            acc_new = alpha * acc + jnp.dot(p.astype(BF16), vb, preferred_element_type=F32)
            return m_new, l_new, acc_new

        ma, la, acca = update(qa, ma, la, acca)
        mb, lb, accb = update(qb, mb, lb, accb)
        return ma, la, acca, mb, lb, accb

    m0 = jnp.full((tq, 1), NEG, F32)
    l0 = jnp.zeros((tq, 1), F32)
    a0 = jnp.zeros((tq, LANES), F32)
    carry = (m0, l0, a0, m0, l0, a0)
    if mode == "dilated":
        near = (512 + tk - 1) // tk + 1
        first_near = jnp.maximum(i - (near - 1), 0)
        far_valid = ((row - col) & 15) == 0
        carry = lax.fori_loop(0, first_near, lambda j, c: step(j, c, far_valid, None), carry)

        def near_step(j, c):
            dist = (i - j) * tk + row - col
            cnt = ((dist <= 128).astype(F32)
                   + ((dist <= 512) & ((dist & 3) == 0)).astype(F32)
                   + ((dist & 15) == 0).astype(F32))
            cnt = jnp.where(dist >= 0, cnt, 0.0)
            return step(j, c, cnt > 0.0, cnt)

        carry = lax.fori_loop(first_near, i + 1, near_step, carry)
    else:
        carry = lax.fori_loop(0, i, lambda j, c: step(j, c, None, None), carry)
        carry = step(i, carry, col <= row, None)
    ma, la, acca, mb, lb, accb = carry
    oa = acca / la
    ob = accb / lb
    if mode == "dilated":
        o_ref[...] = jnp.where(lo, oa, ob)
    else:
        lam = (jnp.exp(jnp.sum(lq1_ref[...] * lk1_ref[...], axis=-1, keepdims=True))
               - jnp.exp(jnp.sum(lq2_ref[...] * lk2_ref[...], axis=-1, keepdims=True))
               + lambda_init)
        o = oa - lam * ob
        o_ref[...] = _rms(o, sg_ref[...]) * (1.0 - lambda_init)


def pair_attention(proj, batch, seq, mode, qcol, kcol, vcol, npairs, q_gain, k_gain, extras=(),
                   lambda_init=0.0, tq=512, tk=512):
    nq = seq // tq
    small = [q_gain.reshape(1, LANES), k_gain.reshape(1, LANES)] + [e.reshape(1, -1) for e in extras]
    small_specs = [pl.BlockSpec(s.shape, lambda b, p, i: (0, 0)) for s in small]
    kernel = functools.partial(_attn_kernel, mode=mode, tq=tq, tk=tk, lambda_init=lambda_init)
    return pl.pallas_call(
        kernel,
        grid=(batch, npairs, nq),
        in_specs=[
            pl.BlockSpec((tq, LANES), lambda b, p, i: (b * nq + i, qcol + p)),
            pl.BlockSpec((seq, LANES), lambda b, p, i: (b, kcol + p)),
            pl.BlockSpec((seq, LANES), lambda b, p, i: (b, vcol + p)),
        ] + small_specs,
        out_specs=pl.BlockSpec((tq, LANES), lambda b, p, i: (b * nq + i, p)),
        out_shape=jax.ShapeDtypeStruct((batch * seq, npairs * LANES), F32),
        scratch_shapes=[pltpu.VMEM((seq, LANES), BF16)],
        compiler_params=pltpu.CompilerParams(
            dimension_semantics=("arbitrary", "arbitrary", "arbitrary"),
            vmem_limit_bytes=VMEM_LIMIT),
        name="attn_" + mode,
    )(proj, proj, proj, *small)


def _out_proj_kernel(x_ref, a_ref, b_ref, wa_ref, wb_ref, o_ref):
    o_ref[...] = (x_ref[...]
                  + jnp.dot(a_ref[...].astype(BF16), wa_ref[...], preferred_element_type=F32)
                  + jnp.dot(b_ref[...].astype(BF16), wb_ref[...], preferred_element_type=F32))


def out_proj(x, a, b, w, *, tm=512):
    n, d = x.shape
    ka, kb = a.shape[1], b.shape[1]
    w = w.astype(BF16)
    tm = min(tm, n)
    return pl.pallas_call(
        _out_proj_kernel,
        grid=(n // tm,),
        in_specs=[
            pl.BlockSpec((tm, d), lambda i: (i, 0)),
            pl.BlockSpec((tm, ka), lambda i: (i, 0)),
            pl.BlockSpec((tm, kb), lambda i: (i, 0)),
            pl.BlockSpec((ka, d), lambda i: (0, 0)),
            pl.BlockSpec((kb, d), lambda i: (0, 0)),
        ],
        out_specs=pl.BlockSpec((tm, d), lambda i: (i, 0)),
        out_shape=jax.ShapeDtypeStruct((n, d), F32),
        compiler_params=pltpu.CompilerParams(
            dimension_semantics=("arbitrary",), vmem_limit_bytes=VMEM_LIMIT),
        name="out_proj",
    )(x, a, b, w[:ka], w[ka:])


def _sgu_kernel(x_ref, g_ref, win_ref, vg_ref, ws_ref, bs_ref, wout_ref, o_ref, gated_ref, *, tm):
    x = x_ref[...]
    xn = _rms(x, g_ref[...]).astype(BF16)
    z = _gelu(jnp.dot(xn, win_ref[...], preferred_element_type=F32))
    u = z[:, :C_WIDTH]
    v = _rms(z[:, C_WIDTH:], vg_ref[...]).astype(BF16)
    r = lax.broadcasted_iota(jnp.int32, (C_CHUNK, C_CHUNK), 0)
    c = lax.broadcasted_iota(jnp.int32, (C_CHUNK, C_CHUNK), 1)
    causal = c <= r
    for grp in range(C_GROUPS):
        ws = jnp.where(causal, ws_ref[grp], 0.0).astype(BF16)
        bias = bs_ref[:, grp:grp + 1]
        cols = slice(grp * C_GROUP_DIM, (grp + 1) * C_GROUP_DIM)
        for ch in range(tm // C_CHUNK):
            rws = slice(ch * C_CHUNK, (ch + 1) * C_CHUNK)
            gate = jnp.dot(ws, v[rws, cols], preferred_element_type=F32) + bias
            gated_ref[rws, cols] = (u[rws, cols] * gate).astype(BF16)
    o_ref[...] = x + jnp.dot(gated_ref[...], wout_ref[...], preferred_element_type=F32)


def sgu_layer(x, norm_g, w_in, v_gain, w_spatial, b_spatial, w_out, *, tm=256):
    n, d = x.shape
    kernel = functools.partial(_sgu_kernel, tm=tm)
    return pl.pallas_call(
        kernel,
        grid=(n // tm,),
        in_specs=[
            pl.BlockSpec((tm, d), lambda i: (i, 0)),
            pl.BlockSpec((1, d), lambda i: (0, 0)),
            pl.BlockSpec((d, 2 * C_WIDTH), lambda i: (0, 0)),
            pl.BlockSpec((1, C_WIDTH), lambda i: (0, 0)),
            pl.BlockSpec((C_GROUPS, C_CHUNK, C_CHUNK), lambda i: (0, 0, 0)),
            pl.BlockSpec((C_CHUNK, C_GROUPS), lambda i: (0, 0)),
            pl.BlockSpec((C_WIDTH, d), lambda i: (0, 0)),
        ],
        out_specs=pl.BlockSpec((tm, d), lambda i: (i, 0)),
        out_shape=jax.ShapeDtypeStruct((n, d), F32),
        scratch_shapes=[pltpu.VMEM((tm, C_WIDTH), BF16)],
        compiler_params=pltpu.CompilerParams(
            dimension_semantics=("arbitrary",), vmem_limit_bytes=VMEM_LIMIT),
        name="sgu",
    )(x, norm_g.reshape(1, d), w_in.astype(BF16), v_gain.reshape(1, C_WIDTH), w_spatial,
      b_spatial.T, w_out.astype(BF16))


def _peer_scores_kernel(x_ref, g_ref, wq_ref, sk_ref, hn_ref, sc_ref):
    hn = _rms(x_ref[...], g_ref[...])
    hn_ref[...] = hn
    q = jnp.dot(hn.astype(BF16), wq_ref[...], preferred_element_type=F32)
    for hp in range(2 * PEER_HEADS):
        cols = slice(hp * PEER_NKEYS, (hp + 1) * PEER_NKEYS)
        sc_ref[cols, :] = _dot_nt(sk_ref[hp].astype(BF16), q[:, cols].astype(BF16))


def peer_scores(x, norm_g, w_query, sub_keys, *, tm=512):
    n, d = x.shape
    nq = w_query.shape[1]
    tm = min(tm, n)
    sk = sub_keys.reshape(2 * PEER_HEADS, PEER_NKEYS, PEER_NKEYS)
    return pl.pallas_call(
        _peer_scores_kernel,
        grid=(n // tm,),
        in_specs=[
            pl.BlockSpec((tm, d), lambda i: (i, 0)),
            pl.BlockSpec((1, d), lambda i: (0, 0)),
            pl.BlockSpec((d, nq), lambda i: (0, 0)),
            pl.BlockSpec(sk.shape, lambda i: (0, 0, 0)),
        ],
        out_specs=[pl.BlockSpec((tm, d), lambda i: (i, 0)),
                   pl.BlockSpec((nq, tm), lambda i: (0, i))],
        out_shape=[jax.ShapeDtypeStruct((n, d), F32), jax.ShapeDtypeStruct((nq, n), F32)],
        compiler_params=pltpu.CompilerParams(
            dimension_semantics=("arbitrary",), vmem_limit_bytes=VMEM_LIMIT),
        name="peer_scores",
    )(x, norm_g.reshape(1, d), w_query.astype(BF16), sk)


SUBLANES = 8


def _peer_topk_kernel(sc_ref, idx_ref, gate_ref):
    tt = sc_ref.shape[1]
    key = lax.broadcasted_iota(jnp.int32, (PEER_NKEYS, tt), 0).astype(F32)
    row16 = lax.broadcasted_iota(jnp.int32, (PEER_TOPK, tt), 0)
    row8 = lax.broadcasted_iota(jnp.int32, (SUBLANES, tt), 0)
    row8f = row8.astype(F32)
    ninf = jnp.float32(-jnp.inf)

    def extract16(s):
        vals = jnp.zeros((PEER_TOPK, tt), F32)
        ids = jnp.zeros((PEER_TOPK, tt), F32)
        for k in range(PEER_TOPK):
            m = jnp.max(s, axis=0, keepdims=True)
            am = jnp.min(jnp.where(s == m, key, float(PEER_NKEYS)), axis=0, keepdims=True)
            s = jnp.where(key == am, ninf, s)
            vals = jnp.where(row16 == k, m, vals)
            ids = jnp.where(row16 == k, am, ids)
        return vals, ids

    def head(h, carry):
        off = pl.multiple_of(h * 2 * PEER_NKEYS, 2 * PEER_NKEYS)
        v1, i1 = extract16(sc_ref[pl.ds(off, PEER_NKEYS), :])
        v2, i2 = extract16(sc_ref[pl.ds(off + PEER_NKEYS, PEER_NKEYS), :])
        e1 = i1 * float(PEER_NKEYS)
        cand = [v1[0:1] + v2]
        eid = [e1[0:1] + i2]
        pos = [row16.astype(F32)]
        for i in range(1, SUBLANES):
            keep = row8 < (PEER_TOPK // (i + 1))
            cand.append(jnp.where(keep, v1[i:i + 1] + v2[0:SUBLANES], ninf))
            eid.append(e1[i:i + 1] + i2[0:SUBLANES])
            pos.append(row8f + float(i * PEER_TOPK))
        cand.append(v1[SUBLANES:] + v2[0:1])
        eid.append(e1[SUBLANES:] + i2[0:1])
        pos.append((row8f + float(SUBLANES)) * float(PEER_TOPK))
        cand = jnp.concatenate(cand, axis=0)
        eid = jnp.concatenate(eid, axis=0)
        pos = jnp.concatenate(pos, axis=0)
        top = jnp.zeros((PEER_TOPK, tt), F32)
        idx = jnp.zeros((PEER_TOPK, tt), F32)
        for k in range(PEER_TOPK):
            m = jnp.max(cand, axis=0, keepdims=True)
            p = jnp.min(jnp.where(cand == m, pos, 1e9), axis=0, keepdims=True)
            hit = pos == p
            e = jnp.max(jnp.where(hit, eid, -1.0), axis=0, keepdims=True)
            cand = jnp.where(hit, ninf, cand)
            top = jnp.where(row16 == k, m, top)
            idx = jnp.where(row16 == k, e, idx)
        w = jnp.exp(top - top[0:1])
        out = pl.ds(pl.multiple_of(h * PEER_TOPK, PEER_TOPK), PEER_TOPK)
        idx_ref[out, :] = idx.astype(jnp.int32)
        gate_ref[out, :] = w / jnp.sum(w, axis=0, keepdims=True)
        return carry

    lax.fori_loop(0, PEER_HEADS, head, 0)


def peer_topk(scores, first_token, n, *, tt=128):
    rows = scores.shape[0]
    b0 = first_token // tt
    idx_t, gate_t = pl.pallas_call(
        _peer_topk_kernel,
        grid=(n // tt,),
        in_specs=[pl.BlockSpec((rows, tt), lambda i: (0, b0 + i))],
        out_specs=[pl.BlockSpec((PEER_PICKS, tt), lambda i: (0, i)),
                   pl.BlockSpec((PEER_PICKS, tt), lambda i: (0, i))],
        out_shape=[jax.ShapeDtypeStruct((PEER_PICKS, n), jnp.int32),
                   jax.ShapeDtypeStruct((PEER_PICKS, n), F32)],
        compiler_params=pltpu.CompilerParams(
            dimension_semantics=("arbitrary",), vmem_limit_bytes=VMEM_LIMIT),
        name="peer_topk",
    )(scores)
    return idx_t.T, gate_t.T


def pack_expert_table(down, up):
    e, d = down.shape
    rows = min(PACK_ROWS, e)
    packed = pl.pallas_call(
        functools.partial(_pack_kernel, rows=rows),
        grid=(e // rows,),
        in_specs=[pl.BlockSpec((rows, d), lambda i: (i, 0)),
                  pl.BlockSpec((rows, d), lambda i: (i, 0))],
        out_specs=pl.BlockSpec(memory_space=pl.ANY),
        out_shape=jax.ShapeDtypeStruct((e, d // LANES, 1, LANES), jnp.int32),
        scratch_shapes=[pltpu.VMEM((d // LANES, rows, LANES), jnp.int32),
                        pltpu.SemaphoreType.DMA(())],
        compiler_params=pltpu.CompilerParams(
            dimension_semantics=("arbitrary",), vmem_limit_bytes=VMEM_LIMIT),
        name="pack_experts",
    )(down, up)
    return packed.reshape(e, d // LANES, LANES)


PACK_ROWS = 512


def _pack_kernel(down_ref, up_ref, out_ref, stage_ref, sem_ref, *, rows):
    step = pl.program_id(0)
    quarter = HALF // LANES

    def words(t):
        lo = pltpu.bitcast(t[:, :HALF].astype(BF16).astype(F32), jnp.uint32)
        hi = pltpu.bitcast(t[:, HALF:].astype(BF16).astype(F32), jnp.uint32)
        return pltpu.bitcast((hi & jnp.uint32(0xFFFF0000)) | (lo >> 16), jnp.int32)

    for k, ref in enumerate((down_ref, up_ref)):
        w = words(ref[...])
        for c in range(quarter):
            stage_ref[k * quarter + c] = w[:, c * LANES:(c + 1) * LANES]
    for r in range(rows):
        pltpu.make_async_copy(stage_ref.at[:, pl.ds(r, 1), :], out_ref.at[step * rows + r],
                              sem_ref).start(priority=r % 2)
    pltpu.make_async_copy(stage_ref, stage_ref, sem_ref).wait()


def _row_copy(tab_ref, buf_ref, sem_ref, expert, slot, row):
    return pltpu.make_async_copy(tab_ref.at[expert],
                                 buf_ref.at[slot, :, pl.ds(row, 1), :], sem_ref.at[slot])


APPLY_TOKENS = 8
CHUNKS = D_MODEL // LANES
STEP_TOKENS = 4 * APPLY_TOKENS


def _staged_words(st_ref, group, t):
    base = (group * APPLY_TOKENS + t) * PEER_PICKS * CHUNKS
    return [st_ref[pl.ds(base + c, PEER_PICKS, stride=CHUNKS), :] for c in range(CHUNKS)]


def _apply_groups(hn_ref, gate_ref, res_ref, o_ref, groups, between):
    tb = APPLY_TOKENS
    lane_even = (lax.broadcasted_iota(jnp.int32, (1, LANES), 1) % 2) == 0
    even2 = jnp.concatenate([lane_even, lane_even], axis=1)
    tok_of_row = lax.broadcasted_iota(jnp.int32, (2 * tb, 1), 0) % tb
    xxs, rsels, tbls = [], [], []
    for first, _ in groups:
        x = hn_ref[pl.ds(first, tb), :]
        xxs.append(jnp.concatenate([x[:, :HALF], x[:, HALF:]], axis=0).astype(BF16))
        rsels.append(jnp.zeros((2 * tb, 2 * PEER_PICKS), F32))
        tbls.append([])
    for t in range(tb):
        for g, (_, words_fn) in enumerate(groups):
            words = jnp.concatenate(words_fn(t), axis=1)
            tbl = pltpu.bitcast(words, BF16)
            tbls[g].append(tbl)
            rsels[g] = rsels[g] + jnp.where(tok_of_row == t, _dot_nt(xxs[g], tbl[:, :HALF]), 0.0)
        between(t)
    for g, (first, _) in enumerate(groups):
        tok = pl.ds(first, tb)
        coefs = []
        for c in range(2 * PEER_PICKS // LANES):
            cols = slice(c * LANES, (c + 1) * LANES)
            part = jnp.where(lane_even, rsels[g][:tb, cols], rsels[g][tb:, cols])
            hid = part + jnp.where(lane_even, pltpu.roll(part, LANES - 1, 1), pltpu.roll(part, 1, 1))
            coefs.append(gate_ref[tok, cols] * _gelu(hid))
        coef = jnp.concatenate(coefs, axis=1)
        cc = jnp.concatenate([jnp.where(even2, coef, 0.0), jnp.where(even2, 0.0, coef)],
                             axis=0).astype(BF16)
        ysel = jnp.zeros((2 * tb, HALF), F32)
        for t in range(tb):
            ysel = ysel + jnp.where(
                tok_of_row == t, jnp.dot(cc, tbls[g][t][:, HALF:], preferred_element_type=F32), 0.0)
        y = jnp.concatenate([ysel[:tb], ysel[tb:]], axis=1)
        o_ref[tok, :] = res_ref[tok, :] + y


def _peer_apply_kernel(idx0_ref, idx1_ref, hn_ref, gate_ref, res_ref, tab_ref, st_ref, o_ref,
                       buf_ref, sem_ref):
    step = pl.program_id(0)
    nsteps = pl.num_programs(0)
    tb = APPLY_TOKENS

    def issue_token(idx_ref, first_token, slot, t):
        for j in range(PEER_PICKS):
            _row_copy(tab_ref, buf_ref, sem_ref, idx_ref[first_token + t, j], slot,
                      t * PEER_PICKS + j).start(priority=j % 2)

    def wait(slot):
        pltpu.make_async_copy(buf_ref.at[slot], buf_ref.at[slot], sem_ref.at[slot]).wait()

    def copied_words(slot, t):
        return [buf_ref[slot, c, pl.ds(t * PEER_PICKS, PEER_PICKS), :] for c in range(CHUNKS)]

    compute = functools.partial(_apply_groups, hn_ref, gate_ref, res_ref, o_ref)

    @pl.when(step == 0)
    def _():
        for t in range(tb):
            issue_token(idx0_ref, 0, 0, t)

    wait(0)
    compute([(0, functools.partial(copied_words, 0)),
             (2 * tb, functools.partial(_staged_words, st_ref, 0))],
            functools.partial(issue_token, idx0_ref, tb, 1))
    wait(1)
    compute([(tb, functools.partial(copied_words, 1)),
             (3 * tb, functools.partial(_staged_words, st_ref, 1))],
            functools.partial(issue_token, idx1_ref, 0, 0))

    @pl.when(step == nsteps - 1)
    def _():
        wait(0)


def _peer_apply_staged_kernel(hn_ref, gate_ref, res_ref, st_ref, o_ref):
    tb = APPLY_TOKENS
    _apply_groups(hn_ref, gate_ref, res_ref, o_ref,
                  [(0, functools.partial(_staged_words, st_ref, 0)),
                   (tb, functools.partial(_staged_words, st_ref, 1))],
                  lambda t: None)


STAGED_BLOCK_ROWS = 2 * APPLY_TOKENS * PEER_PICKS * CHUNKS


def peer_apply(idx, hn, gate2, acc, table, staged, first_token, ntokens):
    n, d = hn.shape
    tb = STEP_TOKENS
    nsteps = ntokens // tb
    s0 = first_token // tb
    smem = pltpu.SMEM
    return pl.pallas_call(
        _peer_apply_kernel,
        grid=(nsteps,),
        in_specs=[
            pl.BlockSpec((tb, PEER_PICKS), lambda i: (s0 + i, 0), memory_space=smem),
            pl.BlockSpec((tb, PEER_PICKS), lambda i: (s0 + jnp.minimum(i + 1, nsteps - 1), 0),
                         memory_space=smem),
            pl.BlockSpec((tb, d), lambda i: (s0 + i, 0)),
            pl.BlockSpec((tb, 2 * PEER_PICKS), lambda i: (s0 + i, 0)),
            pl.BlockSpec((tb, d), lambda i: (s0 + i, 0)),
            pl.BlockSpec(memory_space=pl.ANY),
            pl.BlockSpec((STAGED_BLOCK_ROWS, LANES), lambda i: (i, 0)),
        ],
        out_specs=pl.BlockSpec((tb, d), lambda i: (s0 + i, 0)),
        out_shape=jax.ShapeDtypeStruct((n, d), F32),
        input_output_aliases={4: 0},
        scratch_shapes=[pltpu.VMEM((2, CHUNKS, APPLY_TOKENS * PEER_PICKS, LANES), jnp.int32),
                        pltpu.SemaphoreType.DMA((2,))],
        compiler_params=pltpu.CompilerParams(
            dimension_semantics=("arbitrary",), vmem_limit_bytes=VMEM_LIMIT),
        name="peer_apply",
    )(idx, idx, hn, gate2, acc, table.reshape(table.shape[0], CHUNKS, 1, LANES), staged)


def peer_apply_staged(hn, gate2, acc, staged, first_block, first_token, ntokens):
    n, d = hn.shape
    tb = 2 * APPLY_TOKENS
    s0 = first_token // tb
    return pl.pallas_call(
        _peer_apply_staged_kernel,
        grid=(ntokens // tb,),
        in_specs=[
            pl.BlockSpec((tb, d), lambda i: (s0 + i, 0)),
            pl.BlockSpec((tb, 2 * PEER_PICKS), lambda i: (s0 + i, 0)),
            pl.BlockSpec((tb, d), lambda i: (s0 + i, 0)),
            pl.BlockSpec((STAGED_BLOCK_ROWS, LANES), lambda i: (first_block + i, 0)),
        ],
        out_specs=pl.BlockSpec((tb, d), lambda i: (s0 + i, 0)),
        out_shape=jax.ShapeDtypeStruct((n, d), F32),
        input_output_aliases={2: 0},
        compiler_params=pltpu.CompilerParams(
            dimension_semantics=("arbitrary",), vmem_limit_bytes=VMEM_LIMIT),
        name="peer_apply_staged",
    )(hn, gate2, acc, staged)


SC_CORES = 2
SC_SUBCORES = 16
SC_CHUNK = 32


def sc_gather_rows(table, idx):
    b = idx.shape[0]
    nw = SC_CORES * SC_SUBCORES
    per_w = b // nw
    nchunks = per_w // SC_CHUNK
    assert per_w * nw == b and nchunks * SC_CHUNK == per_w and nchunks % 2 == 0
    mesh = plsc.VectorSubcoreMesh(core_axis_name="c", subcore_axis_name="s")

    @functools.partial(
        pl.kernel, mesh=mesh,
        out_type=jax.ShapeDtypeStruct((b,) + table.shape[1:], table.dtype),
        scratch_types=[
            pltpu.VMEM((per_w,), jnp.int32),
            pltpu.VMEM((SC_CHUNK,) + table.shape[1:], table.dtype),
            pltpu.VMEM((SC_CHUNK,) + table.shape[1:], table.dtype),
            pltpu.SemaphoreType.DMA,
            pltpu.SemaphoreType.DMA,
        ],
        name="sc_gather_rows",
    )
    def gather(tab_hbm, idx_hbm, out_hbm, idx_v, rows0, rows1, sem0, sem1):
        wid = lax.axis_index("s") * SC_CORES + lax.axis_index("c")
        base = wid * per_w
        pltpu.sync_copy(idx_hbm.at[pl.ds(base, per_w)], idx_v)

        def start(chunk, rows, sem):
            off = pl.multiple_of(chunk * SC_CHUNK, SC_CHUNK)
            pltpu.async_copy(tab_hbm.at[idx_v.at[pl.ds(off, SC_CHUNK)]], rows, sem)

        def finish(chunk, rows, sem):
            off = pl.multiple_of(chunk * SC_CHUNK, SC_CHUNK)
            pltpu.make_async_copy(tab_hbm.at[idx_v.at[pl.ds(off, SC_CHUNK)]], rows, sem).wait()
            pltpu.sync_copy(rows, out_hbm.at[pl.ds(base + off, SC_CHUNK)])

        @pl.loop(0, nchunks, step=2)
        def _(i):
            start(i, rows0, sem0)
            finish(i, rows0, sem0)
            start(i + 1, rows1, sem1)
            finish(i + 1, rows1, sem1)

    return gather(table, idx)


PEER_GROUP = 4096
PEER_GROUP_MIXED = 4096


def peer_layer(x, norm_g, w_query, sub_keys, table):
    n = x.shape[0]
    hn, scores = peer_scores(x, norm_g, w_query, sub_keys)
    tg = min(PEER_GROUP, n)
    tm = tg * PEER_GROUP_MIXED // PEER_GROUP
    idx0, gate0 = peer_topk(scores, 0, tg)
    if n > tg:
        idx1, gate1 = peer_topk(scores, tg, n - tg)
        idx, gate = jnp.concatenate([idx0, idx1]), jnp.concatenate([gate0, gate1])
    else:
        idx, gate = idx0, gate0
    gate2 = jnp.repeat(gate, 2, axis=1)

    def staged_indices(g):
        src, t0 = (idx0, 0) if g == 0 else (idx, g * tg)
        mixed = src[t0:t0 + tm].reshape(tm // STEP_TOKENS, STEP_TOKENS, PEER_PICKS)
        return jnp.concatenate([mixed[:, STEP_TOKENS // 2:].reshape(-1),
                                src[t0 + tm:t0 + tg].reshape(-1)])

    acc = x
    ngroups = n // tg
    pending = [staged_indices(g) for g in range(min(2, ngroups))]
    for g in range(ngroups):
        t0 = g * tg
        staged = sc_gather_rows(table, pending.pop(0)).reshape(-1, LANES)
        acc = peer_apply(idx, hn, gate2, acc, table, staged, t0, tm)
        if tg > tm:
            acc = peer_apply_staged(hn, gate2, acc, staged, tm // STEP_TOKENS, t0 + tm, tg - tm)
        if g + 2 < ngroups:
            nxt, acc = lax.optimization_barrier((staged_indices(g + 2), acc))
            pending.append(nxt)
    return acc


def attention_layer(x, batch, seq, norm_g, w_in, a_q_gain, a_k_gain, b_q_gain, b_k_gain,
                    lam_q1, lam_k1, lam_q2, lam_k2, b_sub_gain, w_out, lambda_init):
    proj = norm_matmul(x, norm_g, w_in)
    na = A_HEADS // 2
    tile2 = lambda g: jnp.concatenate([g, g])
    out_a = pair_attention(proj, batch, seq, "dilated", 0, na, 2 * na, na,
                           tile2(a_q_gain), tile2(a_k_gain))
    out_b = pair_attention(proj, batch, seq, "diff", 3 * na, 3 * na + B_HEADS, 3 * na + 2 * B_HEADS,
                           B_HEADS, b_q_gain.reshape(-1), b_k_gain.reshape(-1),
                           extras=(lam_q1, lam_k1, lam_q2, lam_k2, b_sub_gain),
                           lambda_init=lambda_init)
    return out_proj(x, out_a, out_b, w_out)


def kernel(x, attn_norm_g, attn_w_in, a_q_gain, a_k_gain, b_q_gain, b_k_gain, lam_q1, lam_k1,
           lam_q2, lam_k2, b_sub_gain, attn_w_out, sgu_norm_g, sgu_w_in, sgu_v_gain, sgu_w_spatial,
           sgu_b_spatial, sgu_w_out, ffn_norm_g, peer_w_query, peer_sub_keys, peer_down, peer_up):
    batch, seq, d = x.shape
    depth = ffn_norm_g.shape[0]
    h = x.reshape(batch * seq, d)
    tables = [pack_expert_table(peer_down[layer], peer_up[layer]) for layer in range(depth)]
    for layer in range(depth):
        i = layer // 2
        if layer % 2 == 0:
            lambda_init = 0.8 - 0.6 * math.exp(-0.3 * layer)
            h = attention_layer(h, batch, seq, attn_norm_g[i], attn_w_in[i], a_q_gain[i],
                                a_k_gain[i], b_q_gain[i], b_k_gain[i], lam_q1[i], lam_k1[i],
                                lam_q2[i], lam_k2[i], b_sub_gain[i], attn_w_out[i], lambda_init)
        else:
            h = sgu_layer(h, sgu_norm_g[i], sgu_w_in[i], sgu_v_gain[i], sgu_w_spatial[i],
                          sgu_b_spatial[i], sgu_w_out[i])
        h = peer_layer(h, ffn_norm_g[layer], peer_w_query[layer], peer_sub_keys[layer],
                       tables[layer])
    return h.reshape(batch, seq, d)
```

```python
import functools
import math

import jax
import jax.numpy as jnp
from jax import lax
from jax.experimental import pallas as pl
from jax.experimental.pallas import tpu as pltpu
from jax.experimental.pallas import tpu_sc as plsc

D_MODEL = 1024
HEAD_DIM = 64
EPS = 1e-6
NEG = -1e30
A_HEADS = 8
B_HEADS = 4
C_CHUNK = 128
C_GROUPS = 8
C_WIDTH = 2 * D_MODEL
C_GROUP_DIM = C_WIDTH // C_GROUPS
PEER_HEADS = 8
PEER_NKEYS = 128
PEER_TOPK = 16
PEER_PICKS = PEER_HEADS * PEER_TOPK

LANES = 128
HALF = D_MODEL // 2
VMEM_LIMIT = 56 * 1024 * 1024

BF16 = jnp.bfloat16
F32 = jnp.float32


def _gelu(x):
    return 0.5 * x * (1.0 + jnp.tanh(math.sqrt(2.0 / math.pi) * (x + 0.044715 * (x * x * x))))


def _rms(x, g):
    return x * lax.rsqrt(jnp.mean(x * x, axis=-1, keepdims=True) + EPS) * g


def _dot_nt(a, b):
    return lax.dot_general(a, b, (((1,), (1,)), ((), ())), preferred_element_type=F32)


def _norm_matmul_kernel(x_ref, g_ref, w_ref, o_ref, xn_ref):
    @pl.when(pl.program_id(1) == 0)
    def _():
        xn_ref[...] = _rms(x_ref[...], g_ref[...]).astype(BF16)

    o_ref[...] = jnp.dot(xn_ref[...], w_ref[...], preferred_element_type=F32)


def norm_matmul(x, g, w, *, tm=1024, tn=1024):
    n, d = x.shape
    nout = w.shape[1]
    tm = min(tm, n)
    return pl.pallas_call(
        _norm_matmul_kernel,
        grid=(n // tm, nout // tn),
        in_specs=[
            pl.BlockSpec((tm, d), lambda i, j: (i, 0)),
            pl.BlockSpec((1, d), lambda i, j: (0, 0)),
            pl.BlockSpec((d, tn), lambda i, j: (0, j)),
        ],
        out_specs=pl.BlockSpec((tm, tn), lambda i, j: (i, j)),
        out_shape=jax.ShapeDtypeStruct((n, nout), F32),
        scratch_shapes=[pltpu.VMEM((tm, d), BF16)],
        compiler_params=pltpu.CompilerParams(
            dimension_semantics=("arbitrary", "arbitrary"), vmem_limit_bytes=VMEM_LIMIT),
        name="norm_matmul",
    )(x, g.reshape(1, d), w.astype(BF16))


def _pair_norm(t, gain, lo):
    sq = t * t
    s_lo = jnp.sum(jnp.where(lo, sq, 0.0), axis=-1, keepdims=True)
    s_hi = jnp.sum(jnp.where(lo, 0.0, sq), axis=-1, keepdims=True)
    ms = jnp.where(lo, s_lo, s_hi) * (1.0 / HEAD_DIM)
    return t * lax.rsqrt(ms + EPS) * gain


def _attn_kernel(*refs, mode, tq, tk, lambda_init):
    if mode == "dilated":
        q_ref, k_ref, v_ref, qg_ref, kg_ref, o_ref, kn_ref = refs
    else:
        (q_ref, k_ref, v_ref, qg_ref, kg_ref, lq1_ref, lk1_ref, lq2_ref, lk2_ref, sg_ref,
         o_ref, kn_ref) = refs
    i = pl.program_id(2)
    lo = lax.broadcasted_iota(jnp.int32, (1, LANES), 1) < HEAD_DIM

    @pl.when(i == 0)
    def _():
        kn_ref[...] = _pair_norm(k_ref[...], kg_ref[...], lo).astype(BF16)

    qn = _pair_norm(q_ref[...], qg_ref[...], lo) * (HEAD_DIM ** -0.5)
    qa = jnp.where(lo, qn, 0.0).astype(BF16)
    qb = jnp.where(lo, 0.0, qn).astype(BF16)
    assert tq == tk and tk % 16 == 0
    row = lax.broadcasted_iota(jnp.int32, (tq, tk), 0)
    col = lax.broadcasted_iota(jnp.int32, (tq, tk), 1)

    def step(j, carry, valid, weight):
        ma, la, acca, mb, lb, accb = carry
        off = pl.multiple_of(j * tk, tk)
        kb = kn_ref[pl.ds(off, tk), :]
        vb = v_ref[pl.ds(off, tk), :].astype(BF16)

        def update(qh, m, l, acc):
            s = _dot_nt(qh, kb)
            if valid is not None:
                s = jnp.where(valid, s, NEG)
            m_new = jnp.maximum(m, jnp.max(s, axis=-1, keepdims=True))
            alpha = jnp.exp(m - m_new)
            p = jnp.exp(s - m_new)
            if weight is not None:
                p = weight * p
            l_new = alpha * l + jnp.sum(p, axis=-1, keepdims=True)
            acc_new = alpha * acc + jnp.dot(p.astype(BF16), vb, preferred_element_type=F32)
            return m_new, l_new, acc_new

        ma, la, acca = update(qa, ma, la, acca)
        mb, lb, accb = update(qb, mb, lb, accb)
        return ma, la, acca, mb, lb, accb

    m0 = jnp.full((tq, 1), NEG, F32)
    l0 = jnp.zeros((tq, 1), F32)
    a0 = jnp.zeros((tq, LANES), F32)
    carry = (m0, l0, a0, m0, l0, a0)
    if mode == "dilated":
        near = (512 + tk - 1) // tk + 1
        first_near = jnp.maximum(i - (near - 1), 0)
        far_valid = ((row - col) & 15) == 0
        carry = lax.fori_loop(0, first_near, lambda j, c: step(j, c, far_valid, None), carry)

        def near_step(j, c):
            dist = (i - j) * tk + row - col
            cnt = ((dist <= 128).astype(F32)
                   + ((dist <= 512) & ((dist & 3) == 0)).astype(F32)
                   + ((dist & 15) == 0).astype(F32))
            cnt = jnp.where(dist >= 0, cnt, 0.0)
            return step(j, c, cnt > 0.0, cnt)

        carry = lax.fori_loop(first_near, i + 1, near_step, carry)
    else:
        carry = lax.fori_loop(0, i, lambda j, c: step(j, c, None, None), carry)
        carry = step(i, carry, col <= row, None)
    ma, la, acca, mb, lb, accb = carry
    oa = acca / la
    ob = accb / lb
    if mode == "dilated":
        o_ref[...] = jnp.where(lo, oa, ob)
    else:
        lam = (jnp.exp(jnp.sum(lq1_ref[...] * lk1_ref[...], axis=-1, keepdims=True))
               - jnp.exp(jnp.sum(lq2_ref[...] * lk2_ref[...], axis=-1, keepdims=True))
               + lambda_init)
        o = oa - lam * ob
        o_ref[...] = _rms(o, sg_ref[...]) * (1.0 - lambda_init)


def pair_attention(proj, batch, seq, mode, qcol, kcol, vcol, npairs, q_gain, k_gain, extras=(),
                   lambda_init=0.0, tq=512, tk=512):
    nq = seq // tq
    small = [q_gain.reshape(1, LANES), k_gain.reshape(1, LANES)] + [e.reshape(1, -1) for e in extras]
    small_specs = [pl.BlockSpec(s.shape, lambda b, p, i: (0, 0)) for s in small]
    kernel = functools.partial(_attn_kernel, mode=mode, tq=tq, tk=tk, lambda_init=lambda_init)
    return pl.pallas_call(
        kernel,
        grid=(batch, npairs, nq),
        in_specs=[
            pl.BlockSpec((tq, LANES), lambda b, p, i: (b * nq + i, qcol + p)),
            pl.BlockSpec((seq, LANES), lambda b, p, i: (b, kcol + p)),
            pl.BlockSpec((seq, LANES), lambda b, p, i: (b, vcol + p)),
        ] + small_specs,
        out_specs=pl.BlockSpec((tq, LANES), lambda b, p, i: (b * nq + i, p)),
        out_shape=jax.ShapeDtypeStruct((batch * seq, npairs * LANES), F32),
        scratch_shapes=[pltpu.VMEM((seq, LANES), BF16)],
        compiler_params=pltpu.CompilerParams(
            dimension_semantics=("arbitrary", "arbitrary", "arbitrary"),
            vmem_limit_bytes=VMEM_LIMIT),
        name="attn_" + mode,
    )(proj, proj, proj, *small)


def _out_proj_kernel(x_ref, a_ref, b_ref, wa_ref, wb_ref, o_ref):
    o_ref[...] = (x_ref[...]
                  + jnp.dot(a_ref[...].astype(BF16), wa_ref[...], preferred_element_type=F32)
                  + jnp.dot(b_ref[...].astype(BF16), wb_ref[...], preferred_element_type=F32))


def out_proj(x, a, b, w, *, tm=512):
    n, d = x.shape
    ka, kb = a.shape[1], b.shape[1]
    w = w.astype(BF16)
    tm = min(tm, n)
    return pl.pallas_call(
        _out_proj_kernel,
        grid=(n // tm,),
        in_specs=[
            pl.BlockSpec((tm, d), lambda i: (i, 0)),
            pl.BlockSpec((tm, ka), lambda i: (i, 0)),
            pl.BlockSpec((tm, kb), lambda i: (i, 0)),
            pl.BlockSpec((ka, d), lambda i: (0, 0)),
            pl.BlockSpec((kb, d), lambda i: (0, 0)),
        ],
        out_specs=pl.BlockSpec((tm, d), lambda i: (i, 0)),
        out_shape=jax.ShapeDtypeStruct((n, d), F32),
        compiler_params=pltpu.CompilerParams(
            dimension_semantics=("arbitrary",), vmem_limit_bytes=VMEM_LIMIT),
        name="out_proj",
    )(x, a, b, w[:ka], w[ka:])


def _sgu_kernel(x_ref, g_ref, win_ref, vg_ref, ws_ref, bs_ref, wout_ref, o_ref, gated_ref, *, tm):
    x = x_ref[...]
    xn = _rms(x, g_ref[...]).astype(BF16)
    z = _gelu(jnp.dot(xn, win_ref[...], preferred_element_type=F32))
    u = z[:, :C_WIDTH]
    v = _rms(z[:, C_WIDTH:], vg_ref[...]).astype(BF16)
    r = lax.broadcasted_iota(jnp.int32, (C_CHUNK, C_CHUNK), 0)
    c = lax.broadcasted_iota(jnp.int32, (C_CHUNK, C_CHUNK), 1)
    causal = c <= r
    for grp in range(C_GROUPS):
        ws = jnp.where(causal, ws_ref[grp], 0.0).astype(BF16)
        bias = bs_ref[:, grp:grp + 1]
        cols = slice(grp * C_GROUP_DIM, (grp + 1) * C_GROUP_DIM)
        for ch in range(tm // C_CHUNK):
            rws = slice(ch * C_CHUNK, (ch + 1) * C_CHUNK)
            gate = jnp.dot(ws, v[rws, cols], preferred_element_type=F32) + bias
            gated_ref[rws, cols] = (u[rws, cols] * gate).astype(BF16)
    o_ref[...] = x + jnp.dot(gated_ref[...], wout_ref[...], preferred_element_type=F32)


def sgu_layer(x, norm_g, w_in, v_gain, w_spatial, b_spatial, w_out, *, tm=256):
    n, d = x.shape
    kernel = functools.partial(_sgu_kernel, tm=tm)
    return pl.pallas_call(
        kernel,
        grid=(n // tm,),
        in_specs=[
            pl.BlockSpec((tm, d), lambda i: (i, 0)),
            pl.BlockSpec((1, d), lambda i: (0, 0)),
            pl.BlockSpec((d, 2 * C_WIDTH), lambda i: (0, 0)),
            pl.BlockSpec((1, C_WIDTH), lambda i: (0, 0)),
            pl.BlockSpec((C_GROUPS, C_CHUNK, C_CHUNK), lambda i: (0, 0, 0)),
            pl.BlockSpec((C_CHUNK, C_GROUPS), lambda i: (0, 0)),
            pl.BlockSpec((C_WIDTH, d), lambda i: (0, 0)),
        ],
        out_specs=pl.BlockSpec((tm, d), lambda i: (i, 0)),
        out_shape=jax.ShapeDtypeStruct((n, d), F32),
        scratch_shapes=[pltpu.VMEM((tm, C_WIDTH), BF16)],
        compiler_params=pltpu.CompilerParams(
            dimension_semantics=("arbitrary",), vmem_limit_bytes=VMEM_LIMIT),
        name="sgu",
    )(x, norm_g.reshape(1, d), w_in.astype(BF16), v_gain.reshape(1, C_WIDTH), w_spatial,
      b_spatial.T, w_out.astype(BF16))


def _peer_scores_kernel(x_ref, g_ref, wq_ref, sk_ref, hn_ref, sc_ref):
    hn = _rms(x_ref[...], g_ref[...])
    hn_ref[...] = hn
    q = jnp.dot(hn.astype(BF16), wq_ref[...], preferred_element_type=F32)
    for hp in range(2 * PEER_HEADS):
        cols = slice(hp * PEER_NKEYS, (hp + 1) * PEER_NKEYS)
        sc_ref[cols, :] = _dot_nt(sk_ref[hp].astype(BF16), q[:, cols].astype(BF16))


def peer_scores(x, norm_g, w_query, sub_keys, *, tm=512):
    n, d = x.shape
    nq = w_query.shape[1]
    tm = min(tm, n)
    sk = sub_keys.reshape(2 * PEER_HEADS, PEER_NKEYS, PEER_NKEYS)
    return pl.pallas_call(
        _peer_scores_kernel,
        grid=(n // tm,),
        in_specs=[
            pl.BlockSpec((tm, d), lambda i: (i, 0)),
            pl.BlockSpec((1, d), lambda i: (0, 0)),
            pl.BlockSpec((d, nq), lambda i: (0, 0)),
            pl.BlockSpec(sk.shape, lambda i: (0, 0, 0)),
        ],
        out_specs=[pl.BlockSpec((tm, d), lambda i: (i, 0)),
                   pl.BlockSpec((nq, tm), lambda i: (0, i))],
        out_shape=[jax.ShapeDtypeStruct((n, d), F32), jax.ShapeDtypeStruct((nq, n), F32)],
        compiler_params=pltpu.CompilerParams(
            dimension_semantics=("arbitrary",), vmem_limit_bytes=VMEM_LIMIT),
        name="peer_scores",
    )(x, norm_g.reshape(1, d), w_query.astype(BF16), sk)


SUBLANES = 8


def _peer_topk_kernel(sc_ref, idx_ref, gate_ref):
    tt = sc_ref.shape[1]
    key = lax.broadcasted_iota(jnp.int32, (PEER_NKEYS, tt), 0).astype(F32)
    row16 = lax.broadcasted_iota(jnp.int32, (PEER_TOPK, tt), 0)
    row8 = lax.broadcasted_iota(jnp.int32, (SUBLANES, tt), 0)
    row8f = row8.astype(F32)
    ninf = jnp.float32(-jnp.inf)

    def extract16(s):
        vals = jnp.zeros((PEER_TOPK, tt), F32)
        ids = jnp.zeros((PEER_TOPK, tt), F32)
        for k in range(PEER_TOPK):
            m = jnp.max(s, axis=0, keepdims=True)
            am = jnp.min(jnp.where(s == m, key, float(PEER_NKEYS)), axis=0, keepdims=True)
            s = jnp.where(key == am, ninf, s)
            vals = jnp.where(row16 == k, m, vals)
            ids = jnp.where(row16 == k, am, ids)
        return vals, ids

    def head(h, carry):
        off = pl.multiple_of(h * 2 * PEER_NKEYS, 2 * PEER_NKEYS)
        v1, i1 = extract16(sc_ref[pl.ds(off, PEER_NKEYS), :])
        v2, i2 = extract16(sc_ref[pl.ds(off + PEER_NKEYS, PEER_NKEYS), :])
        e1 = i1 * float(PEER_NKEYS)
        cand = [v1[0:1] + v2]
        eid = [e1[0:1] + i2]
        pos = [row16.astype(F32)]
        for i in range(1, SUBLANES):
            keep = row8 < (PEER_TOPK // (i + 1))
            cand.append(jnp.where(keep, v1[i:i + 1] + v2[0:SUBLANES], ninf))
            eid.append(e1[i:i + 1] + i2[0:SUBLANES])
            pos.append(row8f + float(i * PEER_TOPK))
        cand.append(v1[SUBLANES:] + v2[0:1])
        eid.append(e1[SUBLANES:] + i2[0:1])
        pos.append((row8f + float(SUBLANES)) * float(PEER_TOPK))
        cand = jnp.concatenate(cand, axis=0)
        eid = jnp.concatenate(eid, axis=0)
        pos = jnp.concatenate(pos, axis=0)
        top = jnp.zeros((PEER_TOPK, tt), F32)
        idx = jnp.zeros((PEER_TOPK, tt), F32)
        for k in range(PEER_TOPK):
            m = jnp.max(cand, axis=0, keepdims=True)
            p = jnp.min(jnp.where(cand == m, pos, 1e9), axis=0, keepdims=True)
            hit = pos == p
            e = jnp.max(jnp.where(hit, eid, -1.0), axis=0, keepdims=True)
            cand = jnp.where(hit, ninf, cand)
            top = jnp.where(row16 == k, m, top)
            idx = jnp.where(row16 == k, e, idx)
        w = jnp.exp(top - top[0:1])
        out = pl.ds(pl.multiple_of(h * PEER_TOPK, PEER_TOPK), PEER_TOPK)
        idx_ref[out, :] = idx.astype(jnp.int32)
        gate_ref[out, :] = w / jnp.sum(w, axis=0, keepdims=True)
        return carry

    unroll = 4

    def heads(hh, carry):
        for u in range(unroll):
            head(unroll * hh + u, carry)
        return carry

    lax.fori_loop(0, PEER_HEADS // unroll, heads, 0)


def peer_topk(scores, first_token, n, *, tt=128):
    rows = scores.shape[0]
    b0 = first_token // tt
    idx_t, gate_t = pl.pallas_call(
        _peer_topk_kernel,
        grid=(n // tt,),
        in_specs=[pl.BlockSpec((rows, tt), lambda i: (0, b0 + i))],
        out_specs=[pl.BlockSpec((PEER_PICKS, tt), lambda i: (0, i)),
                   pl.BlockSpec((PEER_PICKS, tt), lambda i: (0, i))],
        out_shape=[jax.ShapeDtypeStruct((PEER_PICKS, n), jnp.int32),
                   jax.ShapeDtypeStruct((PEER_PICKS, n), F32)],
        compiler_params=pltpu.CompilerParams(
            dimension_semantics=("arbitrary",), vmem_limit_bytes=VMEM_LIMIT),
        name="peer_topk",
    )(scores)
    return idx_t.T, gate_t.T


def pack_expert_table(down, up):
    e, d = down.shape
    rows = min(PACK_ROWS, e)
    packed = pl.pallas_call(
        functools.partial(_pack_kernel, rows=rows),
        grid=(e // rows,),
        in_specs=[pl.BlockSpec((rows, d), lambda i: (i, 0)),
                  pl.BlockSpec((rows, d), lambda i: (i, 0))],
        out_specs=pl.BlockSpec(memory_space=pl.ANY),
        out_shape=jax.ShapeDtypeStruct((e, d // LANES, 1, LANES), jnp.int32),
        scratch_shapes=[pltpu.VMEM((d // LANES, rows, LANES), jnp.int32),
                        pltpu.SemaphoreType.DMA(())],
        compiler_params=pltpu.CompilerParams(
            dimension_semantics=("arbitrary",), vmem_limit_bytes=VMEM_LIMIT),
        name="pack_experts",
    )(down, up)
    return packed.reshape(e, d // LANES, LANES)


PACK_ROWS = 512


def _pack_kernel(down_ref, up_ref, out_ref, stage_ref, sem_ref, *, rows):
    step = pl.program_id(0)
    quarter = HALF // LANES

    def words(t):
        lo = pltpu.bitcast(t[:, :HALF].astype(BF16).astype(F32), jnp.uint32)
        hi = pltpu.bitcast(t[:, HALF:].astype(BF16).astype(F32), jnp.uint32)
        return pltpu.bitcast((hi & jnp.uint32(0xFFFF0000)) | (lo >> 16), jnp.int32)

    for k, ref in enumerate((down_ref, up_ref)):
        w = words(ref[...])
        for c in range(quarter):
            stage_ref[k * quarter + c] = w[:, c * LANES:(c + 1) * LANES]
    for r in range(rows):
        pltpu.make_async_copy(stage_ref.at[:, pl.ds(r, 1), :], out_ref.at[step * rows + r],
                              sem_ref).start(priority=r % 2)
    pltpu.make_async_copy(stage_ref, stage_ref, sem_ref).wait()


def _row_copy(tab_ref, buf_ref, sem_ref, expert, slot, row):
    return pltpu.make_async_copy(tab_ref.at[expert],
                                 buf_ref.at[slot, :, pl.ds(row, 1), :], sem_ref.at[slot])


APPLY_TOKENS = 8
CHUNKS = D_MODEL // LANES
STEP_TOKENS = 4 * APPLY_TOKENS


def _staged_words(st_ref, group, t):
    base = (group * APPLY_TOKENS + t) * PEER_PICKS * CHUNKS
    return [st_ref[pl.ds(base + c, PEER_PICKS, stride=CHUNKS), :] for c in range(CHUNKS)]


def _apply_groups(hn_ref, gate_ref, res_ref, o_ref, groups, between):
    tb = APPLY_TOKENS
    lane_even = (lax.broadcasted_iota(jnp.int32, (1, LANES), 1) % 2) == 0
    even2 = jnp.concatenate([lane_even, lane_even], axis=1)
    tok_of_row = lax.broadcasted_iota(jnp.int32, (2 * tb, 1), 0) % tb
    xxs, rsels, tbls = [], [], []
    for first, _ in groups:
        x = hn_ref[pl.ds(first, tb), :]
        xxs.append(jnp.concatenate([x[:, :HALF], x[:, HALF:]], axis=0).astype(BF16))
        rsels.append(jnp.zeros((2 * tb, 2 * PEER_PICKS), F32))
        tbls.append([])
    for t in range(tb):
        for g, (_, words_fn) in enumerate(groups):
            words = jnp.concatenate(words_fn(t), axis=1)
            tbl = pltpu.bitcast(words, BF16)
            tbls[g].append(tbl)
            rsels[g] = rsels[g] + jnp.where(tok_of_row == t, _dot_nt(xxs[g], tbl[:, :HALF]), 0.0)
        between(t)
    for g, (first, _) in enumerate(groups):
        tok = pl.ds(first, tb)
        coefs = []
        for c in range(2 * PEER_PICKS // LANES):
            cols = slice(c * LANES, (c + 1) * LANES)
            part = jnp.where(lane_even, rsels[g][:tb, cols], rsels[g][tb:, cols])
            hid = part + jnp.where(lane_even, pltpu.roll(part, LANES - 1, 1), pltpu.roll(part, 1, 1))
            coefs.append(gate_ref[tok, cols] * _gelu(hid))
        coef = jnp.concatenate(coefs, axis=1)
        cc = jnp.concatenate([jnp.where(even2, coef, 0.0), jnp.where(even2, 0.0, coef)],
                             axis=0).astype(BF16)
        ysel = jnp.zeros((2 * tb, HALF), F32)
        for t in range(tb):
            ysel = ysel + jnp.where(
                tok_of_row == t, jnp.dot(cc, tbls[g][t][:, HALF:], preferred_element_type=F32), 0.0)
        y = jnp.concatenate([ysel[:tb], ysel[tb:]], axis=1)
        o_ref[tok, :] = res_ref[tok, :] + y


def _peer_apply_kernel(idx0_ref, idx1_ref, hn_ref, gate_ref, res_ref, tab_ref, st_ref, o_ref,
                       buf_ref, sem_ref):
    step = pl.program_id(0)
    nsteps = pl.num_programs(0)
    tb = APPLY_TOKENS

    def issue_token(idx_ref, first_token, slot, t):
        for j in range(PEER_PICKS):
            _row_copy(tab_ref, buf_ref, sem_ref, idx_ref[first_token + t, j], slot,
                      t * PEER_PICKS + j).start(priority=j % 2)

    def wait(slot):
        pltpu.make_async_copy(buf_ref.at[slot], buf_ref.at[slot], sem_ref.at[slot]).wait()

    def copied_words(slot, t):
        return [buf_ref[slot, c, pl.ds(t * PEER_PICKS, PEER_PICKS), :] for c in range(CHUNKS)]

    compute = functools.partial(_apply_groups, hn_ref, gate_ref, res_ref, o_ref)

    @pl.when(step == 0)
    def _():
        for t in range(tb):
            issue_token(idx0_ref, 0, 0, t)

    wait(0)
    compute([(0, functools.partial(copied_words, 0)),
             (2 * tb, functools.partial(_staged_words, st_ref, 0))],
            functools.partial(issue_token, idx0_ref, tb, 1))
    wait(1)
    compute([(tb, functools.partial(copied_words, 1)),
             (3 * tb, functools.partial(_staged_words, st_ref, 1))],
            functools.partial(issue_token, idx1_ref, 0, 0))

    @pl.when(step == nsteps - 1)
    def _():
        wait(0)


def _peer_apply_staged_kernel(hn_ref, gate_ref, res_ref, st_ref, o_ref):
    tb = APPLY_TOKENS
    _apply_groups(hn_ref, gate_ref, res_ref, o_ref,
                  [(0, functools.partial(_staged_words, st_ref, 0)),
                   (tb, functools.partial(_staged_words, st_ref, 1))],
                  lambda t: None)


STAGED_BLOCK_ROWS = 2 * APPLY_TOKENS * PEER_PICKS * CHUNKS


def peer_apply(idx, hn, gate2, acc, table, staged, first_token, ntokens):
    n, d = hn.shape
    tb = STEP_TOKENS
    nsteps = ntokens // tb
    s0 = first_token // tb
    smem = pltpu.SMEM
    return pl.pallas_call(
        _peer_apply_kernel,
        grid=(nsteps,),
        in_specs=[
            pl.BlockSpec((tb, PEER_PICKS), lambda i: (s0 + i, 0), memory_space=smem),
            pl.BlockSpec((tb, PEER_PICKS), lambda i: (s0 + jnp.minimum(i + 1, nsteps - 1), 0),
                         memory_space=smem),
            pl.BlockSpec((tb, d), lambda i: (s0 + i, 0)),
            pl.BlockSpec((tb, 2 * PEER_PICKS), lambda i: (s0 + i, 0)),
            pl.BlockSpec((tb, d), lambda i: (s0 + i, 0)),
            pl.BlockSpec(memory_space=pl.ANY),
            pl.BlockSpec((STAGED_BLOCK_ROWS, LANES), lambda i: (i, 0)),
        ],
        out_specs=pl.BlockSpec((tb, d), lambda i: (s0 + i, 0)),
        out_shape=jax.ShapeDtypeStruct((n, d), F32),
        input_output_aliases={4: 0},
        scratch_shapes=[pltpu.VMEM((2, CHUNKS, APPLY_TOKENS * PEER_PICKS, LANES), jnp.int32),
                        pltpu.SemaphoreType.DMA((2,))],
        compiler_params=pltpu.CompilerParams(
            dimension_semantics=("arbitrary",), vmem_limit_bytes=VMEM_LIMIT),
        name="peer_apply",
    )(idx, idx, hn, gate2, acc, table.reshape(table.shape[0], CHUNKS, 1, LANES), staged)


def peer_apply_staged(hn, gate2, acc, staged, first_block, first_token, ntokens):
    n, d = hn.shape
    tb = 2 * APPLY_TOKENS
    s0 = first_token // tb
    return pl.pallas_call(
        _peer_apply_staged_kernel,
        grid=(ntokens // tb,),
        in_specs=[
            pl.BlockSpec((tb, d), lambda i: (s0 + i, 0)),
            pl.BlockSpec((tb, 2 * PEER_PICKS), lambda i: (s0 + i, 0)),
            pl.BlockSpec((tb, d), lambda i: (s0 + i, 0)),
            pl.BlockSpec((STAGED_BLOCK_ROWS, LANES), lambda i: (first_block + i, 0)),
        ],
        out_specs=pl.BlockSpec((tb, d), lambda i: (s0 + i, 0)),
        out_shape=jax.ShapeDtypeStruct((n, d), F32),
        input_output_aliases={2: 0},
        compiler_params=pltpu.CompilerParams(
            dimension_semantics=("arbitrary",), vmem_limit_bytes=VMEM_LIMIT),
        name="peer_apply_staged",
    )(hn, gate2, acc, staged)


SC_CORES = 2
SC_SUBCORES = 16
SC_CHUNK = 32


def sc_gather_rows(table, idx):
    b = idx.shape[0]
    nw = SC_CORES * SC_SUBCORES
    per_w = b // nw
    nchunks = per_w // SC_CHUNK
    assert per_w * nw == b and nchunks * SC_CHUNK == per_w and nchunks % 2 == 0
    mesh = plsc.VectorSubcoreMesh(core_axis_name="c", subcore_axis_name="s")

    @functools.partial(
        pl.kernel, mesh=mesh,
        out_type=jax.ShapeDtypeStruct((b,) + table.shape[1:], table.dtype),
        scratch_types=[
            pltpu.VMEM((per_w,), jnp.int32),
            pltpu.VMEM((SC_CHUNK,) + table.shape[1:], table.dtype),
            pltpu.VMEM((SC_CHUNK,) + table.shape[1:], table.dtype),
            pltpu.SemaphoreType.DMA,
            pltpu.SemaphoreType.DMA,
        ],
        name="sc_gather_rows",
    )
    def gather(tab_hbm, idx_hbm, out_hbm, idx_v, rows0, rows1, sem0, sem1):
        wid = lax.axis_index("s") * SC_CORES + lax.axis_index("c")
        base = wid * per_w
        pltpu.sync_copy(idx_hbm.at[pl.ds(base, per_w)], idx_v)

        def start(chunk, rows, sem):
            off = pl.multiple_of(chunk * SC_CHUNK, SC_CHUNK)
            pltpu.async_copy(tab_hbm.at[idx_v.at[pl.ds(off, SC_CHUNK)]], rows, sem)

        def finish(chunk, rows, sem):
            off = pl.multiple_of(chunk * SC_CHUNK, SC_CHUNK)
            pltpu.make_async_copy(tab_hbm.at[idx_v.at[pl.ds(off, SC_CHUNK)]], rows, sem).wait()
            pltpu.sync_copy(rows, out_hbm.at[pl.ds(base + off, SC_CHUNK)])

        @pl.loop(0, nchunks, step=2)
        def _(i):
            start(i, rows0, sem0)
            finish(i, rows0, sem0)
            start(i + 1, rows1, sem1)
            finish(i + 1, rows1, sem1)

    return gather(table, idx)


PEER_GROUP = 4096
PEER_GROUP_MIXED = 4096


def peer_layer(x, norm_g, w_query, sub_keys, table):
    n = x.shape[0]
    hn, scores = peer_scores(x, norm_g, w_query, sub_keys)
    tg = min(PEER_GROUP, n)
    tm = tg * PEER_GROUP_MIXED // PEER_GROUP
    def staged_indices(src, t0):
        mixed = src[t0:t0 + tm].reshape(tm // STEP_TOKENS, STEP_TOKENS, PEER_PICKS)
        return jnp.concatenate([mixed[:, STEP_TOKENS // 2:].reshape(-1),
                                src[t0 + tm:t0 + tg].reshape(-1)])

    idx0, gate0 = peer_topk(scores, 0, tg)
    first = staged_indices(idx0, 0)
    if n > tg:
        scores, first = lax.optimization_barrier((scores, first))
        idx1, gate1 = peer_topk(scores, tg, n - tg)
        idx, gate = jnp.concatenate([idx0, idx1]), jnp.concatenate([gate0, gate1])
    else:
        idx, gate = idx0, gate0
    gate2 = jnp.repeat(gate, 2, axis=1)

    acc = x
    ngroups = n // tg
    pending = [first] + [staged_indices(idx, g * tg) for g in range(1, min(2, ngroups))]
    for g in range(ngroups):
        t0 = g * tg
        staged = sc_gather_rows(table, pending.pop(0)).reshape(-1, LANES)
        acc = peer_apply(idx, hn, gate2, acc, table, staged, t0, tm)
        if tg > tm:
            acc = peer_apply_staged(hn, gate2, acc, staged, tm // STEP_TOKENS, t0 + tm, tg - tm)
        if g + 2 < ngroups:
            nxt, acc = lax.optimization_barrier((staged_indices(idx, (g + 2) * tg), acc))
            pending.append(nxt)
    return acc


def attention_layer(x, batch, seq, norm_g, w_in, a_q_gain, a_k_gain, b_q_gain, b_k_gain,
                    lam_q1, lam_k1, lam_q2, lam_k2, b_sub_gain, w_out, lambda_init):
    proj = norm_matmul(x, norm_g, w_in)
    na = A_HEADS // 2
    tile2 = lambda g: jnp.concatenate([g, g])
    out_a = pair_attention(proj, batch, seq, "dilated", 0, na, 2 * na, na,
                           tile2(a_q_gain), tile2(a_k_gain))
    out_b = pair_attention(proj, batch, seq, "diff", 3 * na, 3 * na + B_HEADS, 3 * na + 2 * B_HEADS,
                           B_HEADS, b_q_gain.reshape(-1), b_k_gain.reshape(-1),
                           extras=(lam_q1, lam_k1, lam_q2, lam_k2, b_sub_gain),
                           lambda_init=lambda_init)
    return out_proj(x, out_a, out_b, w_out)


def kernel(x, attn_norm_g, attn_w_in, a_q_gain, a_k_gain, b_q_gain, b_k_gain, lam_q1, lam_k1,
           lam_q2, lam_k2, b_sub_gain, attn_w_out, sgu_norm_g, sgu_w_in, sgu_v_gain, sgu_w_spatial,
           sgu_b_spatial, sgu_w_out, ffn_norm_g, peer_w_query, peer_sub_keys, peer_down, peer_up):
    batch, seq, d = x.shape
    depth = ffn_norm_g.shape[0]
    h = x.reshape(batch * seq, d)
    tables = [pack_expert_table(peer_down[layer], peer_up[layer]) for layer in range(depth)]
    for layer in range(depth):
        i = layer // 2
        if layer % 2 == 0:
            lambda_init = 0.8 - 0.6 * math.exp(-0.3 * layer)
            h = attention_layer(h, batch, seq, attn_norm_g[i], attn_w_in[i], a_q_gain[i],
                                a_k_gain[i], b_q_gain[i], b_k_gain[i], lam_q1[i], lam_k1[i],
                                lam_q2[i], lam_k2[i], b_sub_gain[i], attn_w_out[i], lambda_init)
        else:
            h = sgu_layer(h, sgu_norm_g[i], sgu_w_in[i], sgu_v_gain[i], sgu_w_spatial[i],
                          sgu_b_spatial[i], sgu_w_out[i])
        h = peer_layer(h, ffn_norm_g[layer], peer_w_query[layer], peer_sub_keys[layer],
                       tables[layer])
    return h.reshape(batch, seq, d)
```

```python
import functools
import math

import jax
import jax.numpy as jnp
from jax import lax
from jax.experimental import pallas as pl
from jax.experimental.pallas import tpu as pltpu
from jax.experimental.pallas import tpu_sc as plsc

D_MODEL = 1024
HEAD_DIM = 64
EPS = 1e-6
NEG = -1e30
A_HEADS = 8
B_HEADS = 4
C_CHUNK = 128
C_GROUPS = 8
C_WIDTH = 2 * D_MODEL
C_GROUP_DIM = C_WIDTH // C_GROUPS
PEER_HEADS = 8
PEER_NKEYS = 128
PEER_TOPK = 16
PEER_PICKS = PEER_HEADS * PEER_TOPK

LANES = 128
HALF = D_MODEL // 2
VMEM_LIMIT = 56 * 1024 * 1024

BF16 = jnp.bfloat16
F32 = jnp.float32


def _gelu(x):
    return 0.5 * x * (1.0 + jnp.tanh(math.sqrt(2.0 / math.pi) * (x + 0.044715 * (x * x * x))))


def _rms(x, g):
    return x * lax.rsqrt(jnp.mean(x * x, axis=-1, keepdims=True) + EPS) * g


def _dot_nt(a, b):
    return lax.dot_general(a, b, (((1,), (1,)), ((), ())), preferred_element_type=F32)


def _norm_matmul_kernel(x_ref, g_ref, w_ref, o_ref, xn_ref):
    @pl.when(pl.program_id(1) == 0)
    def _():
        xn_ref[...] = _rms(x_ref[...], g_ref[...]).astype(BF16)

    o_ref[...] = jnp.dot(xn_ref[...], w_ref[...], preferred_element_type=F32)


def norm_matmul(x, g, w, *, tm=1024, tn=1024):
    n, d = x.shape
    nout = w.shape[1]
    tm = min(tm, n)
    return pl.pallas_call(
        _norm_matmul_kernel,
        grid=(n // tm, nout // tn),
        in_specs=[
            pl.BlockSpec((tm, d), lambda i, j: (i, 0)),
            pl.BlockSpec((1, d), lambda i, j: (0, 0)),
            pl.BlockSpec((d, tn), lambda i, j: (0, j)),
        ],
        out_specs=pl.BlockSpec((tm, tn), lambda i, j: (i, j)),
        out_shape=jax.ShapeDtypeStruct((n, nout), F32),
        scratch_shapes=[pltpu.VMEM((tm, d), BF16)],
        compiler_params=pltpu.CompilerParams(
            dimension_semantics=("arbitrary", "arbitrary"), vmem_limit_bytes=VMEM_LIMIT),
        name="norm_matmul",
    )(x, g.reshape(1, d), w.astype(BF16))


def _pair_norm(t, gain, lo):
    sq = t * t
    s_lo = jnp.sum(jnp.where(lo, sq, 0.0), axis=-1, keepdims=True)
    s_hi = jnp.sum(jnp.where(lo, 0.0, sq), axis=-1, keepdims=True)
    ms = jnp.where(lo, s_lo, s_hi) * (1.0 / HEAD_DIM)
    return t * lax.rsqrt(ms + EPS) * gain


def _attn_kernel(*refs, mode, tq, tk, lambda_init):
    if mode == "dilated":
        q_ref, k_ref, v_ref, qg_ref, kg_ref, o_ref, kn_ref = refs
    else:
        (q_ref, k_ref, v_ref, qg_ref, kg_ref, lq1_ref, lk1_ref, lq2_ref, lk2_ref, sg_ref,
         o_ref, kn_ref) = refs
    i = pl.program_id(2)
    lo = lax.broadcasted_iota(jnp.int32, (1, LANES), 1) < HEAD_DIM

    @pl.when(i == 0)
    def _():
        kn_ref[...] = _pair_norm(k_ref[...], kg_ref[...], lo).astype(BF16)

    qn = _pair_norm(q_ref[...], qg_ref[...], lo) * (HEAD_DIM ** -0.5)
    qa = jnp.where(lo, qn, 0.0).astype(BF16)
    qb = jnp.where(lo, 0.0, qn).astype(BF16)
    assert tq == tk and tk % 16 == 0
    row = lax.broadcasted_iota(jnp.int32, (tq, tk), 0)
    col = lax.broadcasted_iota(jnp.int32, (tq, tk), 1)

    def step(j, carry, valid, weight):
        ma, la, acca, mb, lb, accb = carry
        off = pl.multiple_of(j * tk, tk)
        kb = kn_ref[pl.ds(off, tk), :]
        vb = v_ref[pl.ds(off, tk), :].astype(BF16)

        def update(qh, m, l, acc):
            s = _dot_nt(qh, kb)
            if valid is not None:
                s = jnp.where(valid, s, NEG)
            m_new = jnp.maximum(m, jnp.max(s, axis=-1, keepdims=True))
            alpha = jnp.exp(m - m_new)
            p = jnp.exp(s - m_new)
            if weight is not None:
                p = weight * p
            l_new = alpha * l + jnp.sum(p, axis=-1, keepdims=True)
            acc_new = alpha * acc + jnp.dot(p.astype(BF16), vb, preferred_element_type=F32)
            return m_new, l_new, acc_new

        ma, la, acca = update(qa, ma, la, acca)
        mb, lb, accb = update(qb, mb, lb, accb)
        return ma, la, acca, mb, lb, accb

    m0 = jnp.full((tq, 1), NEG, F32)
    l0 = jnp.zeros((tq, 1), F32)
    a0 = jnp.zeros((tq, LANES), F32)
    carry = (m0, l0, a0, m0, l0, a0)
    if mode == "dilated":
        near = (512 + tk - 1) // tk + 1
        first_near = jnp.maximum(i - (near - 1), 0)
        far_valid = ((row - col) & 15) == 0
        carry = lax.fori_loop(0, first_near, lambda j, c: step(j, c, far_valid, None), carry)

        def near_step(j, c):
            dist = (i - j) * tk + row - col
            cnt = ((dist <= 128).astype(F32)
                   + ((dist <= 512) & ((dist & 3) == 0)).astype(F32)
                   + ((dist & 15) == 0).astype(F32))
            cnt = jnp.where(dist >= 0, cnt, 0.0)
            return step(j, c, cnt > 0.0, cnt)

        carry = lax.fori_loop(first_near, i + 1, near_step, carry)
    else:
        carry = lax.fori_loop(0, i, lambda j, c: step(j, c, None, None), carry)
        carry = step(i, carry, col <= row, None)
    ma, la, acca, mb, lb, accb = carry
    oa = acca / la
    ob = accb / lb
    if mode == "dilated":
        o_ref[...] = jnp.where(lo, oa, ob)
    else:
        lam = (jnp.exp(jnp.sum(lq1_ref[...] * lk1_ref[...], axis=-1, keepdims=True))
               - jnp.exp(jnp.sum(lq2_ref[...] * lk2_ref[...], axis=-1, keepdims=True))
               + lambda_init)
        o = oa - lam * ob
        o_ref[...] = _rms(o, sg_ref[...]) * (1.0 - lambda_init)


def pair_attention(proj, batch, seq, mode, qcol, kcol, vcol, npairs, q_gain, k_gain, extras=(),
                   lambda_init=0.0, tq=512, tk=512):
    nq = seq // tq
    small = [q_gain.reshape(1, LANES), k_gain.reshape(1, LANES)] + [e.reshape(1, -1) for e in extras]
    small_specs = [pl.BlockSpec(s.shape, lambda b, p, i: (0, 0)) for s in small]
    kernel = functools.partial(_attn_kernel, mode=mode, tq=tq, tk=tk, lambda_init=lambda_init)
    return pl.pallas_call(
        kernel,
        grid=(batch, npairs, nq),
        in_specs=[
            pl.BlockSpec((tq, LANES), lambda b, p, i: (b * nq + i, qcol + p)),
            pl.BlockSpec((seq, LANES), lambda b, p, i: (b, kcol + p)),
            pl.BlockSpec((seq, LANES), lambda b, p, i: (b, vcol + p)),
        ] + small_specs,
        out_specs=pl.BlockSpec((tq, LANES), lambda b, p, i: (b * nq + i, p)),
        out_shape=jax.ShapeDtypeStruct((batch * seq, npairs * LANES), F32),
        scratch_shapes=[pltpu.VMEM((seq, LANES), BF16)],
        compiler_params=pltpu.CompilerParams(
            dimension_semantics=("arbitrary", "arbitrary", "arbitrary"),
            vmem_limit_bytes=VMEM_LIMIT),
        name="attn_" + mode,
    )(proj, proj, proj, *small)


def _out_proj_kernel(x_ref, a_ref, b_ref, wa_ref, wb_ref, o_ref):
    o_ref[...] = (x_ref[...]
                  + jnp.dot(a_ref[...].astype(BF16), wa_ref[...], preferred_element_type=F32)
                  + jnp.dot(b_ref[...].astype(BF16), wb_ref[...], preferred_element_type=F32))


def out_proj(x, a, b, w, *, tm=512):
    n, d = x.shape
    ka, kb = a.shape[1], b.shape[1]
    w = w.astype(BF16)
    tm = min(tm, n)
    return pl.pallas_call(
        _out_proj_kernel,
        grid=(n // tm,),
        in_specs=[
            pl.BlockSpec((tm, d), lambda i: (i, 0)),
            pl.BlockSpec((tm, ka), lambda i: (i, 0)),
            pl.BlockSpec((tm, kb), lambda i: (i, 0)),
            pl.BlockSpec((ka, d), lambda i: (0, 0)),
            pl.BlockSpec((kb, d), lambda i: (0, 0)),
        ],
        out_specs=pl.BlockSpec((tm, d), lambda i: (i, 0)),
        out_shape=jax.ShapeDtypeStruct((n, d), F32),
        compiler_params=pltpu.CompilerParams(
            dimension_semantics=("arbitrary",), vmem_limit_bytes=VMEM_LIMIT),
        name="out_proj",
    )(x, a, b, w[:ka], w[ka:])


def _sgu_kernel(x_ref, g_ref, win_ref, vg_ref, ws_ref, bs_ref, wout_ref, o_ref, gated_ref, *, tm):
    x = x_ref[...]
    xn = _rms(x, g_ref[...]).astype(BF16)
    z = _gelu(jnp.dot(xn, win_ref[...], preferred_element_type=F32))
    u = z[:, :C_WIDTH]
    v = _rms(z[:, C_WIDTH:], vg_ref[...]).astype(BF16)
    r = lax.broadcasted_iota(jnp.int32, (C_CHUNK, C_CHUNK), 0)
    c = lax.broadcasted_iota(jnp.int32, (C_CHUNK, C_CHUNK), 1)
    causal = c <= r
    for grp in range(C_GROUPS):
        ws = jnp.where(causal, ws_ref[grp], 0.0).astype(BF16)
        bias = bs_ref[:, grp:grp + 1]
        cols = slice(grp * C_GROUP_DIM, (grp + 1) * C_GROUP_DIM)
        for ch in range(tm // C_CHUNK):
            rws = slice(ch * C_CHUNK, (ch + 1) * C_CHUNK)
            gate = jnp.dot(ws, v[rws, cols], preferred_element_type=F32) + bias
            gated_ref[rws, cols] = (u[rws, cols] * gate).astype(BF16)
    o_ref[...] = x + jnp.dot(gated_ref[...], wout_ref[...], preferred_element_type=F32)


def sgu_layer(x, norm_g, w_in, v_gain, w_spatial, b_spatial, w_out, *, tm=256):
    n, d = x.shape
    kernel = functools.partial(_sgu_kernel, tm=tm)
    return pl.pallas_call(
        kernel,
        grid=(n // tm,),
        in_specs=[
            pl.BlockSpec((tm, d), lambda i: (i, 0)),
            pl.BlockSpec((1, d), lambda i: (0, 0)),
            pl.BlockSpec((d, 2 * C_WIDTH), lambda i: (0, 0)),
            pl.BlockSpec((1, C_WIDTH), lambda i: (0, 0)),
            pl.BlockSpec((C_GROUPS, C_CHUNK, C_CHUNK), lambda i: (0, 0, 0)),
            pl.BlockSpec((C_CHUNK, C_GROUPS), lambda i: (0, 0)),
            pl.BlockSpec((C_WIDTH, d), lambda i: (0, 0)),
        ],
        out_specs=pl.BlockSpec((tm, d), lambda i: (i, 0)),
        out_shape=jax.ShapeDtypeStruct((n, d), F32),
        scratch_shapes=[pltpu.VMEM((tm, C_WIDTH), BF16)],
        compiler_params=pltpu.CompilerParams(
            dimension_semantics=("arbitrary",), vmem_limit_bytes=VMEM_LIMIT),
        name="sgu",
    )(x, norm_g.reshape(1, d), w_in.astype(BF16), v_gain.reshape(1, C_WIDTH), w_spatial,
      b_spatial.T, w_out.astype(BF16))


def _peer_scores_kernel(x_ref, g_ref, wq_ref, sk_ref, hn_ref, sc_ref):
    hn = _rms(x_ref[...], g_ref[...])
    hn_ref[...] = hn
    q = jnp.dot(hn.astype(BF16), wq_ref[...], preferred_element_type=F32)
    for hp in range(2 * PEER_HEADS):
        cols = slice(hp * PEER_NKEYS, (hp + 1) * PEER_NKEYS)
        sc_ref[cols, :] = _dot_nt(sk_ref[hp].astype(BF16), q[:, cols].astype(BF16))


def peer_scores(x, norm_g, w_query, sub_keys, *, tm=512):
    n, d = x.shape
    nq = w_query.shape[1]
    tm = min(tm, n)
    sk = sub_keys.reshape(2 * PEER_HEADS, PEER_NKEYS, PEER_NKEYS)
    return pl.pallas_call(
        _peer_scores_kernel,
        grid=(n // tm,),
        in_specs=[
            pl.BlockSpec((tm, d), lambda i: (i, 0)),
            pl.BlockSpec((1, d), lambda i: (0, 0)),
            pl.BlockSpec((d, nq), lambda i: (0, 0)),
            pl.BlockSpec(sk.shape, lambda i: (0, 0, 0)),
        ],
        out_specs=[pl.BlockSpec((tm, d), lambda i: (i, 0)),
                   pl.BlockSpec((nq, tm), lambda i: (0, i))],
        out_shape=[jax.ShapeDtypeStruct((n, d), F32), jax.ShapeDtypeStruct((nq, n), F32)],
        compiler_params=pltpu.CompilerParams(
            dimension_semantics=("arbitrary",), vmem_limit_bytes=VMEM_LIMIT),
        name="peer_scores",
    )(x, norm_g.reshape(1, d), w_query.astype(BF16), sk)


SUBLANES = 8


def _peer_topk_kernel(sc_ref, idx_ref, gate_ref):
    tt = sc_ref.shape[1]
    key = lax.broadcasted_iota(jnp.int32, (PEER_NKEYS, tt), 0).astype(F32)
    row16 = lax.broadcasted_iota(jnp.int32, (PEER_TOPK, tt), 0)
    row8 = lax.broadcasted_iota(jnp.int32, (SUBLANES, tt), 0)
    row8f = row8.astype(F32)
    ninf = jnp.float32(-jnp.inf)

    def extract16(s):
        vals = jnp.zeros((PEER_TOPK, tt), F32)
        ids = jnp.zeros((PEER_TOPK, tt), F32)
        for k in range(PEER_TOPK):
            m = jnp.max(s, axis=0, keepdims=True)
            am = jnp.min(jnp.where(s == m, key, float(PEER_NKEYS)), axis=0, keepdims=True)
            s = jnp.where(key == am, ninf, s)
            vals = jnp.where(row16 == k, m, vals)
            ids = jnp.where(row16 == k, am, ids)
        return vals, ids

    def head(h, carry):
        off = pl.multiple_of(h * 2 * PEER_NKEYS, 2 * PEER_NKEYS)
        v1, i1 = extract16(sc_ref[pl.ds(off, PEER_NKEYS), :])
        v2, i2 = extract16(sc_ref[pl.ds(off + PEER_NKEYS, PEER_NKEYS), :])
        e1 = i1 * float(PEER_NKEYS)
        cand = [v1[0:1] + v2]
        eid = [e1[0:1] + i2]
        pos = [row16.astype(F32)]
        for i in range(1, SUBLANES):
            keep = row8 < (PEER_TOPK // (i + 1))
            cand.append(jnp.where(keep, v1[i:i + 1] + v2[0:SUBLANES], ninf))
            eid.append(e1[i:i + 1] + i2[0:SUBLANES])
            pos.append(row8f + float(i * PEER_TOPK))
        cand.append(v1[SUBLANES:] + v2[0:1])
        eid.append(e1[SUBLANES:] + i2[0:1])
        pos.append((row8f + float(SUBLANES)) * float(PEER_TOPK))
        cand = jnp.concatenate(cand, axis=0)
        eid = jnp.concatenate(eid, axis=0)
        pos = jnp.concatenate(pos, axis=0)
        top = jnp.zeros((PEER_TOPK, tt), F32)
        idx = jnp.zeros((PEER_TOPK, tt), F32)
        for k in range(PEER_TOPK):
            m = jnp.max(cand, axis=0, keepdims=True)
            p = jnp.min(jnp.where(cand == m, pos, 1e9), axis=0, keepdims=True)
            hit = pos == p
            e = jnp.max(jnp.where(hit, eid, -1.0), axis=0, keepdims=True)
            cand = jnp.where(hit, ninf, cand)
            top = jnp.where(row16 == k, m, top)
            idx = jnp.where(row16 == k, e, idx)
        w = jnp.exp(top - top[0:1])
        out = pl.ds(pl.multiple_of(h * PEER_TOPK, PEER_TOPK), PEER_TOPK)
        idx_ref[out, :] = idx.astype(jnp.int32)
        gate_ref[out, :] = w / jnp.sum(w, axis=0, keepdims=True)
        return carry

    unroll = 4

    def heads(hh, carry):
        for u in range(unroll):
            head(unroll * hh + u, carry)
        return carry

    lax.fori_loop(0, PEER_HEADS // unroll, heads, 0)


def peer_topk(scores, first_token, n, *, tt=128):
    rows = scores.shape[0]
    b0 = first_token // tt
    idx_t, gate_t = pl.pallas_call(
        _peer_topk_kernel,
        grid=(n // tt,),
        in_specs=[pl.BlockSpec((rows, tt), lambda i: (0, b0 + i))],
        out_specs=[pl.BlockSpec((PEER_PICKS, tt), lambda i: (0, i)),
                   pl.BlockSpec((PEER_PICKS, tt), lambda i: (0, i))],
        out_shape=[jax.ShapeDtypeStruct((PEER_PICKS, n), jnp.int32),
                   jax.ShapeDtypeStruct((PEER_PICKS, n), F32)],
        compiler_params=pltpu.CompilerParams(
            dimension_semantics=("arbitrary",), vmem_limit_bytes=VMEM_LIMIT),
        name="peer_topk",
    )(scores)
    return idx_t.T, gate_t.T


def pack_expert_table(down, up):
    e, d = down.shape
    rows = min(PACK_ROWS, e)
    packed = pl.pallas_call(
        functools.partial(_pack_kernel, rows=rows),
        grid=(e // rows,),
        in_specs=[pl.BlockSpec((rows, d), lambda i: (i, 0)),
                  pl.BlockSpec((rows, d), lambda i: (i, 0))],
        out_specs=pl.BlockSpec(memory_space=pl.ANY),
        out_shape=jax.ShapeDtypeStruct((e, d // LANES, 1, LANES), jnp.int32),
        scratch_shapes=[pltpu.VMEM((d // LANES, rows, LANES), jnp.int32),
                        pltpu.SemaphoreType.DMA(())],
        compiler_params=pltpu.CompilerParams(
            dimension_semantics=("arbitrary",), vmem_limit_bytes=VMEM_LIMIT),
        name="pack_experts",
    )(down, up)
    return packed.reshape(e, d // LANES, LANES)


PACK_ROWS = 512


def _pack_kernel(down_ref, up_ref, out_ref, stage_ref, sem_ref, *, rows):
    step = pl.program_id(0)
    quarter = HALF // LANES

    def words(t):
        lo = pltpu.bitcast(t[:, :HALF].astype(BF16).astype(F32), jnp.uint32)
        hi = pltpu.bitcast(t[:, HALF:].astype(BF16).astype(F32), jnp.uint32)
        return pltpu.bitcast((hi & jnp.uint32(0xFFFF0000)) | (lo >> 16), jnp.int32)

    for k, ref in enumerate((down_ref, up_ref)):
        w = words(ref[...])
        for c in range(quarter):
            stage_ref[k * quarter + c] = w[:, c * LANES:(c + 1) * LANES]
    for r in range(rows):
        pltpu.make_async_copy(stage_ref.at[:, pl.ds(r, 1), :], out_ref.at[step * rows + r],
                              sem_ref).start(priority=r % 2)
    pltpu.make_async_copy(stage_ref, stage_ref, sem_ref).wait()


def _row_copy(tab_ref, buf_ref, sem_ref, expert, slot, row):
    return pltpu.make_async_copy(tab_ref.at[expert],
                                 buf_ref.at[slot, :, pl.ds(row, 1), :], sem_ref.at[slot])


APPLY_TOKENS = 8
CHUNKS = D_MODEL // LANES
STEP_TOKENS = 4 * APPLY_TOKENS


def _staged_words(st_ref, group, t):
    base = (group * APPLY_TOKENS + t) * PEER_PICKS * CHUNKS
    return [st_ref[pl.ds(base + c, PEER_PICKS, stride=CHUNKS), :] for c in range(CHUNKS)]


def _apply_groups(hn_ref, gate_ref, res_ref, o_ref, groups, between):
    tb = APPLY_TOKENS
    lane_even = (lax.broadcasted_iota(jnp.int32, (1, LANES), 1) % 2) == 0
    even2 = jnp.concatenate([lane_even, lane_even], axis=1)
    tok_of_row = lax.broadcasted_iota(jnp.int32, (2 * tb, 1), 0) % tb
    xxs, rsels, tbls = [], [], []
    for first, _ in groups:
        x = hn_ref[pl.ds(first, tb), :]
        xxs.append(jnp.concatenate([x[:, :HALF], x[:, HALF:]], axis=0).astype(BF16))
        rsels.append(jnp.zeros((2 * tb, 2 * PEER_PICKS), F32))
        tbls.append([])
    for t in range(tb):
        for g, (_, words_fn) in enumerate(groups):
            words = jnp.concatenate(words_fn(t), axis=1)
            tbl = pltpu.bitcast(words, BF16)
            tbls[g].append(tbl)
            rsels[g] = rsels[g] + jnp.where(tok_of_row == t, _dot_nt(xxs[g], tbl[:, :HALF]), 0.0)
        between(t)
    for g, (first, _) in enumerate(groups):
        tok = pl.ds(first, tb)
        coefs = []
        for c in range(2 * PEER_PICKS // LANES):
            cols = slice(c * LANES, (c + 1) * LANES)
            part = jnp.where(lane_even, rsels[g][:tb, cols], rsels[g][tb:, cols])
            hid = part + jnp.where(lane_even, pltpu.roll(part, LANES - 1, 1), pltpu.roll(part, 1, 1))
            coefs.append(gate_ref[tok, cols] * _gelu(hid))
        coef = jnp.concatenate(coefs, axis=1)
        cc = jnp.concatenate([jnp.where(even2, coef, 0.0), jnp.where(even2, 0.0, coef)],
                             axis=0).astype(BF16)
        ysel = jnp.zeros((2 * tb, HALF), F32)
        for t in range(tb):
            ysel = ysel + jnp.where(
                tok_of_row == t, jnp.dot(cc, tbls[g][t][:, HALF:], preferred_element_type=F32), 0.0)
        y = jnp.concatenate([ysel[:tb], ysel[tb:]], axis=1)
        o_ref[tok, :] = res_ref[tok, :] + y


def _peer_apply_kernel(idx0_ref, idx1_ref, hn_ref, gate_ref, res_ref, tab_ref, st_ref, o_ref,
                       buf_ref, sem_ref):
    step = pl.program_id(0)
    nsteps = pl.num_programs(0)
    tb = APPLY_TOKENS

    def issue_token(idx_ref, first_token, slot, t):
        for j in range(PEER_PICKS):
            _row_copy(tab_ref, buf_ref, sem_ref, idx_ref[first_token + t, j], slot,
                      t * PEER_PICKS + j).start(priority=j % 2)

    def wait(slot):
        pltpu.make_async_copy(buf_ref.at[slot], buf_ref.at[slot], sem_ref.at[slot]).wait()

    def copied_words(slot, t):
        return [buf_ref[slot, c, pl.ds(t * PEER_PICKS, PEER_PICKS), :] for c in range(CHUNKS)]

    compute = functools.partial(_apply_groups, hn_ref, gate_ref, res_ref, o_ref)

    @pl.when(step == 0)
    def _():
        for t in range(tb):
            issue_token(idx0_ref, 0, 0, t)

    wait(0)
    compute([(0, functools.partial(copied_words, 0)),
             (2 * tb, functools.partial(_staged_words, st_ref, 0))],
            functools.partial(issue_token, idx0_ref, tb, 1))
    wait(1)
    compute([(tb, functools.partial(copied_words, 1)),
             (3 * tb, functools.partial(_staged_words, st_ref, 1))],
            functools.partial(issue_token, idx1_ref, 0, 0))

    @pl.when(step == nsteps - 1)
    def _():
        wait(0)


STAGED_BLOCK_ROWS = 2 * APPLY_TOKENS * PEER_PICKS * CHUNKS


def peer_apply(idx, hn, gate2, acc, table, staged, first_token, ntokens):
    n, d = hn.shape
    tb = STEP_TOKENS
    nsteps = ntokens // tb
    s0 = first_token // tb
    smem = pltpu.SMEM
    return pl.pallas_call(
        _peer_apply_kernel,
        grid=(nsteps,),
        in_specs=[
            pl.BlockSpec((tb, PEER_PICKS), lambda i: (s0 + i, 0), memory_space=smem),
            pl.BlockSpec((tb, PEER_PICKS), lambda i: (s0 + jnp.minimum(i + 1, nsteps - 1), 0),
                         memory_space=smem),
            pl.BlockSpec((tb, d), lambda i: (s0 + i, 0)),
            pl.BlockSpec((tb, 2 * PEER_PICKS), lambda i: (s0 + i, 0)),
            pl.BlockSpec((tb, d), lambda i: (s0 + i, 0)),
            pl.BlockSpec(memory_space=pl.ANY),
            pl.BlockSpec((STAGED_BLOCK_ROWS, LANES), lambda i: (i, 0)),
        ],
        out_specs=pl.BlockSpec((tb, d), lambda i: (s0 + i, 0)),
        out_shape=jax.ShapeDtypeStruct((n, d), F32),
        input_output_aliases={4: 0},
        scratch_shapes=[pltpu.VMEM((2, CHUNKS, APPLY_TOKENS * PEER_PICKS, LANES), jnp.int32),
                        pltpu.SemaphoreType.DMA((2,))],
        compiler_params=pltpu.CompilerParams(
            dimension_semantics=("arbitrary",), vmem_limit_bytes=VMEM_LIMIT),
        name="peer_apply",
    )(idx, idx, hn, gate2, acc, table.reshape(table.shape[0], CHUNKS, 1, LANES), staged)


SC_CORES = 2
SC_SUBCORES = 16
SC_CHUNK = 32


def sc_gather_rows(table, idx):
    b = idx.shape[0]
    nw = SC_CORES * SC_SUBCORES
    per_w = b // nw
    nchunks = per_w // SC_CHUNK
    assert per_w * nw == b and nchunks * SC_CHUNK == per_w and nchunks % 2 == 0
    mesh = plsc.VectorSubcoreMesh(core_axis_name="c", subcore_axis_name="s")

    @functools.partial(
        pl.kernel, mesh=mesh,
        out_type=jax.ShapeDtypeStruct((b,) + table.shape[1:], table.dtype),
        scratch_types=[
            pltpu.VMEM((per_w,), jnp.int32),
            pltpu.VMEM((SC_CHUNK,) + table.shape[1:], table.dtype),
            pltpu.VMEM((SC_CHUNK,) + table.shape[1:], table.dtype),
            pltpu.SemaphoreType.DMA,
            pltpu.SemaphoreType.DMA,
        ],
        name="sc_gather_rows",
    )
    def gather(tab_hbm, idx_hbm, out_hbm, idx_v, rows0, rows1, sem0, sem1):
        wid = lax.axis_index("s") * SC_CORES + lax.axis_index("c")
        base = wid * per_w
        pltpu.sync_copy(idx_hbm.at[pl.ds(base, per_w)], idx_v)

        def start(chunk, rows, sem):
            off = pl.multiple_of(chunk * SC_CHUNK, SC_CHUNK)
            pltpu.async_copy(tab_hbm.at[idx_v.at[pl.ds(off, SC_CHUNK)]], rows, sem)

        def finish(chunk, rows, sem):
            off = pl.multiple_of(chunk * SC_CHUNK, SC_CHUNK)
            pltpu.make_async_copy(tab_hbm.at[idx_v.at[pl.ds(off, SC_CHUNK)]], rows, sem).wait()
            pltpu.sync_copy(rows, out_hbm.at[pl.ds(base + off, SC_CHUNK)])

        @pl.loop(0, nchunks, step=2)
        def _(i):
            start(i, rows0, sem0)
            finish(i, rows0, sem0)
            start(i + 1, rows1, sem1)
            finish(i + 1, rows1, sem1)

    return gather(table, idx)


PEER_GROUP = 4096
PEER_FIRST_GROUP = 2048


def _group_bounds(n):
    first = min(PEER_FIRST_GROUP, n)
    bounds = [0, first] + ([2 * first] if n >= 2 * first else [])
    while bounds[-1] < n:
        bounds.append(min(bounds[-1] + PEER_GROUP, n))
    return bounds


def peer_layer(x, norm_g, w_query, sub_keys, table):
    n = x.shape[0]
    hn, scores = peer_scores(x, norm_g, w_query, sub_keys)
    bounds = _group_bounds(n)
    ngroups = len(bounds) - 1

    def staged_indices(src, lo, hi):
        steps = src[lo:hi].reshape((hi - lo) // STEP_TOKENS, STEP_TOKENS, PEER_PICKS)
        return steps[:, STEP_TOKENS // 2:].reshape(-1)

    idx0, gate0 = peer_topk(scores, 0, bounds[1])
    first = staged_indices(idx0, 0, bounds[1])
    if ngroups > 1:
        scores, first = lax.optimization_barrier((scores, first))
        idx1, gate1 = peer_topk(scores, bounds[1], n - bounds[1])
        idx, gate = jnp.concatenate([idx0, idx1]), jnp.concatenate([gate0, gate1])
    else:
        idx, gate = idx0, gate0
    gate2 = jnp.repeat(gate, 2, axis=1)

    acc = x
    pending = [first] + [staged_indices(idx, bounds[g], bounds[g + 1]) for g in range(1, min(2, ngroups))]
    for g in range(ngroups):
        staged = sc_gather_rows(table, pending.pop(0)).reshape(-1, LANES)
        acc = peer_apply(idx, hn, gate2, acc, table, staged, bounds[g], bounds[g + 1] - bounds[g])
        if g + 2 < ngroups:
            nxt, acc = lax.optimization_barrier(
                (staged_indices(idx, bounds[g + 2], bounds[g + 3]), acc))
            pending.append(nxt)
    return acc


def attention_layer(x, batch, seq, norm_g, w_in, a_q_gain, a_k_gain, b_q_gain, b_k_gain,
                    lam_q1, lam_k1, lam_q2, lam_k2, b_sub_gain, w_out, lambda_init):
    proj = norm_matmul(x, norm_g, w_in)
    na = A_HEADS // 2
    tile2 = lambda g: jnp.concatenate([g, g])
    out_a = pair_attention(proj, batch, seq, "dilated", 0, na, 2 * na, na,
                           tile2(a_q_gain), tile2(a_k_gain))
    out_b = pair_attention(proj, batch, seq, "diff", 3 * na, 3 * na + B_HEADS, 3 * na + 2 * B_HEADS,
                           B_HEADS, b_q_gain.reshape(-1), b_k_gain.reshape(-1),
                           extras=(lam_q1, lam_k1, lam_q2, lam_k2, b_sub_gain),
                           lambda_init=lambda_init)
    return out_proj(x, out_a, out_b, w_out)


def kernel(x, attn_norm_g, attn_w_in, a_q_gain, a_k_gain, b_q_gain, b_k_gain, lam_q1, lam_k1,
           lam_q2, lam_k2, b_sub_gain, attn_w_out, sgu_norm_g, sgu_w_in, sgu_v_gain, sgu_w_spatial,
           sgu_b_spatial, sgu_w_out, ffn_norm_g, peer_w_query, peer_sub_keys, peer_down, peer_up):
    batch, seq, d = x.shape
    depth = ffn_norm_g.shape[0]
    h = x.reshape(batch * seq, d)
    tables = [pack_expert_table(peer_down[layer], peer_up[layer]) for layer in range(depth)]
    for layer in range(depth):
        i = layer // 2
        if layer % 2 == 0:
            lambda_init = 0.8 - 0.6 * math.exp(-0.3 * layer)
            h = attention_layer(h, batch, seq, attn_norm_g[i], attn_w_in[i], a_q_gain[i],
                                a_k_gain[i], b_q_gain[i], b_k_gain[i], lam_q1[i], lam_k1[i],
                                lam_q2[i], lam_k2[i], b_sub_gain[i], attn_w_out[i], lambda_init)
        else:
            h = sgu_layer(h, sgu_norm_g[i], sgu_w_in[i], sgu_v_gain[i], sgu_w_spatial[i],
                          sgu_b_spatial[i], sgu_w_out[i])
        h = peer_layer(h, ffn_norm_g[layer], peer_w_query[layer], peer_sub_keys[layer],
                       tables[layer])
    return h.reshape(batch, seq, d)
```

```python
import functools
import math

import jax
import jax.numpy as jnp
from jax import lax
from jax.experimental import pallas as pl
from jax.experimental.pallas import tpu as pltpu
from jax.experimental.pallas import tpu_sc as plsc

D_MODEL = 1024
HEAD_DIM = 64
EPS = 1e-6
NEG = -1e30
A_HEADS = 8
B_HEADS = 4
C_CHUNK = 128
C_GROUPS = 8
C_WIDTH = 2 * D_MODEL
C_GROUP_DIM = C_WIDTH // C_GROUPS
PEER_HEADS = 8
PEER_NKEYS = 128
PEER_TOPK = 16
PEER_PICKS = PEER_HEADS * PEER_TOPK

LANES = 128
HALF = D_MODEL // 2
VMEM_LIMIT = 56 * 1024 * 1024

BF16 = jnp.bfloat16
F32 = jnp.float32


def _gelu(x):
    return 0.5 * x * (1.0 + jnp.tanh(math.sqrt(2.0 / math.pi) * (x + 0.044715 * (x * x * x))))


def _rms(x, g):
    return x * lax.rsqrt(jnp.mean(x * x, axis=-1, keepdims=True) + EPS) * g


def _dot_nt(a, b):
    return lax.dot_general(a, b, (((1,), (1,)), ((), ())), preferred_element_type=F32)


def _norm_matmul_kernel(x_ref, g_ref, w_ref, o_ref, xn_ref):
    @pl.when(pl.program_id(1) == 0)
    def _():
        xn_ref[...] = _rms(x_ref[...], g_ref[...]).astype(BF16)

    o_ref[...] = jnp.dot(xn_ref[...], w_ref[...], preferred_element_type=F32)


def norm_matmul(x, g, w, *, tm=1024, tn=1024):
    n, d = x.shape
    nout = w.shape[1]
    tm = min(tm, n)
    return pl.pallas_call(
        _norm_matmul_kernel,
        grid=(n // tm, nout // tn),
        in_specs=[
            pl.BlockSpec((tm, d), lambda i, j: (i, 0)),
            pl.BlockSpec((1, d), lambda i, j: (0, 0)),
            pl.BlockSpec((d, tn), lambda i, j: (0, j)),
        ],
        out_specs=pl.BlockSpec((tm, tn), lambda i, j: (i, j)),
        out_shape=jax.ShapeDtypeStruct((n, nout), F32),
        scratch_shapes=[pltpu.VMEM((tm, d), BF16)],
        compiler_params=pltpu.CompilerParams(
            dimension_semantics=("arbitrary", "arbitrary"), vmem_limit_bytes=VMEM_LIMIT),
        name="norm_matmul",
    )(x, g.reshape(1, d), w.astype(BF16))


def _pair_norm(t, gain, lo):
    sq = t * t
    s_lo = jnp.sum(jnp.where(lo, sq, 0.0), axis=-1, keepdims=True)
    s_hi = jnp.sum(jnp.where(lo, 0.0, sq), axis=-1, keepdims=True)
    ms = jnp.where(lo, s_lo, s_hi) * (1.0 / HEAD_DIM)
    return t * lax.rsqrt(ms + EPS) * gain


def _attn_kernel(*refs, mode, tq, tk, lambda_init):
    if mode == "dilated":
        q_ref, k_ref, v_ref, qg_ref, kg_ref, o_ref, kn_ref = refs
    else:
        (q_ref, k_ref, v_ref, qg_ref, kg_ref, lq1_ref, lk1_ref, lq2_ref, lk2_ref, sg_ref,
         o_ref, kn_ref) = refs
    i = pl.program_id(2)
    lo = lax.broadcasted_iota(jnp.int32, (1, LANES), 1) < HEAD_DIM

    @pl.when(i == 0)
    def _():
        kn_ref[...] = _pair_norm(k_ref[...], kg_ref[...], lo).astype(BF16)

    qn = _pair_norm(q_ref[...], qg_ref[...], lo) * (HEAD_DIM ** -0.5)
    qa = jnp.where(lo, qn, 0.0).astype(BF16)
    qb = jnp.where(lo, 0.0, qn).astype(BF16)
    assert tq == tk and tk % 16 == 0
    row = lax.broadcasted_iota(jnp.int32, (tq, tk), 0)
    col = lax.broadcasted_iota(jnp.int32, (tq, tk), 1)

    def step(j, carry, valid, weight):
        ma, la, acca, mb, lb, accb = carry
        off = pl.multiple_of(j * tk, tk)
        kb = kn_ref[pl.ds(off, tk), :]
        vb = v_ref[pl.ds(off, tk), :].astype(BF16)

        def update(qh, m, l, acc):
            s = _dot_nt(qh, kb)
            if valid is not None:
                s = jnp.where(valid, s, NEG)
            m_new = jnp.maximum(m, jnp.max(s, axis=-1, keepdims=True))
            alpha = jnp.exp(m - m_new)
            p = jnp.exp(s - m_new)
            if weight is not None:
                p = weight * p
            l_new = alpha * l + jnp.sum(p, axis=-1, keepdims=True)
            acc_new = alpha * acc + jnp.dot(p.astype(BF16), vb, preferred_element_type=F32)
            return m_new, l_new, acc_new

        ma, la, acca = update(qa, ma, la, acca)
        mb, lb, accb = update(qb, mb, lb, accb)
        return ma, la, acca, mb, lb, accb

    m0 = jnp.full((tq, 1), NEG, F32)
    l0 = jnp.zeros((tq, 1), F32)
    a0 = jnp.zeros((tq, LANES), F32)
    carry = (m0, l0, a0, m0, l0, a0)
    if mode == "dilated":
        near = (512 + tk - 1) // tk + 1
        first_near = jnp.maximum(i - (near - 1), 0)
        far_valid = ((row - col) & 15) == 0
        carry = lax.fori_loop(0, first_near, lambda j, c: step(j, c, far_valid, None), carry)

        def near_step(j, c):
            dist = (i - j) * tk + row - col
            cnt = ((dist <= 128).astype(F32)
                   + ((dist <= 512) & ((dist & 3) == 0)).astype(F32)
                   + ((dist & 15) == 0).astype(F32))
            cnt = jnp.where(dist >= 0, cnt, 0.0)
            return step(j, c, cnt > 0.0, cnt)

        carry = lax.fori_loop(first_near, i + 1, near_step, carry)
    else:
        carry = lax.fori_loop(0, i, lambda j, c: step(j, c, None, None), carry)
        carry = step(i, carry, col <= row, None)
    ma, la, acca, mb, lb, accb = carry
    oa = acca / la
    ob = accb / lb
    if mode == "dilated":
        o_ref[...] = jnp.where(lo, oa, ob)
    else:
        lam = (jnp.exp(jnp.sum(lq1_ref[...] * lk1_ref[...], axis=-1, keepdims=True))
               - jnp.exp(jnp.sum(lq2_ref[...] * lk2_ref[...], axis=-1, keepdims=True))
               + lambda_init)
        o = oa - lam * ob
        o_ref[...] = _rms(o, sg_ref[...]) * (1.0 - lambda_init)


def pair_attention(proj, batch, seq, mode, qcol, kcol, vcol, npairs, q_gain, k_gain, extras=(),
                   lambda_init=0.0, tq=512, tk=512):
    nq = seq // tq
    small = [q_gain.reshape(1, LANES), k_gain.reshape(1, LANES)] + [e.reshape(1, -1) for e in extras]
    small_specs = [pl.BlockSpec(s.shape, lambda b, p, i: (0, 0)) for s in small]
    kernel = functools.partial(_attn_kernel, mode=mode, tq=tq, tk=tk, lambda_init=lambda_init)
    return pl.pallas_call(
        kernel,
        grid=(batch, npairs, nq),
        in_specs=[
            pl.BlockSpec((tq, LANES), lambda b, p, i: (b * nq + i, qcol + p)),
            pl.BlockSpec((seq, LANES), lambda b, p, i: (b, kcol + p)),
            pl.BlockSpec((seq, LANES), lambda b, p, i: (b, vcol + p)),
        ] + small_specs,
        out_specs=pl.BlockSpec((tq, LANES), lambda b, p, i: (b * nq + i, p)),
        out_shape=jax.ShapeDtypeStruct((batch * seq, npairs * LANES), F32),
        scratch_shapes=[pltpu.VMEM((seq, LANES), BF16)],
        compiler_params=pltpu.CompilerParams(
            dimension_semantics=("arbitrary", "arbitrary", "arbitrary"),
            vmem_limit_bytes=VMEM_LIMIT),
        name="attn_" + mode,
    )(proj, proj, proj, *small)


def _out_proj_kernel(x_ref, a_ref, b_ref, wa_ref, wb_ref, o_ref):
    o_ref[...] = (x_ref[...]
                  + jnp.dot(a_ref[...].astype(BF16), wa_ref[...], preferred_element_type=F32)
                  + jnp.dot(b_ref[...].astype(BF16), wb_ref[...], preferred_element_type=F32))


def out_proj(x, a, b, w, *, tm=512):
    n, d = x.shape
    ka, kb = a.shape[1], b.shape[1]
    w = w.astype(BF16)
    tm = min(tm, n)
    return pl.pallas_call(
        _out_proj_kernel,
        grid=(n // tm,),
        in_specs=[
            pl.BlockSpec((tm, d), lambda i: (i, 0)),
            pl.BlockSpec((tm, ka), lambda i: (i, 0)),
            pl.BlockSpec((tm, kb), lambda i: (i, 0)),
            pl.BlockSpec((ka, d), lambda i: (0, 0)),
            pl.BlockSpec((kb, d), lambda i: (0, 0)),
        ],
        out_specs=pl.BlockSpec((tm, d), lambda i: (i, 0)),
        out_shape=jax.ShapeDtypeStruct((n, d), F32),
        compiler_params=pltpu.CompilerParams(
            dimension_semantics=("arbitrary",), vmem_limit_bytes=VMEM_LIMIT),
        name="out_proj",
    )(x, a, b, w[:ka], w[ka:])


def _sgu_kernel(x_ref, g_ref, win_ref, vg_ref, ws_ref, bs_ref, wout_ref, o_ref, gated_ref, *, tm):
    x = x_ref[...]
    xn = _rms(x, g_ref[...]).astype(BF16)
    z = _gelu(jnp.dot(xn, win_ref[...], preferred_element_type=F32))
    u = z[:, :C_WIDTH]
    v = _rms(z[:, C_WIDTH:], vg_ref[...]).astype(BF16)
    r = lax.broadcasted_iota(jnp.int32, (C_CHUNK, C_CHUNK), 0)
    c = lax.broadcasted_iota(jnp.int32, (C_CHUNK, C_CHUNK), 1)
    causal = c <= r
    for grp in range(C_GROUPS):
        ws = jnp.where(causal, ws_ref[grp], 0.0).astype(BF16)
        bias = bs_ref[:, grp:grp + 1]
        cols = slice(grp * C_GROUP_DIM, (grp + 1) * C_GROUP_DIM)
        for ch in range(tm // C_CHUNK):
            rws = slice(ch * C_CHUNK, (ch + 1) * C_CHUNK)
            gate = jnp.dot(ws, v[rws, cols], preferred_element_type=F32) + bias
            gated_ref[rws, cols] = (u[rws, cols] * gate).astype(BF16)
    o_ref[...] = x + jnp.dot(gated_ref[...], wout_ref[...], preferred_element_type=F32)


def sgu_layer(x, norm_g, w_in, v_gain, w_spatial, b_spatial, w_out, *, tm=256):
    n, d = x.shape
    kernel = functools.partial(_sgu_kernel, tm=tm)
    return pl.pallas_call(
        kernel,
        grid=(n // tm,),
        in_specs=[
            pl.BlockSpec((tm, d), lambda i: (i, 0)),
            pl.BlockSpec((1, d), lambda i: (0, 0)),
            pl.BlockSpec((d, 2 * C_WIDTH), lambda i: (0, 0)),
            pl.BlockSpec((1, C_WIDTH), lambda i: (0, 0)),
            pl.BlockSpec((C_GROUPS, C_CHUNK, C_CHUNK), lambda i: (0, 0, 0)),
            pl.BlockSpec((C_CHUNK, C_GROUPS), lambda i: (0, 0)),
            pl.BlockSpec((C_WIDTH, d), lambda i: (0, 0)),
        ],
        out_specs=pl.BlockSpec((tm, d), lambda i: (i, 0)),
        out_shape=jax.ShapeDtypeStruct((n, d), F32),
        scratch_shapes=[pltpu.VMEM((tm, C_WIDTH), BF16)],
        compiler_params=pltpu.CompilerParams(
            dimension_semantics=("arbitrary",), vmem_limit_bytes=VMEM_LIMIT),
        name="sgu",
    )(x, norm_g.reshape(1, d), w_in.astype(BF16), v_gain.reshape(1, C_WIDTH), w_spatial,
      b_spatial.T, w_out.astype(BF16))


def _peer_scores_kernel(x_ref, g_ref, wq_ref, sk_ref, hn_ref, sc_ref):
    hn = _rms(x_ref[...], g_ref[...])
    hn_ref[...] = hn
    q = jnp.dot(hn.astype(BF16), wq_ref[...], preferred_element_type=F32)
    for hp in range(2 * PEER_HEADS):
        cols = slice(hp * PEER_NKEYS, (hp + 1) * PEER_NKEYS)
        sc_ref[cols, :] = _dot_nt(sk_ref[hp].astype(BF16), q[:, cols].astype(BF16))


def peer_scores(x, norm_g, w_query, sub_keys, *, tm=512):
    n, d = x.shape
    nq = w_query.shape[1]
    tm = min(tm, n)
    sk = sub_keys.reshape(2 * PEER_HEADS, PEER_NKEYS, PEER_NKEYS)
    return pl.pallas_call(
        _peer_scores_kernel,
        grid=(n // tm,),
        in_specs=[
            pl.BlockSpec((tm, d), lambda i: (i, 0)),
            pl.BlockSpec((1, d), lambda i: (0, 0)),
            pl.BlockSpec((d, nq), lambda i: (0, 0)),
            pl.BlockSpec(sk.shape, lambda i: (0, 0, 0)),
        ],
        out_specs=[pl.BlockSpec((tm, d), lambda i: (i, 0)),
                   pl.BlockSpec((nq, tm), lambda i: (0, i))],
        out_shape=[jax.ShapeDtypeStruct((n, d), F32), jax.ShapeDtypeStruct((nq, n), F32)],
        compiler_params=pltpu.CompilerParams(
            dimension_semantics=("arbitrary",), vmem_limit_bytes=VMEM_LIMIT),
        name="peer_scores",
    )(x, norm_g.reshape(1, d), w_query.astype(BF16), sk)


SUBLANES = 8


def _peer_topk_kernel(sc_ref, idx_ref, gate_ref):
    tt = sc_ref.shape[1]
    key = lax.broadcasted_iota(jnp.int32, (PEER_NKEYS, tt), 0).astype(F32)
    row16 = lax.broadcasted_iota(jnp.int32, (PEER_TOPK, tt), 0)
    row8 = lax.broadcasted_iota(jnp.int32, (SUBLANES, tt), 0)
    row8f = row8.astype(F32)
    ninf = jnp.float32(-jnp.inf)

    def extract16(s):
        vals = jnp.zeros((PEER_TOPK, tt), F32)
        ids = jnp.zeros((PEER_TOPK, tt), F32)
        for k in range(PEER_TOPK):
            m = jnp.max(s, axis=0, keepdims=True)
            am = jnp.min(jnp.where(s == m, key, float(PEER_NKEYS)), axis=0, keepdims=True)
            s = jnp.where(key == am, ninf, s)
            vals = jnp.where(row16 == k, m, vals)
            ids = jnp.where(row16 == k, am, ids)
        return vals, ids

    def head(h, carry):
        off = pl.multiple_of(h * 2 * PEER_NKEYS, 2 * PEER_NKEYS)
        v1, i1 = extract16(sc_ref[pl.ds(off, PEER_NKEYS), :])
        v2, i2 = extract16(sc_ref[pl.ds(off + PEER_NKEYS, PEER_NKEYS), :])
        e1 = i1 * float(PEER_NKEYS)
        cand = [v1[0:1] + v2]
        eid = [e1[0:1] + i2]
        pos = [row16.astype(F32)]
        for i in range(1, SUBLANES):
            keep = row8 < (PEER_TOPK // (i + 1))
            cand.append(jnp.where(keep, v1[i:i + 1] + v2[0:SUBLANES], ninf))
            eid.append(e1[i:i + 1] + i2[0:SUBLANES])
            pos.append(row8f + float(i * PEER_TOPK))
        cand.append(v1[SUBLANES:] + v2[0:1])
        eid.append(e1[SUBLANES:] + i2[0:1])
        pos.append((row8f + float(SUBLANES)) * float(PEER_TOPK))
        cand = jnp.concatenate(cand, axis=0)
        eid = jnp.concatenate(eid, axis=0)
        pos = jnp.concatenate(pos, axis=0)
        top = jnp.zeros((PEER_TOPK, tt), F32)
        idx = jnp.zeros((PEER_TOPK, tt), F32)
        for k in range(PEER_TOPK):
            m = jnp.max(cand, axis=0, keepdims=True)
            p = jnp.min(jnp.where(cand == m, pos, 1e9), axis=0, keepdims=True)
            hit = pos == p
            e = jnp.max(jnp.where(hit, eid, -1.0), axis=0, keepdims=True)
            cand = jnp.where(hit, ninf, cand)
            top = jnp.where(row16 == k, m, top)
            idx = jnp.where(row16 == k, e, idx)
        w = jnp.exp(top - top[0:1])
        out = pl.ds(pl.multiple_of(h * PEER_TOPK, PEER_TOPK), PEER_TOPK)
        idx_ref[out, :] = idx.astype(jnp.int32)
        gate_ref[out, :] = w / jnp.sum(w, axis=0, keepdims=True)
        return carry

    unroll = 4

    def heads(hh, carry):
        for u in range(unroll):
            head(unroll * hh + u, carry)
        return carry

    lax.fori_loop(0, PEER_HEADS // unroll, heads, 0)


def peer_topk(scores, first_token, n, *, tt=256):
    rows = scores.shape[0]
    b0 = first_token // tt
    idx_t, gate_t = pl.pallas_call(
        _peer_topk_kernel,
        grid=(n // tt,),
        in_specs=[pl.BlockSpec((rows, tt), lambda i: (0, b0 + i))],
        out_specs=[pl.BlockSpec((PEER_PICKS, tt), lambda i: (0, i)),
                   pl.BlockSpec((PEER_PICKS, tt), lambda i: (0, i))],
        out_shape=[jax.ShapeDtypeStruct((PEER_PICKS, n), jnp.int32),
                   jax.ShapeDtypeStruct((PEER_PICKS, n), F32)],
        compiler_params=pltpu.CompilerParams(
            dimension_semantics=("arbitrary",), vmem_limit_bytes=VMEM_LIMIT),
        name="peer_topk",
    )(scores)
    return idx_t.T, gate_t.T


def pack_expert_table(down, up):
    e, d = down.shape
    rows = min(PACK_ROWS, e)
    packed = pl.pallas_call(
        functools.partial(_pack_kernel, rows=rows),
        grid=(e // rows,),
        in_specs=[pl.BlockSpec((rows, d), lambda i: (i, 0)),
                  pl.BlockSpec((rows, d), lambda i: (i, 0))],
        out_specs=pl.BlockSpec(memory_space=pl.ANY),
        out_shape=jax.ShapeDtypeStruct((e, d // LANES, 1, LANES), jnp.int32),
        scratch_shapes=[pltpu.VMEM((d // LANES, rows, LANES), jnp.int32),
                        pltpu.SemaphoreType.DMA(())],
        compiler_params=pltpu.CompilerParams(
            dimension_semantics=("arbitrary",), vmem_limit_bytes=VMEM_LIMIT),
        name="pack_experts",
    )(down, up)
    return packed.reshape(e, d // LANES, LANES)


PACK_ROWS = 512


def _pack_kernel(down_ref, up_ref, out_ref, stage_ref, sem_ref, *, rows):
    step = pl.program_id(0)
    quarter = HALF // LANES

    def words(t):
        lo = pltpu.bitcast(t[:, :HALF].astype(BF16).astype(F32), jnp.uint32)
        hi = pltpu.bitcast(t[:, HALF:].astype(BF16).astype(F32), jnp.uint32)
        return pltpu.bitcast((hi & jnp.uint32(0xFFFF0000)) | (lo >> 16), jnp.int32)

    for k, ref in enumerate((down_ref, up_ref)):
        w = words(ref[...])
        for c in range(quarter):
            stage_ref[k * quarter + c] = w[:, c * LANES:(c + 1) * LANES]
    for r in range(rows):
        pltpu.make_async_copy(stage_ref.at[:, pl.ds(r, 1), :], out_ref.at[step * rows + r],
                              sem_ref).start(priority=r % 2)
    pltpu.make_async_copy(stage_ref, stage_ref, sem_ref).wait()


def _row_copy(tab_ref, buf_ref, sem_ref, expert, slot, row):
    return pltpu.make_async_copy(tab_ref.at[expert],
                                 buf_ref.at[slot, :, pl.ds(row, 1), :], sem_ref.at[slot])


APPLY_TOKENS = 8
CHUNKS = D_MODEL // LANES
STEP_TOKENS = 4 * APPLY_TOKENS


def _staged_words(st_ref, group, t):
    base = (group * APPLY_TOKENS + t) * PEER_PICKS * CHUNKS
    return [st_ref[pl.ds(base + c, PEER_PICKS, stride=CHUNKS), :] for c in range(CHUNKS)]


def _apply_groups(hn_ref, gate_ref, res_ref, o_ref, groups, between):
    tb = APPLY_TOKENS
    lane_even = (lax.broadcasted_iota(jnp.int32, (1, LANES), 1) % 2) == 0
    even2 = jnp.concatenate([lane_even, lane_even], axis=1)
    tok_of_row = lax.broadcasted_iota(jnp.int32, (2 * tb, 1), 0) % tb
    xxs, rsels, tbls = [], [], []
    for first, _ in groups:
        x = hn_ref[pl.ds(first, tb), :]
        xxs.append(jnp.concatenate([x[:, :HALF], x[:, HALF:]], axis=0).astype(BF16))
        rsels.append(jnp.zeros((2 * tb, 2 * PEER_PICKS), F32))
        tbls.append([])
    for t in range(tb):
        for g, (_, words_fn) in enumerate(groups):
            words = jnp.concatenate(words_fn(t), axis=1)
            tbl = pltpu.bitcast(words, BF16)
            tbls[g].append(tbl)
            rsels[g] = rsels[g] + jnp.where(tok_of_row == t, _dot_nt(xxs[g], tbl[:, :HALF]), 0.0)
        between(t)
    for g, (first, _) in enumerate(groups):
        tok = pl.ds(first, tb)
        coefs = []
        for c in range(2 * PEER_PICKS // LANES):
            cols = slice(c * LANES, (c + 1) * LANES)
            part = jnp.where(lane_even, rsels[g][:tb, cols], rsels[g][tb:, cols])
            hid = part + jnp.where(lane_even, pltpu.roll(part, LANES - 1, 1), pltpu.roll(part, 1, 1))
            coefs.append(gate_ref[tok, cols] * _gelu(hid))
        coef = jnp.concatenate(coefs, axis=1)
        cc = jnp.concatenate([jnp.where(even2, coef, 0.0), jnp.where(even2, 0.0, coef)],
                             axis=0).astype(BF16)
        ysel = jnp.zeros((2 * tb, HALF), F32)
        for t in range(tb):
            ysel = ysel + jnp.where(
                tok_of_row == t, jnp.dot(cc, tbls[g][t][:, HALF:], preferred_element_type=F32), 0.0)
        y = jnp.concatenate([ysel[:tb], ysel[tb:]], axis=1)
        o_ref[tok, :] = res_ref[tok, :] + y


def _peer_apply_kernel(idx0_ref, idx1_ref, hn_ref, gate_ref, res_ref, tab_ref, st_ref, o_ref,
                       buf_ref, sem_ref):
    step = pl.program_id(0)
    nsteps = pl.num_programs(0)
    tb = APPLY_TOKENS

    def issue_token(idx_ref, first_token, slot, t):
        for j in range(PEER_PICKS):
            _row_copy(tab_ref, buf_ref, sem_ref, idx_ref[first_token + t, j], slot,
                      t * PEER_PICKS + j).start(priority=j % 2)

    def wait(slot):
        pltpu.make_async_copy(buf_ref.at[slot], buf_ref.at[slot], sem_ref.at[slot]).wait()

    def copied_words(slot, t):
        return [buf_ref[slot, c, pl.ds(t * PEER_PICKS, PEER_PICKS), :] for c in range(CHUNKS)]

    compute = functools.partial(_apply_groups, hn_ref, gate_ref, res_ref, o_ref)

    @pl.when(step == 0)
    def _():
        for t in range(tb):
            issue_token(idx0_ref, 0, 0, t)

    wait(0)
    compute([(0, functools.partial(copied_words, 0)),
             (2 * tb, functools.partial(_staged_words, st_ref, 0))],
            functools.partial(issue_token, idx0_ref, tb, 1))
    wait(1)
    compute([(tb, functools.partial(copied_words, 1)),
             (3 * tb, functools.partial(_staged_words, st_ref, 1))],
            functools.partial(issue_token, idx1_ref, 0, 0))

    @pl.when(step == nsteps - 1)
    def _():
        wait(0)


STAGED_BLOCK_ROWS = 2 * APPLY_TOKENS * PEER_PICKS * CHUNKS


def peer_apply(idx, hn, gate2, acc, table, staged, first_token, ntokens):
    n, d = hn.shape
    tb = STEP_TOKENS
    nsteps = ntokens // tb
    s0 = first_token // tb
    smem = pltpu.SMEM
    return pl.pallas_call(
        _peer_apply_kernel,
        grid=(nsteps,),
        in_specs=[
            pl.BlockSpec((tb, PEER_PICKS), lambda i: (s0 + i, 0), memory_space=smem),
            pl.BlockSpec((tb, PEER_PICKS), lambda i: (s0 + jnp.minimum(i + 1, nsteps - 1), 0),
                         memory_space=smem),
            pl.BlockSpec((tb, d), lambda i: (s0 + i, 0)),
            pl.BlockSpec((tb, 2 * PEER_PICKS), lambda i: (s0 + i, 0)),
            pl.BlockSpec((tb, d), lambda i: (s0 + i, 0)),
            pl.BlockSpec(memory_space=pl.ANY),
            pl.BlockSpec((STAGED_BLOCK_ROWS, LANES), lambda i: (i, 0)),
        ],
        out_specs=pl.BlockSpec((tb, d), lambda i: (s0 + i, 0)),
        out_shape=jax.ShapeDtypeStruct((n, d), F32),
        input_output_aliases={4: 0},
        scratch_shapes=[pltpu.VMEM((2, CHUNKS, APPLY_TOKENS * PEER_PICKS, LANES), jnp.int32),
                        pltpu.SemaphoreType.DMA((2,))],
        compiler_params=pltpu.CompilerParams(
            dimension_semantics=("arbitrary",), vmem_limit_bytes=VMEM_LIMIT),
        name="peer_apply",
    )(idx, idx, hn, gate2, acc, table.reshape(table.shape[0], CHUNKS, 1, LANES), staged)


SC_CORES = 2
SC_SUBCORES = 16
SC_CHUNK = 32


def sc_gather_rows(table, idx):
    b = idx.shape[0]
    nw = SC_CORES * SC_SUBCORES
    per_w = b // nw
    nchunks = per_w // SC_CHUNK
    assert per_w * nw == b and nchunks * SC_CHUNK == per_w and nchunks % 2 == 0
    mesh = plsc.VectorSubcoreMesh(core_axis_name="c", subcore_axis_name="s")

    @functools.partial(
        pl.kernel, mesh=mesh,
        out_type=jax.ShapeDtypeStruct((b,) + table.shape[1:], table.dtype),
        scratch_types=[
            pltpu.VMEM((per_w,), jnp.int32),
            pltpu.VMEM((SC_CHUNK,) + table.shape[1:], table.dtype),
            pltpu.VMEM((SC_CHUNK,) + table.shape[1:], table.dtype),
            pltpu.SemaphoreType.DMA,
            pltpu.SemaphoreType.DMA,
        ],
        name="sc_gather_rows",
    )
    def gather(tab_hbm, idx_hbm, out_hbm, idx_v, rows0, rows1, sem0, sem1):
        wid = lax.axis_index("s") * SC_CORES + lax.axis_index("c")
        base = wid * per_w
        pltpu.sync_copy(idx_hbm.at[pl.ds(base, per_w)], idx_v)

        def start(chunk, rows, sem):
            off = pl.multiple_of(chunk * SC_CHUNK, SC_CHUNK)
            pltpu.async_copy(tab_hbm.at[idx_v.at[pl.ds(off, SC_CHUNK)]], rows, sem)

        def finish(chunk, rows, sem):
            off = pl.multiple_of(chunk * SC_CHUNK, SC_CHUNK)
            pltpu.make_async_copy(tab_hbm.at[idx_v.at[pl.ds(off, SC_CHUNK)]], rows, sem).wait()
            pltpu.sync_copy(rows, out_hbm.at[pl.ds(base + off, SC_CHUNK)])

        @pl.loop(0, nchunks, step=2)
        def _(i):
            start(i, rows0, sem0)
            finish(i, rows0, sem0)
            start(i + 1, rows1, sem1)
            finish(i + 1, rows1, sem1)

    return gather(table, idx)


PEER_GROUP = 4096
PEER_FIRST_GROUP = 2048


def _group_bounds(n):
    first = min(PEER_FIRST_GROUP, n)
    bounds = [0, first] + ([2 * first] if n >= 2 * first else [])
    while bounds[-1] < n:
        bounds.append(min(bounds[-1] + PEER_GROUP, n))
    return bounds


def peer_layer(x, norm_g, w_query, sub_keys, table):
    n = x.shape[0]
    hn, scores = peer_scores(x, norm_g, w_query, sub_keys)
    bounds = _group_bounds(n)
    ngroups = len(bounds) - 1

    def staged_indices(src, lo, hi):
        steps = src[lo:hi].reshape((hi - lo) // STEP_TOKENS, STEP_TOKENS, PEER_PICKS)
        return steps[:, STEP_TOKENS // 2:].reshape(-1)

    idx0, gate0 = peer_topk(scores, 0, bounds[1])
    first = staged_indices(idx0, 0, bounds[1])
    if ngroups > 1:
        scores, first = lax.optimization_barrier((scores, first))
        idx1, gate1 = peer_topk(scores, bounds[1], n - bounds[1])
        idx, gate = jnp.concatenate([idx0, idx1]), jnp.concatenate([gate0, gate1])
    else:
        idx, gate = idx0, gate0
    gate2 = jnp.repeat(gate, 2, axis=1)

    acc = x
    pending = [first] + [staged_indices(idx, bounds[g], bounds[g + 1]) for g in range(1, min(2, ngroups))]
    for g in range(ngroups):
        staged = sc_gather_rows(table, pending.pop(0)).reshape(-1, LANES)
        acc = peer_apply(idx, hn, gate2, acc, table, staged, bounds[g], bounds[g + 1] - bounds[g])
        if g + 2 < ngroups:
            nxt, acc = lax.optimization_barrier(
                (staged_indices(idx, bounds[g + 2], bounds[g + 3]), acc))
            pending.append(nxt)
    return acc


def attention_layer(x, batch, seq, norm_g, w_in, a_q_gain, a_k_gain, b_q_gain, b_k_gain,
                    lam_q1, lam_k1, lam_q2, lam_k2, b_sub_gain, w_out, lambda_init):
    proj = norm_matmul(x, norm_g, w_in)
    na = A_HEADS // 2
    tile2 = lambda g: jnp.concatenate([g, g])
    out_a = pair_attention(proj, batch, seq, "dilated", 0, na, 2 * na, na,
                           tile2(a_q_gain), tile2(a_k_gain))
    out_b = pair_attention(proj, batch, seq, "diff", 3 * na, 3 * na + B_HEADS, 3 * na + 2 * B_HEADS,
                           B_HEADS, b_q_gain.reshape(-1), b_k_gain.reshape(-1),
                           extras=(lam_q1, lam_k1, lam_q2, lam_k2, b_sub_gain),
                           lambda_init=lambda_init)
    return out_proj(x, out_a, out_b, w_out)


def kernel(x, attn_norm_g, attn_w_in, a_q_gain, a_k_gain, b_q_gain, b_k_gain, lam_q1, lam_k1,
           lam_q2, lam_k2, b_sub_gain, attn_w_out, sgu_norm_g, sgu_w_in, sgu_v_gain, sgu_w_spatial,
           sgu_b_spatial, sgu_w_out, ffn_norm_g, peer_w_query, peer_sub_keys, peer_down, peer_up):
    batch, seq, d = x.shape
    depth = ffn_norm_g.shape[0]
    h = x.reshape(batch * seq, d)
    tables = [pack_expert_table(peer_down[layer], peer_up[layer]) for layer in range(depth)]
    for layer in range(depth):
        i = layer // 2
        if layer % 2 == 0:
            lambda_init = 0.8 - 0.6 * math.exp(-0.3 * layer)
            h = attention_layer(h, batch, seq, attn_norm_g[i], attn_w_in[i], a_q_gain[i],
                                a_k_gain[i], b_q_gain[i], b_k_gain[i], lam_q1[i], lam_k1[i],
                                lam_q2[i], lam_k2[i], b_sub_gain[i], attn_w_out[i], lambda_init)
        else:
            h = sgu_layer(h, sgu_norm_g[i], sgu_w_in[i], sgu_v_gain[i], sgu_w_spatial[i],
                          sgu_b_spatial[i], sgu_w_out[i])
        h = peer_layer(h, ffn_norm_g[layer], peer_w_query[layer], peer_sub_keys[layer],
                       tables[layer])
    return h.reshape(batch, seq, d)
```

```python
import functools
import math

import jax
import jax.numpy as jnp
from jax import lax
from jax.experimental import pallas as pl
from jax.experimental.pallas import tpu as pltpu
from jax.experimental.pallas import tpu_sc as plsc

D_MODEL = 1024
HEAD_DIM = 64
EPS = 1e-6
NEG = -1e30
A_HEADS = 8
B_HEADS = 4
C_CHUNK = 128
C_GROUPS = 8
C_WIDTH = 2 * D_MODEL
C_GROUP_DIM = C_WIDTH // C_GROUPS
PEER_HEADS = 8
PEER_NKEYS = 128
PEER_TOPK = 16
PEER_PICKS = PEER_HEADS * PEER_TOPK

LANES = 128
HALF = D_MODEL // 2
VMEM_LIMIT = 56 * 1024 * 1024

BF16 = jnp.bfloat16
F32 = jnp.float32


def _gelu(x):
    return 0.5 * x * (1.0 + jnp.tanh(math.sqrt(2.0 / math.pi) * (x + 0.044715 * (x * x * x))))


def _rms(x, g):
    return x * lax.rsqrt(jnp.mean(x * x, axis=-1, keepdims=True) + EPS) * g


def _dot_nt(a, b):
    return lax.dot_general(a, b, (((1,), (1,)), ((), ())), preferred_element_type=F32)


def _norm_matmul_kernel(x_ref, g_ref, w_ref, o_ref, xn_ref):
    @pl.when(pl.program_id(1) == 0)
    def _():
        xn_ref[...] = _rms(x_ref[...], g_ref[...]).astype(BF16)

    o_ref[...] = jnp.dot(xn_ref[...], w_ref[...], preferred_element_type=F32)


def norm_matmul(x, g, w, *, tm=1024, tn=1024):
    n, d = x.shape
    nout = w.shape[1]
    tm = min(tm, n)
    return pl.pallas_call(
        _norm_matmul_kernel,
        grid=(n // tm, nout // tn),
        in_specs=[
            pl.BlockSpec((tm, d), lambda i, j: (i, 0)),
            pl.BlockSpec((1, d), lambda i, j: (0, 0)),
            pl.BlockSpec((d, tn), lambda i, j: (0, j)),
        ],
        out_specs=pl.BlockSpec((tm, tn), lambda i, j: (i, j)),
        out_shape=jax.ShapeDtypeStruct((n, nout), F32),
        scratch_shapes=[pltpu.VMEM((tm, d), BF16)],
        compiler_params=pltpu.CompilerParams(
            dimension_semantics=("arbitrary", "arbitrary"), vmem_limit_bytes=VMEM_LIMIT),
        name="norm_matmul",
    )(x, g.reshape(1, d), w.astype(BF16))


def _pair_norm(t, gain, lo):
    sq = t * t
    s_lo = jnp.sum(jnp.where(lo, sq, 0.0), axis=-1, keepdims=True)
    s_hi = jnp.sum(jnp.where(lo, 0.0, sq), axis=-1, keepdims=True)
    ms = jnp.where(lo, s_lo, s_hi) * (1.0 / HEAD_DIM)
    return t * lax.rsqrt(ms + EPS) * gain


def _attn_kernel(*refs, mode, tq, tk, lambda_init):
    if mode == "dilated":
        q_ref, k_ref, v_ref, qg_ref, kg_ref, o_ref, kn_ref = refs
    else:
        (q_ref, k_ref, v_ref, qg_ref, kg_ref, lq1_ref, lk1_ref, lq2_ref, lk2_ref, sg_ref,
         o_ref, kn_ref) = refs
    i = pl.program_id(2)
    lo = lax.broadcasted_iota(jnp.int32, (1, LANES), 1) < HEAD_DIM

    @pl.when(i == 0)
    def _():
        kn_ref[...] = _pair_norm(k_ref[...], kg_ref[...], lo).astype(BF16)

    qn = _pair_norm(q_ref[...], qg_ref[...], lo) * (HEAD_DIM ** -0.5)
    qa = jnp.where(lo, qn, 0.0).astype(BF16)
    qb = jnp.where(lo, 0.0, qn).astype(BF16)
    assert tq == tk and tk % 16 == 0
    row = lax.broadcasted_iota(jnp.int32, (tq, tk), 0)
    col = lax.broadcasted_iota(jnp.int32, (tq, tk), 1)

    def step(j, carry, valid, weight):
        ma, la, acca, mb, lb, accb = carry
        off = pl.multiple_of(j * tk, tk)
        kb = kn_ref[pl.ds(off, tk), :]
        vb = v_ref[pl.ds(off, tk), :].astype(BF16)

        def update(qh, m, l, acc):
            s = _dot_nt(qh, kb)
            if valid is not None:
                s = jnp.where(valid, s, NEG)
            m_new = jnp.maximum(m, jnp.max(s, axis=-1, keepdims=True))
            alpha = jnp.exp(m - m_new)
            p = jnp.exp(s - m_new)
            if weight is not None:
                p = weight * p
            l_new = alpha * l + jnp.sum(p, axis=-1, keepdims=True)
            acc_new = alpha * acc + jnp.dot(p.astype(BF16), vb, preferred_element_type=F32)
            return m_new, l_new, acc_new

        ma, la, acca = update(qa, ma, la, acca)
        mb, lb, accb = update(qb, mb, lb, accb)
        return ma, la, acca, mb, lb, accb

    m0 = jnp.full((tq, 1), NEG, F32)
    l0 = jnp.zeros((tq, 1), F32)
    a0 = jnp.zeros((tq, LANES), F32)
    carry = (m0, l0, a0, m0, l0, a0)
    if mode == "dilated":
        near = (512 + tk - 1) // tk + 1
        first_near = jnp.maximum(i - (near - 1), 0)
        far_valid = ((row - col) & 15) == 0
        carry = lax.fori_loop(0, first_near, lambda j, c: step(j, c, far_valid, None), carry)

        def near_step(j, c):
            dist = (i - j) * tk + row - col
            cnt = ((dist <= 128).astype(F32)
                   + ((dist <= 512) & ((dist & 3) == 0)).astype(F32)
                   + ((dist & 15) == 0).astype(F32))
            cnt = jnp.where(dist >= 0, cnt, 0.0)
            return step(j, c, cnt > 0.0, cnt)

        carry = lax.fori_loop(first_near, i + 1, near_step, carry)
    else:
        carry = lax.fori_loop(0, i, lambda j, c: step(j, c, None, None), carry)
        carry = step(i, carry, col <= row, None)
    ma, la, acca, mb, lb, accb = carry
    oa = acca / la
    ob = accb / lb
    if mode == "dilated":
        o_ref[...] = jnp.where(lo, oa, ob)
    else:
        lam = (jnp.exp(jnp.sum(lq1_ref[...] * lk1_ref[...], axis=-1, keepdims=True))
               - jnp.exp(jnp.sum(lq2_ref[...] * lk2_ref[...], axis=-1, keepdims=True))
               + lambda_init)
        o = oa - lam * ob
        o_ref[...] = _rms(o, sg_ref[...]) * (1.0 - lambda_init)


def pair_attention(proj, batch, seq, mode, qcol, kcol, vcol, npairs, q_gain, k_gain, extras=(),
                   lambda_init=0.0, tq=512, tk=512):
    nq = seq // tq
    small = [q_gain.reshape(1, LANES), k_gain.reshape(1, LANES)] + [e.reshape(1, -1) for e in extras]
    small_specs = [pl.BlockSpec(s.shape, lambda b, p, i: (0, 0)) for s in small]
    kernel = functools.partial(_attn_kernel, mode=mode, tq=tq, tk=tk, lambda_init=lambda_init)
    return pl.pallas_call(
        kernel,
        grid=(batch, npairs, nq),
        in_specs=[
            pl.BlockSpec((tq, LANES), lambda b, p, i: (b * nq + i, qcol + p)),
            pl.BlockSpec((seq, LANES), lambda b, p, i: (b, kcol + p)),
            pl.BlockSpec((seq, LANES), lambda b, p, i: (b, vcol + p)),
        ] + small_specs,
        out_specs=pl.BlockSpec((tq, LANES), lambda b, p, i: (b * nq + i, p)),
        out_shape=jax.ShapeDtypeStruct((batch * seq, npairs * LANES), F32),
        scratch_shapes=[pltpu.VMEM((seq, LANES), BF16)],
        compiler_params=pltpu.CompilerParams(
            dimension_semantics=("arbitrary", "arbitrary", "arbitrary"),
            vmem_limit_bytes=VMEM_LIMIT),
        name="attn_" + mode,
    )(proj, proj, proj, *small)


def _out_proj_kernel(x_ref, a_ref, b_ref, wa_ref, wb_ref, o_ref):
    o_ref[...] = (x_ref[...]
                  + jnp.dot(a_ref[...].astype(BF16), wa_ref[...], preferred_element_type=F32)
                  + jnp.dot(b_ref[...].astype(BF16), wb_ref[...], preferred_element_type=F32))


def out_proj(x, a, b, w, *, tm=512):
    n, d = x.shape
    ka, kb = a.shape[1], b.shape[1]
    w = w.astype(BF16)
    tm = min(tm, n)
    return pl.pallas_call(
        _out_proj_kernel,
        grid=(n // tm,),
        in_specs=[
            pl.BlockSpec((tm, d), lambda i: (i, 0)),
            pl.BlockSpec((tm, ka), lambda i: (i, 0)),
            pl.BlockSpec((tm, kb), lambda i: (i, 0)),
            pl.BlockSpec((ka, d), lambda i: (0, 0)),
            pl.BlockSpec((kb, d), lambda i: (0, 0)),
        ],
        out_specs=pl.BlockSpec((tm, d), lambda i: (i, 0)),
        out_shape=jax.ShapeDtypeStruct((n, d), F32),
        compiler_params=pltpu.CompilerParams(
            dimension_semantics=("arbitrary",), vmem_limit_bytes=VMEM_LIMIT),
        name="out_proj",
    )(x, a, b, w[:ka], w[ka:])


def _sgu_kernel(x_ref, g_ref, win_ref, vg_ref, ws_ref, bs_ref, wout_ref, o_ref, gated_ref, *, tm):
    x = x_ref[...]
    xn = _rms(x, g_ref[...]).astype(BF16)
    z = _gelu(jnp.dot(xn, win_ref[...], preferred_element_type=F32))
    u = z[:, :C_WIDTH]
    v = _rms(z[:, C_WIDTH:], vg_ref[...]).astype(BF16)
    r = lax.broadcasted_iota(jnp.int32, (C_CHUNK, C_CHUNK), 0)
    c = lax.broadcasted_iota(jnp.int32, (C_CHUNK, C_CHUNK), 1)
    causal = c <= r
    for grp in range(C_GROUPS):
        ws = jnp.where(causal, ws_ref[grp], 0.0).astype(BF16)
        bias = bs_ref[:, grp:grp + 1]
        cols = slice(grp * C_GROUP_DIM, (grp + 1) * C_GROUP_DIM)
        for ch in range(tm // C_CHUNK):
            rws = slice(ch * C_CHUNK, (ch + 1) * C_CHUNK)
            gate = jnp.dot(ws, v[rws, cols], preferred_element_type=F32) + bias
            gated_ref[rws, cols] = (u[rws, cols] * gate).astype(BF16)
    o_ref[...] = x + jnp.dot(gated_ref[...], wout_ref[...], preferred_element_type=F32)


def sgu_layer(x, norm_g, w_in, v_gain, w_spatial, b_spatial, w_out, *, tm=256):
    n, d = x.shape
    kernel = functools.partial(_sgu_kernel, tm=tm)
    return pl.pallas_call(
        kernel,
        grid=(n // tm,),
        in_specs=[
            pl.BlockSpec((tm, d), lambda i: (i, 0)),
            pl.BlockSpec((1, d), lambda i: (0, 0)),
            pl.BlockSpec((d, 2 * C_WIDTH), lambda i: (0, 0)),
            pl.BlockSpec((1, C_WIDTH), lambda i: (0, 0)),
            pl.BlockSpec((C_GROUPS, C_CHUNK, C_CHUNK), lambda i: (0, 0, 0)),
            pl.BlockSpec((C_CHUNK, C_GROUPS), lambda i: (0, 0)),
            pl.BlockSpec((C_WIDTH, d), lambda i: (0, 0)),
        ],
        out_specs=pl.BlockSpec((tm, d), lambda i: (i, 0)),
        out_shape=jax.ShapeDtypeStruct((n, d), F32),
        scratch_shapes=[pltpu.VMEM((tm, C_WIDTH), BF16)],
        compiler_params=pltpu.CompilerParams(
            dimension_semantics=("arbitrary",), vmem_limit_bytes=VMEM_LIMIT),
        name="sgu",
    )(x, norm_g.reshape(1, d), w_in.astype(BF16), v_gain.reshape(1, C_WIDTH), w_spatial,
      b_spatial.T, w_out.astype(BF16))


def _peer_scores_kernel(x_ref, g_ref, wq_ref, sk_ref, hn_ref, sc_ref):
    hn = _rms(x_ref[...], g_ref[...])
    hn_ref[...] = hn
    q = jnp.dot(hn.astype(BF16), wq_ref[...], preferred_element_type=F32)
    for hp in range(2 * PEER_HEADS):
        cols = slice(hp * PEER_NKEYS, (hp + 1) * PEER_NKEYS)
        sc_ref[cols, :] = _dot_nt(sk_ref[hp].astype(BF16), q[:, cols].astype(BF16))


def peer_scores(x, norm_g, w_query, sub_keys, *, tm=512):
    n, d = x.shape
    nq = w_query.shape[1]
    tm = min(tm, n)
    sk = sub_keys.reshape(2 * PEER_HEADS, PEER_NKEYS, PEER_NKEYS)
    return pl.pallas_call(
        _peer_scores_kernel,
        grid=(n // tm,),
        in_specs=[
            pl.BlockSpec((tm, d), lambda i: (i, 0)),
            pl.BlockSpec((1, d), lambda i: (0, 0)),
            pl.BlockSpec((d, nq), lambda i: (0, 0)),
            pl.BlockSpec(sk.shape, lambda i: (0, 0, 0)),
        ],
        out_specs=[pl.BlockSpec((tm, d), lambda i: (i, 0)),
                   pl.BlockSpec((nq, tm), lambda i: (0, i))],
        out_shape=[jax.ShapeDtypeStruct((n, d), F32), jax.ShapeDtypeStruct((nq, n), F32)],
        compiler_params=pltpu.CompilerParams(
            dimension_semantics=("arbitrary",), vmem_limit_bytes=VMEM_LIMIT),
        name="peer_scores",
    )(x, norm_g.reshape(1, d), w_query.astype(BF16), sk)


SUBLANES = 8


def _peer_topk_kernel(sc_ref, idx_ref, gate_ref):
    tt = sc_ref.shape[1]
    key = lax.broadcasted_iota(jnp.int32, (PEER_NKEYS, tt), 0).astype(F32)
    row16 = lax.broadcasted_iota(jnp.int32, (PEER_TOPK, tt), 0)
    row8 = lax.broadcasted_iota(jnp.int32, (SUBLANES, tt), 0)
    row8f = row8.astype(F32)
    ninf = jnp.float32(-jnp.inf)

    def extract16(s):
        vals = jnp.zeros((PEER_TOPK, tt), F32)
        ids = jnp.zeros((PEER_TOPK, tt), F32)
        for k in range(PEER_TOPK):
            m = jnp.max(s, axis=0, keepdims=True)
            am = jnp.min(jnp.where(s == m, key, float(PEER_NKEYS)), axis=0, keepdims=True)
            s = jnp.where(key == am, ninf, s)
            vals = jnp.where(row16 == k, m, vals)
            ids = jnp.where(row16 == k, am, ids)
        return vals, ids

    def head(h, carry):
        off = pl.multiple_of(h * 2 * PEER_NKEYS, 2 * PEER_NKEYS)
        v1, i1 = extract16(sc_ref[pl.ds(off, PEER_NKEYS), :])
        v2, i2 = extract16(sc_ref[pl.ds(off + PEER_NKEYS, PEER_NKEYS), :])
        e1 = i1 * float(PEER_NKEYS)
        cand = [v1[0:1] + v2]
        eid = [e1[0:1] + i2]
        pos = [row16.astype(F32)]
        for i in range(1, SUBLANES):
            keep = row8 < (PEER_TOPK // (i + 1))
            cand.append(jnp.where(keep, v1[i:i + 1] + v2[0:SUBLANES], ninf))
            eid.append(e1[i:i + 1] + i2[0:SUBLANES])
            pos.append(row8f + float(i * PEER_TOPK))
        cand.append(v1[SUBLANES:] + v2[0:1])
        eid.append(e1[SUBLANES:] + i2[0:1])
        pos.append((row8f + float(SUBLANES)) * float(PEER_TOPK))
        cand = jnp.concatenate(cand, axis=0)
        eid = jnp.concatenate(eid, axis=0)
        pos = jnp.concatenate(pos, axis=0)
        top = jnp.zeros((PEER_TOPK, tt), F32)
        idx = jnp.zeros((PEER_TOPK, tt), F32)
        for k in range(PEER_TOPK):
            m = jnp.max(cand, axis=0, keepdims=True)
            p = jnp.min(jnp.where(cand == m, pos, 1e9), axis=0, keepdims=True)
            hit = pos == p
            e = jnp.max(jnp.where(hit, eid, -1.0), axis=0, keepdims=True)
            cand = jnp.where(hit, ninf, cand)
            top = jnp.where(row16 == k, m, top)
            idx = jnp.where(row16 == k, e, idx)
        w = jnp.exp(top - top[0:1])
        out = pl.ds(pl.multiple_of(h * PEER_TOPK, PEER_TOPK), PEER_TOPK)
        idx_ref[out, :] = idx.astype(jnp.int32)
        gate_ref[out, :] = w / jnp.sum(w, axis=0, keepdims=True)
        return carry

    unroll = 4

    def heads(hh, carry):
        for u in range(unroll):
            head(unroll * hh + u, carry)
        return carry

    lax.fori_loop(0, PEER_HEADS // unroll, heads, 0)


def peer_topk(scores, first_token, n, *, tt=256):
    rows = scores.shape[0]
    b0 = first_token // tt
    idx_t, gate_t = pl.pallas_call(
        _peer_topk_kernel,
        grid=(n // tt,),
        in_specs=[pl.BlockSpec((rows, tt), lambda i: (0, b0 + i))],
        out_specs=[pl.BlockSpec((PEER_PICKS, tt), lambda i: (0, i)),
                   pl.BlockSpec((PEER_PICKS, tt), lambda i: (0, i))],
        out_shape=[jax.ShapeDtypeStruct((PEER_PICKS, n), jnp.int32),
                   jax.ShapeDtypeStruct((PEER_PICKS, n), F32)],
        compiler_params=pltpu.CompilerParams(
            dimension_semantics=("arbitrary",), vmem_limit_bytes=VMEM_LIMIT),
        name="peer_topk",
    )(scores)
    return idx_t.T, gate_t.T


def pack_expert_table(down, up):
    e, d = down.shape
    rows = min(PACK_ROWS, e)
    packed = pl.pallas_call(
        functools.partial(_pack_kernel, rows=rows),
        grid=(e // rows,),
        in_specs=[pl.BlockSpec((rows, d), lambda i: (i, 0)),
                  pl.BlockSpec((rows, d), lambda i: (i, 0))],
        out_specs=pl.BlockSpec(memory_space=pl.ANY),
        out_shape=jax.ShapeDtypeStruct((e, d // LANES, 1, LANES), jnp.int32),
        scratch_shapes=[pltpu.VMEM((d // LANES, rows, LANES), jnp.int32),
                        pltpu.SemaphoreType.DMA(())],
        compiler_params=pltpu.CompilerParams(
            dimension_semantics=("arbitrary",), vmem_limit_bytes=VMEM_LIMIT),
        name="pack_experts",
    )(down, up)
    return packed.reshape(e, d // LANES, LANES)


PACK_ROWS = 512


def _pack_kernel(down_ref, up_ref, out_ref, stage_ref, sem_ref, *, rows):
    step = pl.program_id(0)
    quarter = HALF // LANES

    def words(t):
        lo = pltpu.bitcast(t[:, :HALF].astype(BF16).astype(F32), jnp.uint32)
        hi = pltpu.bitcast(t[:, HALF:].astype(BF16).astype(F32), jnp.uint32)
        return pltpu.bitcast((hi & jnp.uint32(0xFFFF0000)) | (lo >> 16), jnp.int32)

    for k, ref in enumerate((down_ref, up_ref)):
        w = words(ref[...])
        for c in range(quarter):
            stage_ref[k * quarter + c] = w[:, c * LANES:(c + 1) * LANES]
    for r in range(rows):
        pltpu.make_async_copy(stage_ref.at[:, pl.ds(r, 1), :], out_ref.at[step * rows + r],
                              sem_ref).start(priority=r % 2)
    pltpu.make_async_copy(stage_ref, stage_ref, sem_ref).wait()


def _row_copy(tab_ref, buf_ref, sem_ref, expert, slot, row):
    return pltpu.make_async_copy(tab_ref.at[expert],
                                 buf_ref.at[slot, :, pl.ds(row, 1), :], sem_ref.at[slot])


APPLY_TOKENS = 8
CHUNKS = D_MODEL // LANES
STEP_TOKENS = 4 * APPLY_TOKENS


def _staged_words(st_ref, group, t):
    base = (group * APPLY_TOKENS + t) * PEER_PICKS * CHUNKS
    return [st_ref[pl.ds(base + c, PEER_PICKS, stride=CHUNKS), :] for c in range(CHUNKS)]


def _apply_groups(hn_ref, gate_ref, res_ref, o_ref, groups, between):
    tb = APPLY_TOKENS
    lane_even = (lax.broadcasted_iota(jnp.int32, (1, LANES), 1) % 2) == 0
    even2 = jnp.concatenate([lane_even, lane_even], axis=1)
    tok_of_row = lax.broadcasted_iota(jnp.int32, (2 * tb, 1), 0) % tb
    xxs, rsels, tbls = [], [], []
    for first, _ in groups:
        x = hn_ref[pl.ds(first, tb), :]
        xxs.append(jnp.concatenate([x[:, :HALF], x[:, HALF:]], axis=0).astype(BF16))
        rsels.append(jnp.zeros((2 * tb, 2 * PEER_PICKS), F32))
        tbls.append([])
    for t in range(tb):
        for g, (_, words_fn) in enumerate(groups):
            words = jnp.concatenate(words_fn(t), axis=1)
            tbl = pltpu.bitcast(words, BF16)
            tbls[g].append(tbl)
            rsels[g] = rsels[g] + jnp.where(tok_of_row == t, _dot_nt(xxs[g], tbl[:, :HALF]), 0.0)
        between(t)
    for g, (first, _) in enumerate(groups):
        tok = pl.ds(first, tb)
        coefs = []
        for c in range(2 * PEER_PICKS // LANES):
            cols = slice(c * LANES, (c + 1) * LANES)
            part = jnp.where(lane_even, rsels[g][:tb, cols], rsels[g][tb:, cols])
            hid = part + jnp.where(lane_even, pltpu.roll(part, LANES - 1, 1), pltpu.roll(part, 1, 1))
            coefs.append(gate_ref[tok, cols] * _gelu(hid))
        coef = jnp.concatenate(coefs, axis=1)
        cc = jnp.concatenate([jnp.where(even2, coef, 0.0), jnp.where(even2, 0.0, coef)],
                             axis=0).astype(BF16)
        ysel = jnp.zeros((2 * tb, HALF), F32)
        for t in range(tb):
            ysel = ysel + jnp.where(
                tok_of_row == t, jnp.dot(cc, tbls[g][t][:, HALF:], preferred_element_type=F32), 0.0)
        y = jnp.concatenate([ysel[:tb], ysel[tb:]], axis=1)
        o_ref[tok, :] = res_ref[tok, :] + y


def _peer_apply_kernel(idx0_ref, idx1_ref, hn_ref, gate_ref, res_ref, tab_ref, st_ref, o_ref,
                       buf_ref, sem_ref):
    step = pl.program_id(0)
    nsteps = pl.num_programs(0)
    tb = APPLY_TOKENS

    def issue_token(idx_ref, first_token, slot, t):
        for j in range(PEER_PICKS):
            _row_copy(tab_ref, buf_ref, sem_ref, idx_ref[first_token + t, j], slot,
                      t * PEER_PICKS + j).start(priority=j % 2)

    def wait(slot):
        pltpu.make_async_copy(buf_ref.at[slot], buf_ref.at[slot], sem_ref.at[slot]).wait()

    def copied_words(slot, t):
        return [buf_ref[slot, c, pl.ds(t * PEER_PICKS, PEER_PICKS), :] for c in range(CHUNKS)]

    compute = functools.partial(_apply_groups, hn_ref, gate_ref, res_ref, o_ref)

    @pl.when(step == 0)
    def _():
        for t in range(tb):
            issue_token(idx0_ref, 0, 0, t)

    wait(0)
    compute([(0, functools.partial(copied_words, 0)),
             (2 * tb, functools.partial(_staged_words, st_ref, 0))],
            functools.partial(issue_token, idx0_ref, tb, 1))
    wait(1)
    compute([(tb, functools.partial(copied_words, 1)),
             (3 * tb, functools.partial(_staged_words, st_ref, 1))],
            functools.partial(issue_token, idx1_ref, 0, 0))

    @pl.when(step == nsteps - 1)
    def _():
        wait(0)


STAGED_BLOCK_ROWS = 2 * APPLY_TOKENS * PEER_PICKS * CHUNKS


def peer_apply(idx, hn, gate2, acc, table, staged, first_token, ntokens):
    n, d = hn.shape
    tb = STEP_TOKENS
    nsteps = ntokens // tb
    s0 = first_token // tb
    smem = pltpu.SMEM
    return pl.pallas_call(
        _peer_apply_kernel,
        grid=(nsteps,),
        in_specs=[
            pl.BlockSpec((tb, PEER_PICKS), lambda i: (s0 + i, 0), memory_space=smem),
            pl.BlockSpec((tb, PEER_PICKS), lambda i: (s0 + jnp.minimum(i + 1, nsteps - 1), 0),
                         memory_space=smem),
            pl.BlockSpec((tb, d), lambda i: (s0 + i, 0)),
            pl.BlockSpec((tb, 2 * PEER_PICKS), lambda i: (s0 + i, 0)),
            pl.BlockSpec((tb, d), lambda i: (s0 + i, 0)),
            pl.BlockSpec(memory_space=pl.ANY),
            pl.BlockSpec((STAGED_BLOCK_ROWS, LANES), lambda i: (i, 0)),
        ],
        out_specs=pl.BlockSpec((tb, d), lambda i: (s0 + i, 0)),
        out_shape=jax.ShapeDtypeStruct((n, d), F32),
        input_output_aliases={4: 0},
        scratch_shapes=[pltpu.VMEM((2, CHUNKS, APPLY_TOKENS * PEER_PICKS, LANES), jnp.int32),
                        pltpu.SemaphoreType.DMA((2,))],
        compiler_params=pltpu.CompilerParams(
            dimension_semantics=("arbitrary",), vmem_limit_bytes=VMEM_LIMIT),
        name="peer_apply",
    )(idx, idx, hn, gate2, acc, table.reshape(table.shape[0], CHUNKS, 1, LANES), staged)


SC_CORES = 2
SC_SUBCORES = 16
SC_CHUNK = 32


def sc_gather_rows(table, idx):
    b = idx.shape[0]
    nw = SC_CORES * SC_SUBCORES
    per_w = b // nw
    nchunks = per_w // SC_CHUNK
    assert per_w * nw == b and nchunks * SC_CHUNK == per_w and nchunks % 2 == 0
    mesh = plsc.VectorSubcoreMesh(core_axis_name="c", subcore_axis_name="s")

    @functools.partial(
        pl.kernel, mesh=mesh,
        out_type=jax.ShapeDtypeStruct((b,) + table.shape[1:], table.dtype),
        scratch_types=[
            pltpu.VMEM((per_w,), jnp.int32),
            pltpu.VMEM((SC_CHUNK,) + table.shape[1:], table.dtype),
            pltpu.VMEM((SC_CHUNK,) + table.shape[1:], table.dtype),
            pltpu.SemaphoreType.DMA,
            pltpu.SemaphoreType.DMA,
        ],
        name="sc_gather_rows",
    )
    def gather(tab_hbm, idx_hbm, out_hbm, idx_v, rows0, rows1, sem0, sem1):
        wid = lax.axis_index("s") * SC_CORES + lax.axis_index("c")
        base = wid * per_w
        pltpu.sync_copy(idx_hbm.at[pl.ds(base, per_w)], idx_v)

        def start(chunk, rows, sem):
            off = pl.multiple_of(chunk * SC_CHUNK, SC_CHUNK)
            pltpu.async_copy(tab_hbm.at[idx_v.at[pl.ds(off, SC_CHUNK)]], rows, sem)

        def finish(chunk, rows, sem):
            off = pl.multiple_of(chunk * SC_CHUNK, SC_CHUNK)
            pltpu.make_async_copy(tab_hbm.at[idx_v.at[pl.ds(off, SC_CHUNK)]], rows, sem).wait()
            pltpu.sync_copy(rows, out_hbm.at[pl.ds(base + off, SC_CHUNK)])

        @pl.loop(0, nchunks, step=2)
        def _(i):
            start(i, rows0, sem0)
            finish(i, rows0, sem0)
            start(i + 1, rows1, sem1)
            finish(i + 1, rows1, sem1)

    return gather(table, idx)


PEER_GROUP = 4096
PEER_FIRST_GROUP = 1024


def _group_bounds(n):
    first = min(PEER_FIRST_GROUP, n)
    bounds = [0, first] + ([2 * first] if n >= 2 * first else [])
    while bounds[-1] < n:
        bounds.append(min(bounds[-1] + PEER_GROUP, n))
    return bounds


def peer_layer(x, norm_g, w_query, sub_keys, table):
    n = x.shape[0]
    hn, scores = peer_scores(x, norm_g, w_query, sub_keys)
    bounds = _group_bounds(n)
    ngroups = len(bounds) - 1

    def staged_indices(src, lo, hi):
        steps = src[lo:hi].reshape((hi - lo) // STEP_TOKENS, STEP_TOKENS, PEER_PICKS)
        return steps[:, STEP_TOKENS // 2:].reshape(-1)

    idx0, gate0 = peer_topk(scores, 0, bounds[1])
    first = staged_indices(idx0, 0, bounds[1])
    if ngroups > 1:
        scores, first = lax.optimization_barrier((scores, first))
        idx1, gate1 = peer_topk(scores, bounds[1], n - bounds[1])
        idx, gate = jnp.concatenate([idx0, idx1]), jnp.concatenate([gate0, gate1])
    else:
        idx, gate = idx0, gate0
    gate2 = jnp.repeat(gate, 2, axis=1)

    acc = x
    pending = [first] + [staged_indices(idx, bounds[g], bounds[g + 1]) for g in range(1, min(2, ngroups))]
    for g in range(ngroups):
        staged = sc_gather_rows(table, pending.pop(0)).reshape(-1, LANES)
        acc = peer_apply(idx, hn, gate2, acc, table, staged, bounds[g], bounds[g + 1] - bounds[g])
        if g + 2 < ngroups:
            nxt, acc = lax.optimization_barrier(
                (staged_indices(idx, bounds[g + 2], bounds[g + 3]), acc))
            pending.append(nxt)
    return acc


def attention_layer(x, batch, seq, norm_g, w_in, a_q_gain, a_k_gain, b_q_gain, b_k_gain,
                    lam_q1, lam_k1, lam_q2, lam_k2, b_sub_gain, w_out, lambda_init):
    proj = norm_matmul(x, norm_g, w_in)
    na = A_HEADS // 2
    tile2 = lambda g: jnp.concatenate([g, g])
    out_a = pair_attention(proj, batch, seq, "dilated", 0, na, 2 * na, na,
                           tile2(a_q_gain), tile2(a_k_gain))
    out_b = pair_attention(proj, batch, seq, "diff", 3 * na, 3 * na + B_HEADS, 3 * na + 2 * B_HEADS,
                           B_HEADS, b_q_gain.reshape(-1), b_k_gain.reshape(-1),
                           extras=(lam_q1, lam_k1, lam_q2, lam_k2, b_sub_gain),
                           lambda_init=lambda_init)
    return out_proj(x, out_a, out_b, w_out)


def kernel(x, attn_norm_g, attn_w_in, a_q_gain, a_k_gain, b_q_gain, b_k_gain, lam_q1, lam_k1,
           lam_q2, lam_k2, b_sub_gain, attn_w_out, sgu_norm_g, sgu_w_in, sgu_v_gain, sgu_w_spatial,
           sgu_b_spatial, sgu_w_out, ffn_norm_g, peer_w_query, peer_sub_keys, peer_down, peer_up):
    batch, seq, d = x.shape
    depth = ffn_norm_g.shape[0]
    h = x.reshape(batch * seq, d)
    tables = [pack_expert_table(peer_down[layer], peer_up[layer]) for layer in range(depth)]
    for layer in range(depth):
        i = layer // 2
        if layer % 2 == 0:
            lambda_init = 0.8 - 0.6 * math.exp(-0.3 * layer)
            h = attention_layer(h, batch, seq, attn_norm_g[i], attn_w_in[i], a_q_gain[i],
                                a_k_gain[i], b_q_gain[i], b_k_gain[i], lam_q1[i], lam_k1[i],
                                lam_q2[i], lam_k2[i], b_sub_gain[i], attn_w_out[i], lambda_init)
        else:
            h = sgu_layer(h, sgu_norm_g[i], sgu_w_in[i], sgu_v_gain[i], sgu_w_spatial[i],
                          sgu_b_spatial[i], sgu_w_out[i])
        h = peer_layer(h, ffn_norm_g[layer], peer_w_query[layer], peer_sub_keys[layer],
                       tables[layer])
    return h.reshape(batch, seq, d)
```

```python
import functools
import math

import jax
import jax.numpy as jnp
from jax import lax
from jax.experimental import pallas as pl
from jax.experimental.pallas import tpu as pltpu
from jax.experimental.pallas import tpu_sc as plsc

D_MODEL = 1024
HEAD_DIM = 64
EPS = 1e-6
NEG = -1e30
A_HEADS = 8
B_HEADS = 4
C_CHUNK = 128
C_GROUPS = 8
C_WIDTH = 2 * D_MODEL
C_GROUP_DIM = C_WIDTH // C_GROUPS
PEER_HEADS = 8
PEER_NKEYS = 128
PEER_TOPK = 16
PEER_PICKS = PEER_HEADS * PEER_TOPK

LANES = 128
HALF = D_MODEL // 2
VMEM_LIMIT = 56 * 1024 * 1024

BF16 = jnp.bfloat16
F32 = jnp.float32


def _gelu(x):
    return 0.5 * x * (1.0 + jnp.tanh(math.sqrt(2.0 / math.pi) * (x + 0.044715 * (x * x * x))))


def _rms(x, g):
    return x * lax.rsqrt(jnp.mean(x * x, axis=-1, keepdims=True) + EPS) * g


def _dot_nt(a, b):
    return lax.dot_general(a, b, (((1,), (1,)), ((), ())), preferred_element_type=F32)


def _norm_matmul_kernel(x_ref, g_ref, w_ref, o_ref, xn_ref):
    @pl.when(pl.program_id(1) == 0)
    def _():
        xn_ref[...] = _rms(x_ref[...], g_ref[...]).astype(BF16)

    o_ref[...] = jnp.dot(xn_ref[...], w_ref[...], preferred_element_type=F32)


def norm_matmul(x, g, w, *, tm=1024, tn=1024):
    n, d = x.shape
    nout = w.shape[1]
    tm = min(tm, n)
    return pl.pallas_call(
        _norm_matmul_kernel,
        grid=(n // tm, nout // tn),
        in_specs=[
            pl.BlockSpec((tm, d), lambda i, j: (i, 0)),
            pl.BlockSpec((1, d), lambda i, j: (0, 0)),
            pl.BlockSpec((d, tn), lambda i, j: (0, j)),
        ],
        out_specs=pl.BlockSpec((tm, tn), lambda i, j: (i, j)),
        out_shape=jax.ShapeDtypeStruct((n, nout), F32),
        scratch_shapes=[pltpu.VMEM((tm, d), BF16)],
        compiler_params=pltpu.CompilerParams(
            dimension_semantics=("arbitrary", "arbitrary"), vmem_limit_bytes=VMEM_LIMIT),
        name="norm_matmul",
    )(x, g.reshape(1, d), w.astype(BF16))


def _pair_norm(t, gain, lo):
    sq = t * t
    s_lo = jnp.sum(jnp.where(lo, sq, 0.0), axis=-1, keepdims=True)
    s_hi = jnp.sum(jnp.where(lo, 0.0, sq), axis=-1, keepdims=True)
    ms = jnp.where(lo, s_lo, s_hi) * (1.0 / HEAD_DIM)
    return t * lax.rsqrt(ms + EPS) * gain


def _attn_kernel(*refs, mode, tq, tk, lambda_init):
    if mode == "dilated":
        q_ref, k_ref, v_ref, qg_ref, kg_ref, o_ref, kn_ref = refs
    else:
        (q_ref, k_ref, v_ref, qg_ref, kg_ref, lq1_ref, lk1_ref, lq2_ref, lk2_ref, sg_ref,
         o_ref, kn_ref) = refs
    i = pl.program_id(2)
    lo = lax.broadcasted_iota(jnp.int32, (1, LANES), 1) < HEAD_DIM

    @pl.when(i == 0)
    def _():
        kn_ref[...] = _pair_norm(k_ref[...], kg_ref[...], lo).astype(BF16)

    qn = _pair_norm(q_ref[...], qg_ref[...], lo) * (HEAD_DIM ** -0.5)
    qa = jnp.where(lo, qn, 0.0).astype(BF16)
    qb = jnp.where(lo, 0.0, qn).astype(BF16)
    assert tq == tk and tk % 16 == 0
    row = lax.broadcasted_iota(jnp.int32, (tq, tk), 0)
    col = lax.broadcasted_iota(jnp.int32, (tq, tk), 1)

    def step(j, carry, valid, weight):
        ma, la, acca, mb, lb, accb = carry
        off = pl.multiple_of(j * tk, tk)
        kb = kn_ref[pl.ds(off, tk), :]
        vb = v_ref[pl.ds(off, tk), :].astype(BF16)

        def update(qh, m, l, acc):
            s = _dot_nt(qh, kb)
            if valid is not None:
                s = jnp.where(valid, s, NEG)
            m_new = jnp.maximum(m, jnp.max(s, axis=-1, keepdims=True))
            alpha = jnp.exp(m - m_new)
            p = jnp.exp(s - m_new)
            if weight is not None:
                p = weight * p
            l_new = alpha * l + jnp.sum(p, axis=-1, keepdims=True)
            acc_new = alpha * acc + jnp.dot(p.astype(BF16), vb, preferred_element_type=F32)
            return m_new, l_new, acc_new

        ma, la, acca = update(qa, ma, la, acca)
        mb, lb, accb = update(qb, mb, lb, accb)
        return ma, la, acca, mb, lb, accb

    m0 = jnp.full((tq, 1), NEG, F32)
    l0 = jnp.zeros((tq, 1), F32)
    a0 = jnp.zeros((tq, LANES), F32)
    carry = (m0, l0, a0, m0, l0, a0)
    if mode == "dilated":
        near = (512 + tk - 1) // tk + 1
        first_near = jnp.maximum(i - (near - 1), 0)
        far_valid = ((row - col) & 15) == 0
        carry = lax.fori_loop(0, first_near, lambda j, c: step(j, c, far_valid, None), carry)

        def near_step(j, c):
            dist = (i - j) * tk + row - col
            cnt = ((dist <= 128).astype(F32)
                   + ((dist <= 512) & ((dist & 3) == 0)).astype(F32)
                   + ((dist & 15) == 0).astype(F32))
            cnt = jnp.where(dist >= 0, cnt, 0.0)
            return step(j, c, cnt > 0.0, cnt)

        carry = lax.fori_loop(first_near, i + 1, near_step, carry)
    else:
        carry = lax.fori_loop(0, i, lambda j, c: step(j, c, None, None), carry)
        carry = step(i, carry, col <= row, None)
    ma, la, acca, mb, lb, accb = carry
    oa = acca / la
    ob = accb / lb
    if mode == "dilated":
        o_ref[...] = jnp.where(lo, oa, ob)
    else:
        lam = (jnp.exp(jnp.sum(lq1_ref[...] * lk1_ref[...], axis=-1, keepdims=True))
               - jnp.exp(jnp.sum(lq2_ref[...] * lk2_ref[...], axis=-1, keepdims=True))
               + lambda_init)
        o = oa - lam * ob
        o_ref[...] = _rms(o, sg_ref[...]) * (1.0 - lambda_init)


def pair_attention(proj, batch, seq, mode, qcol, kcol, vcol, npairs, q_gain, k_gain, extras=(),
                   lambda_init=0.0, tq=512, tk=512):
    nq = seq // tq
    small = [q_gain.reshape(1, LANES), k_gain.reshape(1, LANES)] + [e.reshape(1, -1) for e in extras]
    small_specs = [pl.BlockSpec(s.shape, lambda b, p, i: (0, 0)) for s in small]
    kernel = functools.partial(_attn_kernel, mode=mode, tq=tq, tk=tk, lambda_init=lambda_init)
    return pl.pallas_call(
        kernel,
        grid=(batch, npairs, nq),
        in_specs=[
            pl.BlockSpec((tq, LANES), lambda b, p, i: (b * nq + i, qcol + p)),
            pl.BlockSpec((seq, LANES), lambda b, p, i: (b, kcol + p)),
            pl.BlockSpec((seq, LANES), lambda b, p, i: (b, vcol + p)),
        ] + small_specs,
        out_specs=pl.BlockSpec((tq, LANES), lambda b, p, i: (b * nq + i, p)),
        out_shape=jax.ShapeDtypeStruct((batch * seq, npairs * LANES), F32),
        scratch_shapes=[pltpu.VMEM((seq, LANES), BF16)],
        compiler_params=pltpu.CompilerParams(
            dimension_semantics=("arbitrary", "arbitrary", "arbitrary"),
            vmem_limit_bytes=VMEM_LIMIT),
        name="attn_" + mode,
    )(proj, proj, proj, *small)


def _out_proj_kernel(x_ref, a_ref, b_ref, wa_ref, wb_ref, o_ref):
    o_ref[...] = (x_ref[...]
                  + jnp.dot(a_ref[...].astype(BF16), wa_ref[...], preferred_element_type=F32)
                  + jnp.dot(b_ref[...].astype(BF16), wb_ref[...], preferred_element_type=F32))


def out_proj(x, a, b, w, *, tm=512):
    n, d = x.shape
    ka, kb = a.shape[1], b.shape[1]
    w = w.astype(BF16)
    tm = min(tm, n)
    return pl.pallas_call(
        _out_proj_kernel,
        grid=(n // tm,),
        in_specs=[
            pl.BlockSpec((tm, d), lambda i: (i, 0)),
            pl.BlockSpec((tm, ka), lambda i: (i, 0)),
            pl.BlockSpec((tm, kb), lambda i: (i, 0)),
            pl.BlockSpec((ka, d), lambda i: (0, 0)),
            pl.BlockSpec((kb, d), lambda i: (0, 0)),
        ],
        out_specs=pl.BlockSpec((tm, d), lambda i: (i, 0)),
        out_shape=jax.ShapeDtypeStruct((n, d), F32),
        compiler_params=pltpu.CompilerParams(
            dimension_semantics=("arbitrary",), vmem_limit_bytes=VMEM_LIMIT),
        name="out_proj",
    )(x, a, b, w[:ka], w[ka:])


def _sgu_kernel(x_ref, g_ref, win_ref, vg_ref, ws_ref, bs_ref, wout_ref, o_ref, gated_ref, *, tm):
    x = x_ref[...]
    xn = _rms(x, g_ref[...]).astype(BF16)
    z = _gelu(jnp.dot(xn, win_ref[...], preferred_element_type=F32))
    u = z[:, :C_WIDTH]
    v = _rms(z[:, C_WIDTH:], vg_ref[...]).astype(BF16)
    r = lax.broadcasted_iota(jnp.int32, (C_CHUNK, C_CHUNK), 0)
    c = lax.broadcasted_iota(jnp.int32, (C_CHUNK, C_CHUNK), 1)
    causal = c <= r
    for grp in range(C_GROUPS):
        ws = jnp.where(causal, ws_ref[grp], 0.0).astype(BF16)
        bias = bs_ref[:, grp:grp + 1]
        cols = slice(grp * C_GROUP_DIM, (grp + 1) * C_GROUP_DIM)
        for ch in range(tm // C_CHUNK):
            rws = slice(ch * C_CHUNK, (ch + 1) * C_CHUNK)
            gate = jnp.dot(ws, v[rws, cols], preferred_element_type=F32) + bias
            gated_ref[rws, cols] = (u[rws, cols] * gate).astype(BF16)
    o_ref[...] = x + jnp.dot(gated_ref[...], wout_ref[...], preferred_element_type=F32)


def sgu_layer(x, norm_g, w_in, v_gain, w_spatial, b_spatial, w_out, *, tm=256):
    n, d = x.shape
    kernel = functools.partial(_sgu_kernel, tm=tm)
    return pl.pallas_call(
        kernel,
        grid=(n // tm,),
        in_specs=[
            pl.BlockSpec((tm, d), lambda i: (i, 0)),
            pl.BlockSpec((1, d), lambda i: (0, 0)),
            pl.BlockSpec((d, 2 * C_WIDTH), lambda i: (0, 0)),
            pl.BlockSpec((1, C_WIDTH), lambda i: (0, 0)),
            pl.BlockSpec((C_GROUPS, C_CHUNK, C_CHUNK), lambda i: (0, 0, 0)),
            pl.BlockSpec((C_CHUNK, C_GROUPS), lambda i: (0, 0)),
            pl.BlockSpec((C_WIDTH, d), lambda i: (0, 0)),
        ],
        out_specs=pl.BlockSpec((tm, d), lambda i: (i, 0)),
        out_shape=jax.ShapeDtypeStruct((n, d), F32),
        scratch_shapes=[pltpu.VMEM((tm, C_WIDTH), BF16)],
        compiler_params=pltpu.CompilerParams(
            dimension_semantics=("arbitrary",), vmem_limit_bytes=VMEM_LIMIT),
        name="sgu",
    )(x, norm_g.reshape(1, d), w_in.astype(BF16), v_gain.reshape(1, C_WIDTH), w_spatial,
      b_spatial.T, w_out.astype(BF16))


def _peer_scores_kernel(x_ref, g_ref, wq_ref, sk_ref, hn_ref, sc_ref):
    hn = _rms(x_ref[...], g_ref[...])
    hn_ref[...] = hn
    q = jnp.dot(hn.astype(BF16), wq_ref[...], preferred_element_type=F32)
    for hp in range(2 * PEER_HEADS):
        cols = slice(hp * PEER_NKEYS, (hp + 1) * PEER_NKEYS)
        sc_ref[cols, :] = _dot_nt(sk_ref[hp].astype(BF16), q[:, cols].astype(BF16))


def peer_scores(x, norm_g, w_query, sub_keys, *, tm=512):
    n, d = x.shape
    nq = w_query.shape[1]
    tm = min(tm, n)
    sk = sub_keys.reshape(2 * PEER_HEADS, PEER_NKEYS, PEER_NKEYS)
    return pl.pallas_call(
        _peer_scores_kernel,
        grid=(n // tm,),
        in_specs=[
            pl.BlockSpec((tm, d), lambda i: (i, 0)),
            pl.BlockSpec((1, d), lambda i: (0, 0)),
            pl.BlockSpec((d, nq), lambda i: (0, 0)),
            pl.BlockSpec(sk.shape, lambda i: (0, 0, 0)),
        ],
        out_specs=[pl.BlockSpec((tm, d), lambda i: (i, 0)),
                   pl.BlockSpec((nq, tm), lambda i: (0, i))],
        out_shape=[jax.ShapeDtypeStruct((n, d), F32), jax.ShapeDtypeStruct((nq, n), F32)],
        compiler_params=pltpu.CompilerParams(
            dimension_semantics=("arbitrary",), vmem_limit_bytes=VMEM_LIMIT),
        name="peer_scores",
    )(x, norm_g.reshape(1, d), w_query.astype(BF16), sk)


SUBLANES = 8


def _peer_topk_kernel(sc_ref, idx_ref, gate_ref):
    tt = sc_ref.shape[1]
    key = lax.broadcasted_iota(jnp.int32, (PEER_NKEYS, tt), 0).astype(F32)
    row16 = lax.broadcasted_iota(jnp.int32, (PEER_TOPK, tt), 0)
    row8 = lax.broadcasted_iota(jnp.int32, (SUBLANES, tt), 0)
    row8f = row8.astype(F32)
    ninf = jnp.float32(-jnp.inf)

    def extract16(s):
        vals = jnp.zeros((PEER_TOPK, tt), F32)
        ids = jnp.zeros((PEER_TOPK, tt), F32)
        for k in range(PEER_TOPK):
            m = jnp.max(s, axis=0, keepdims=True)
            am = jnp.min(jnp.where(s == m, key, float(PEER_NKEYS)), axis=0, keepdims=True)
            s = jnp.where(key == am, ninf, s)
            vals = jnp.where(row16 == k, m, vals)
            ids = jnp.where(row16 == k, am, ids)
        return vals, ids

    def head(h, carry):
        off = pl.multiple_of(h * 2 * PEER_NKEYS, 2 * PEER_NKEYS)
        v1, i1 = extract16(sc_ref[pl.ds(off, PEER_NKEYS), :])
        v2, i2 = extract16(sc_ref[pl.ds(off + PEER_NKEYS, PEER_NKEYS), :])
        e1 = i1 * float(PEER_NKEYS)
        cand = [v1[0:1] + v2]
        eid = [e1[0:1] + i2]
        pos = [row16.astype(F32)]
        for i in range(1, SUBLANES):
            keep = row8 < (PEER_TOPK // (i + 1))
            cand.append(jnp.where(keep, v1[i:i + 1] + v2[0:SUBLANES], ninf))
            eid.append(e1[i:i + 1] + i2[0:SUBLANES])
            pos.append(row8f + float(i * PEER_TOPK))
        cand.append(v1[SUBLANES:] + v2[0:1])
        eid.append(e1[SUBLANES:] + i2[0:1])
        pos.append((row8f + float(SUBLANES)) * float(PEER_TOPK))
        cand = jnp.concatenate(cand, axis=0)
        eid = jnp.concatenate(eid, axis=0)
        pos = jnp.concatenate(pos, axis=0)
        top = jnp.zeros((PEER_TOPK, tt), F32)
        idx = jnp.zeros((PEER_TOPK, tt), F32)
        for k in range(PEER_TOPK):
            m = jnp.max(cand, axis=0, keepdims=True)
            p = jnp.min(jnp.where(cand == m, pos, 1e9), axis=0, keepdims=True)
            hit = pos == p
            e = jnp.max(jnp.where(hit, eid, -1.0), axis=0, keepdims=True)
            cand = jnp.where(hit, ninf, cand)
            top = jnp.where(row16 == k, m, top)
            idx = jnp.where(row16 == k, e, idx)
        w = jnp.exp(top - top[0:1])
        out = pl.ds(pl.multiple_of(h * PEER_TOPK, PEER_TOPK), PEER_TOPK)
        idx_ref[out, :] = idx.astype(jnp.int32)
        gate_ref[out, :] = w / jnp.sum(w, axis=0, keepdims=True)
        return carry

    unroll = 4

    def heads(hh, carry):
        for u in range(unroll):
            head(unroll * hh + u, carry)
        return carry

    lax.fori_loop(0, PEER_HEADS // unroll, heads, 0)


def peer_topk(scores, first_token, n, *, tt=256):
    rows = scores.shape[0]
    b0 = first_token // tt
    idx_t, gate_t = pl.pallas_call(
        _peer_topk_kernel,
        grid=(n // tt,),
        in_specs=[pl.BlockSpec((rows, tt), lambda i: (0, b0 + i))],
        out_specs=[pl.BlockSpec((PEER_PICKS, tt), lambda i: (0, i)),
                   pl.BlockSpec((PEER_PICKS, tt), lambda i: (0, i))],
        out_shape=[jax.ShapeDtypeStruct((PEER_PICKS, n), jnp.int32),
                   jax.ShapeDtypeStruct((PEER_PICKS, n), F32)],
        compiler_params=pltpu.CompilerParams(
            dimension_semantics=("arbitrary",), vmem_limit_bytes=VMEM_LIMIT),
        name="peer_topk",
    )(scores)
    return idx_t.T, gate_t.T


def pack_expert_table(down, up):
    e, d = down.shape
    rows = min(PACK_ROWS, e)
    packed = pl.pallas_call(
        functools.partial(_pack_kernel, rows=rows),
        grid=(e // rows,),
        in_specs=[pl.BlockSpec((rows, d), lambda i: (i, 0)),
                  pl.BlockSpec((rows, d), lambda i: (i, 0))],
        out_specs=pl.BlockSpec(memory_space=pl.ANY),
        out_shape=jax.ShapeDtypeStruct((e, d // LANES, 1, LANES), jnp.int32),
        scratch_shapes=[pltpu.VMEM((d // LANES, rows, LANES), jnp.int32),
                        pltpu.SemaphoreType.DMA(())],
        compiler_params=pltpu.CompilerParams(
            dimension_semantics=("arbitrary",), vmem_limit_bytes=VMEM_LIMIT),
        name="pack_experts",
    )(down, up)
    return packed.reshape(e, d // LANES, LANES)


PACK_ROWS = 512


def _pack_kernel(down_ref, up_ref, out_ref, stage_ref, sem_ref, *, rows):
    step = pl.program_id(0)
    quarter = HALF // LANES

    def words(t):
        lo = pltpu.bitcast(t[:, :HALF].astype(BF16).astype(F32), jnp.uint32)
        hi = pltpu.bitcast(t[:, HALF:].astype(BF16).astype(F32), jnp.uint32)
        return pltpu.bitcast((hi & jnp.uint32(0xFFFF0000)) | (lo >> 16), jnp.int32)

    for k, ref in enumerate((down_ref, up_ref)):
        w = words(ref[...])
        for c in range(quarter):
            stage_ref[k * quarter + c] = w[:, c * LANES:(c + 1) * LANES]
    for r in range(rows):
        pltpu.make_async_copy(stage_ref.at[:, pl.ds(r, 1), :], out_ref.at[step * rows + r],
                              sem_ref).start(priority=r % 2)
    pltpu.make_async_copy(stage_ref, stage_ref, sem_ref).wait()


def _row_copy(tab_ref, buf_ref, sem_ref, expert, slot, row):
    return pltpu.make_async_copy(tab_ref.at[expert],
                                 buf_ref.at[slot, :, pl.ds(row, 1), :], sem_ref.at[slot])


APPLY_TOKENS = 8
CHUNKS = D_MODEL // LANES
STEP_TOKENS = 4 * APPLY_TOKENS


def _staged_words(st_ref, group, t):
    base = (group * APPLY_TOKENS + t) * PEER_PICKS * CHUNKS
    return [st_ref[pl.ds(base + c, PEER_PICKS, stride=CHUNKS), :] for c in range(CHUNKS)]


def _apply_groups(hn_ref, gate_ref, res_ref, o_ref, groups, between):
    tb = APPLY_TOKENS
    lane_even = (lax.broadcasted_iota(jnp.int32, (1, LANES), 1) % 2) == 0
    even2 = jnp.concatenate([lane_even, lane_even], axis=1)
    tok_of_row = lax.broadcasted_iota(jnp.int32, (2 * tb, 1), 0) % tb
    xxs, rsels, tbls = [], [], []
    for first, _ in groups:
        x = hn_ref[pl.ds(first, tb), :]
        xxs.append(jnp.concatenate([x[:, :HALF], x[:, HALF:]], axis=0).astype(BF16))
        rsels.append(jnp.zeros((2 * tb, 2 * PEER_PICKS), F32))
        tbls.append([])
    for t in range(tb):
        for g, (_, words_fn) in enumerate(groups):
            words = jnp.concatenate(words_fn(t), axis=1)
            tbl = pltpu.bitcast(words, BF16)
            tbls[g].append(tbl)
            rsels[g] = rsels[g] + jnp.where(tok_of_row == t, _dot_nt(xxs[g], tbl[:, :HALF]), 0.0)
        between(t)
    for g, (first, _) in enumerate(groups):
        tok = pl.ds(first, tb)
        coefs = []
        for c in range(2 * PEER_PICKS // LANES):
            cols = slice(c * LANES, (c + 1) * LANES)
            part = jnp.where(lane_even, rsels[g][:tb, cols], rsels[g][tb:, cols])
            hid = part + jnp.where(lane_even, pltpu.roll(part, LANES - 1, 1), pltpu.roll(part, 1, 1))
            coefs.append(gate_ref[tok, cols] * _gelu(hid))
        coef = jnp.concatenate(coefs, axis=1)
        cc = jnp.concatenate([jnp.where(even2, coef, 0.0), jnp.where(even2, 0.0, coef)],
                             axis=0).astype(BF16)
        ysel = jnp.zeros((2 * tb, HALF), F32)
        for t in range(tb):
            ysel = ysel + jnp.where(
                tok_of_row == t, jnp.dot(cc, tbls[g][t][:, HALF:], preferred_element_type=F32), 0.0)
        y = jnp.concatenate([ysel[:tb], ysel[tb:]], axis=1)
        o_ref[tok, :] = res_ref[tok, :] + y


def _peer_apply_kernel(idx0_ref, idx1_ref, hn_ref, gate_ref, res_ref, tab_ref, st_ref, o_ref,
                       buf_ref, sem_ref):
    step = pl.program_id(0)
    nsteps = pl.num_programs(0)
    tb = APPLY_TOKENS

    def issue_token(idx_ref, first_token, slot, t):
        for j in range(PEER_PICKS):
            _row_copy(tab_ref, buf_ref, sem_ref, idx_ref[first_token + t, j], slot,
                      t * PEER_PICKS + j).start(priority=j % 2)

    def wait(slot):
        pltpu.make_async_copy(buf_ref.at[slot], buf_ref.at[slot], sem_ref.at[slot]).wait()

    def copied_words(slot, t):
        return [buf_ref[slot, c, pl.ds(t * PEER_PICKS, PEER_PICKS), :] for c in range(CHUNKS)]

    compute = functools.partial(_apply_groups, hn_ref, gate_ref, res_ref, o_ref)

    @pl.when(step == 0)
    def _():
        for t in range(tb):
            issue_token(idx0_ref, 0, 0, t)

    wait(0)
    compute([(0, functools.partial(copied_words, 0)),
             (2 * tb, functools.partial(_staged_words, st_ref, 0))],
            functools.partial(issue_token, idx0_ref, tb, 1))
    wait(1)
    compute([(tb, functools.partial(copied_words, 1)),
             (3 * tb, functools.partial(_staged_words, st_ref, 1))],
            functools.partial(issue_token, idx1_ref, 0, 0))

    @pl.when(step == nsteps - 1)
    def _():
        wait(0)


STAGED_BLOCK_ROWS = 2 * APPLY_TOKENS * PEER_PICKS * CHUNKS


def peer_apply(idx, hn, gate2, acc, table, staged, first_token, ntokens):
    n, d = hn.shape
    tb = STEP_TOKENS
    nsteps = ntokens // tb
    s0 = first_token // tb
    smem = pltpu.SMEM
    return pl.pallas_call(
        _peer_apply_kernel,
        grid=(nsteps,),
        in_specs=[
            pl.BlockSpec((tb, PEER_PICKS), lambda i: (s0 + i, 0), memory_space=smem),
            pl.BlockSpec((tb, PEER_PICKS), lambda i: (s0 + jnp.minimum(i + 1, nsteps - 1), 0),
                         memory_space=smem),
            pl.BlockSpec((tb, d), lambda i: (s0 + i, 0)),
            pl.BlockSpec((tb, 2 * PEER_PICKS), lambda i: (s0 + i, 0)),
            pl.BlockSpec((tb, d), lambda i: (s0 + i, 0)),
            pl.BlockSpec(memory_space=pl.ANY),
            pl.BlockSpec((STAGED_BLOCK_ROWS, LANES), lambda i: (i, 0)),
        ],
        out_specs=pl.BlockSpec((tb, d), lambda i: (s0 + i, 0)),
        out_shape=jax.ShapeDtypeStruct((n, d), F32),
        input_output_aliases={4: 0},
        scratch_shapes=[pltpu.VMEM((2, CHUNKS, APPLY_TOKENS * PEER_PICKS, LANES), jnp.int32),
                        pltpu.SemaphoreType.DMA((2,))],
        compiler_params=pltpu.CompilerParams(
            dimension_semantics=("arbitrary",), vmem_limit_bytes=VMEM_LIMIT),
        name="peer_apply",
    )(idx, idx, hn, gate2, acc, table.reshape(table.shape[0], CHUNKS, 1, LANES), staged)


SC_CORES = 2
SC_SUBCORES = 16
SC_CHUNK = 32


def sc_gather_rows(table, idx, *, throttle):
    b = idx.shape[0]
    nw = SC_CORES * SC_SUBCORES
    per_w = b // nw
    nchunks = per_w // SC_CHUNK
    assert per_w * nw == b and nchunks * SC_CHUNK == per_w and nchunks % 2 == 0
    mesh = plsc.VectorSubcoreMesh(core_axis_name="c", subcore_axis_name="s")

    @functools.partial(
        pl.kernel, mesh=mesh,
        out_type=jax.ShapeDtypeStruct((b,) + table.shape[1:], table.dtype),
        scratch_types=[
            pltpu.VMEM((per_w,), jnp.int32),
            pltpu.VMEM((SC_CHUNK,) + table.shape[1:], table.dtype),
            pltpu.VMEM((SC_CHUNK,) + table.shape[1:], table.dtype),
            pltpu.SemaphoreType.DMA,
            pltpu.SemaphoreType.DMA,
        ],
        name="sc_gather_rows",
    )
    def gather(tab_hbm, idx_hbm, out_hbm, idx_v, rows0, rows1, sem0, sem1):
        wid = lax.axis_index("s") * SC_CORES + lax.axis_index("c")
        base = wid * per_w
        pltpu.sync_copy(idx_hbm.at[pl.ds(base, per_w)], idx_v)

        def start(chunk, rows, sem):
            off = pl.multiple_of(chunk * SC_CHUNK, SC_CHUNK)
            pltpu.async_copy(tab_hbm.at[idx_v.at[pl.ds(off, SC_CHUNK)]], rows, sem)

        def finish(chunk, rows, sem):
            off = pl.multiple_of(chunk * SC_CHUNK, SC_CHUNK)
            pltpu.make_async_copy(tab_hbm.at[idx_v.at[pl.ds(off, SC_CHUNK)]], rows, sem).wait()
            pltpu.sync_copy(rows, out_hbm.at[pl.ds(base + off, SC_CHUNK)])

        if throttle:
            @pl.loop(0, nchunks, step=2)
            def _(i):
                start(i, rows0, sem0)
                finish(i, rows0, sem0)
                start(i + 1, rows1, sem1)
                finish(i + 1, rows1, sem1)
        else:
            start(0, rows0, sem0)

            @pl.loop(0, nchunks, step=2)
            def _(i):
                start(i + 1, rows1, sem1)
                finish(i, rows0, sem0)

                @pl.when(i + 2 < nchunks)
                def _():
                    start(i + 2, rows0, sem0)

                finish(i + 1, rows1, sem1)

    return gather(table, idx)


PEER_GROUP = 4096
PEER_FIRST_GROUP = 2048


def _group_bounds(n):
    first = min(PEER_FIRST_GROUP, n)
    bounds = [0, first] + ([2 * first] if n >= 2 * first else [])
    while bounds[-1] < n:
        bounds.append(min(bounds[-1] + PEER_GROUP, n))
    return bounds


def peer_layer(x, norm_g, w_query, sub_keys, table):
    n = x.shape[0]
    hn, scores = peer_scores(x, norm_g, w_query, sub_keys)
    bounds = _group_bounds(n)
    ngroups = len(bounds) - 1

    def staged_indices(src, lo, hi):
        steps = src[lo:hi].reshape((hi - lo) // STEP_TOKENS, STEP_TOKENS, PEER_PICKS)
        return steps[:, STEP_TOKENS // 2:].reshape(-1)

    idx0, gate0 = peer_topk(scores, 0, bounds[1])
    first = staged_indices(idx0, 0, bounds[1])
    if ngroups > 1:
        scores, first = lax.optimization_barrier((scores, first))
        idx1, gate1 = peer_topk(scores, bounds[1], n - bounds[1])
        idx, gate = jnp.concatenate([idx0, idx1]), jnp.concatenate([gate0, gate1])
    else:
        idx, gate = idx0, gate0
    gate2 = jnp.repeat(gate, 2, axis=1)

    acc = x
    pending = [first] + [staged_indices(idx, bounds[g], bounds[g + 1]) for g in range(1, min(2, ngroups))]
    for g in range(ngroups):
        staged = sc_gather_rows(table, pending.pop(0), throttle=g > 0).reshape(-1, LANES)
        acc = peer_apply(idx, hn, gate2, acc, table, staged, bounds[g], bounds[g + 1] - bounds[g])
        if g + 2 < ngroups:
            nxt, acc = lax.optimization_barrier(
                (staged_indices(idx, bounds[g + 2], bounds[g + 3]), acc))
            pending.append(nxt)
    return acc


def attention_layer(x, batch, seq, norm_g, w_in, a_q_gain, a_k_gain, b_q_gain, b_k_gain,
                    lam_q1, lam_k1, lam_q2, lam_k2, b_sub_gain, w_out, lambda_init):
    proj = norm_matmul(x, norm_g, w_in)
    na = A_HEADS // 2
    tile2 = lambda g: jnp.concatenate([g, g])
    out_a = pair_attention(proj, batch, seq, "dilated", 0, na, 2 * na, na,
                           tile2(a_q_gain), tile2(a_k_gain))
    out_b = pair_attention(proj, batch, seq, "diff", 3 * na, 3 * na + B_HEADS, 3 * na + 2 * B_HEADS,
                           B_HEADS, b_q_gain.reshape(-1), b_k_gain.reshape(-1),
                           extras=(lam_q1, lam_k1, lam_q2, lam_k2, b_sub_gain),
                           lambda_init=lambda_init)
    return out_proj(x, out_a, out_b, w_out)


def kernel(x, attn_norm_g, attn_w_in, a_q_gain, a_k_gain, b_q_gain, b_k_gain, lam_q1, lam_k1,
           lam_q2, lam_k2, b_sub_gain, attn_w_out, sgu_norm_g, sgu_w_in, sgu_v_gain, sgu_w_spatial,
           sgu_b_spatial, sgu_w_out, ffn_norm_g, peer_w_query, peer_sub_keys, peer_down, peer_up):
    batch, seq, d = x.shape
    depth = ffn_norm_g.shape[0]
    h = x.reshape(batch * seq, d)
    tables = [pack_expert_table(peer_down[layer], peer_up[layer]) for layer in range(depth)]
    for layer in range(depth):
        i = layer // 2
        if layer % 2 == 0:
            lambda_init = 0.8 - 0.6 * math.exp(-0.3 * layer)
            h = attention_layer(h, batch, seq, attn_norm_g[i], attn_w_in[i], a_q_gain[i],
                                a_k_gain[i], b_q_gain[i], b_k_gain[i], lam_q1[i], lam_k1[i],
                                lam_q2[i], lam_k2[i], b_sub_gain[i], attn_w_out[i], lambda_init)
        else:
            h = sgu_layer(h, sgu_norm_g[i], sgu_w_in[i], sgu_v_gain[i], sgu_w_spatial[i],
                          sgu_b_spatial[i], sgu_w_out[i])
        h = peer_layer(h, ffn_norm_g[layer], peer_w_query[layer], peer_sub_keys[layer],
                       tables[layer])
    return h.reshape(batch, seq, d)
```
